```python
import jax, jax.numpy as jnp
from jax import lax
import numpy as np

D_MODEL = 1024
BATCH = 8
SEQ = 8192
DEPTH = 2
DEC_BATCH = 4
DEC_SEQ = 4096
PAST_LEN = 128

GRID_W = 64
POOL_WINDOWS = (2, 4, 8, 16)
N_POOL_GROUPS = 4
POOL_WIDTH = D_MODEL // 2
POOL_GROUP = POOL_WIDTH // N_POOL_GROUPS
N_HEADS = 8
HEAD_DIM = 64
ATTN_WIDTH = N_HEADS * HEAD_DIM
WIN_ROWS = 8
WIN_COLS = 16
CONV_WIDTH = D_MODEL // 2
CONV_K = 3
N_BRANCH = 3
IN_COLS = POOL_WIDTH + 3 * ATTN_WIDTH + 3 * CONV_WIDTH + N_BRANCH * D_MODEL
N_EXPERTS = 32
TOP_K = 4
D_EXPERT = D_MODEL
SWIGLU_ALPHA = 1.702
SWIGLU_LIMIT = 7.0
MOE_BLOCK = 256
RMS_EPS = 1e-6
N_MOD = 6

kernel_name = 'hybrid_pool_natten_shortconv_moe_encoder'


def rms_norm(x, g):
    xf = x.astype(jnp.float32)
    y = xf * lax.rsqrt(jnp.mean(xf * xf, axis=-1, keepdims=True) + RMS_EPS)
    return (y * g.astype(jnp.float32)).astype(x.dtype)


def pool_mixer(u, pool_w, pool_scale):
    B, L, _ = u.shape
    ug = u.astype(jnp.float32).reshape(B, L, N_POOL_GROUPS, POOL_GROUP)
    cs = jnp.concatenate([jnp.zeros((B, 1, N_POOL_GROUPS, POOL_GROUP), jnp.float32),
                          jnp.cumsum(ug, axis=1)], axis=1)
    t = jnp.arange(L)[:, None]
    win = jnp.array(POOL_WINDOWS, dtype=jnp.int32)[None, :]
    lo = jnp.clip(t - win // 2, 0, L)
    hi = jnp.clip(t - win // 2 + win, 0, L)
    grp = jnp.arange(N_POOL_GROUPS)[None, :]
    s = cs[:, hi, grp] - cs[:, lo, grp]
    mean = s / (hi - lo).astype(jnp.float32)[None, :, :, None]
    m = (mean - ug).astype(u.dtype)
    z = jnp.einsum('blgc,gcd->blgd', m, pool_w).reshape(B, L, POOL_WIDTH)
    return z * pool_scale


def neighborhood_attention(q, k, v, rpb):
    B, L, _ = q.shape
    rows = L // GRID_W
    wr = min(WIN_ROWS, rows)
    qg = q.reshape(B, rows, GRID_W, N_HEADS, HEAD_DIM)
    kg = k.reshape(B, rows, GRID_W, N_HEADS, HEAD_DIM)
    vg = v.reshape(B, rows, GRID_W, N_HEADS, HEAD_DIM)
    r_all = jnp.arange(rows)
    row_start = jnp.clip(r_all - wr // 2, 0, rows - wr)
    cols = jnp.arange(GRID_W)
    col_start = jnp.clip(cols - WIN_COLS // 2, 0, GRID_W - WIN_COLS)
    col_idx = col_start[:, None] + jnp.arange(WIN_COLS)[None, :]
    dcol = col_idx - cols[:, None]
    rpb_c = rpb[:, :, dcol + WIN_COLS - 1]
    scale = HEAD_DIM ** -0.5

    def row_block(args):
        r, rs = args
        q_r = lax.dynamic_index_in_dim(qg, r, axis=1, keepdims=False)
        k_rows = lax.dynamic_slice_in_dim(kg, rs, wr, axis=1)
        v_rows = lax.dynamic_slice_in_dim(vg, rs, wr, axis=1)
        k_win = k_rows[:, :, col_idx]
        v_win = v_rows[:, :, col_idx]
        drow = rs + jnp.arange(wr) - r
        bias = rpb_c[:, drow + WIN_ROWS - 1].transpose(0, 2, 1, 3)
        s = (jnp.einsum('bqhd,biqjhd->bhqij', q_r, k_win).astype(jnp.float32) * scale
             + bias[None].astype(jnp.float32))
        p = jax.nn.softmax(s.reshape(B, N_HEADS, GRID_W, wr * WIN_COLS), axis=-1)
        p = p.reshape(s.shape).astype(v.dtype)
        return jnp.einsum('bhqij,biqjhd->bqhd', p, v_win)

    o = lax.map(row_block, (r_all, row_start))
    return o.transpose(1, 0, 2, 3, 4).reshape(B, L, ATTN_WIDTH)


def conv_mixer(bg, cg, xc, conv_w):
    vv = cg * xc
    y = lax.conv_general_dilated(vv, conv_w[:, None, :].astype(vv.dtype), window_strides=(1,),
                                 padding=((CONV_K // 2, CONV_K // 2),),
                                 dimension_numbers=('NWC', 'WIO', 'NWC'),
                                 feature_group_count=CONV_WIDTH)
    return bg * y


def token_mixer(h, w_in, pool_w, pool_scale, rpb, conv_w, w_branch, w_out):
    B, L, _ = h.shape
    z = h @ w_in
    sizes = [POOL_WIDTH, ATTN_WIDTH, ATTN_WIDTH, ATTN_WIDTH,
             CONV_WIDTH, CONV_WIDTH, CONV_WIDTH, N_BRANCH * D_MODEL]
    cuts = [int(c) for c in np.cumsum(sizes)[:-1]]
    u, q, k, v, bg, cg, xc, gl = jnp.split(z, cuts, axis=-1)
    pa = pool_mixer(u, pool_w, pool_scale)
    pb = neighborhood_attention(q, k, v, rpb)
    pc = conv_mixer(bg, cg, xc, conv_w)
    g = jax.nn.sigmoid(gl.reshape(B, L, N_BRANCH, D_MODEL))
    merged = (g[:, :, 0] * (pa @ w_branch[0])
              + g[:, :, 1] * (pb @ w_branch[1])
              + g[:, :, 2] * (pc @ w_branch[2]))
    return merged @ w_out


def clamped_swiglu(gu):
    gate, up = gu[..., :D_EXPERT], gu[..., D_EXPERT:]
    gate = jnp.minimum(gate, SWIGLU_LIMIT)
    up = jnp.clip(up, -SWIGLU_LIMIT, SWIGLU_LIMIT)
    return (up + 1) * (gate * jax.nn.sigmoid(gate * SWIGLU_ALPHA))


def moe(h, router_w, router_b, w_gu, b_gu, w_down, b_down):
    B, L, D = h.shape
    N = B * L
    NK = N * TOP_K
    xf = h.reshape(N, D)
    logits = (xf @ router_w + router_b).astype(jnp.float32)
    top_v, top_i = lax.top_k(logits, TOP_K)
    gates = jax.nn.softmax(top_v, axis=-1)
    e_flat = top_i.reshape(-1).astype(jnp.int32)
    tok_flat = jnp.arange(NK, dtype=jnp.int32) // TOP_K
    g_flat = gates.reshape(-1)
    order = jnp.argsort(e_flat)
    e_sorted = e_flat[order]
    counts = jnp.bincount(e_flat, length=N_EXPERTS).astype(jnp.int32)
    padded = (counts + MOE_BLOCK - 1) // MOE_BLOCK * MOE_BLOCK
    start = jnp.cumsum(counts) - counts
    pend = jnp.cumsum(padded)
    pstart = pend - padded
    dest = pstart[e_sorted] + (jnp.arange(NK, dtype=jnp.int32) - start[e_sorted])
    P = NK + N_EXPERTS * MOE_BLOCK
    n_blk = P // MOE_BLOCK
    slot_tok = jnp.full((P,), N, jnp.int32).at[dest].set(tok_flat[order])
    slot_gate = jnp.zeros((P,), jnp.float32).at[dest].set(g_flat[order])
    blk_expert = jnp.minimum(jnp.searchsorted(pend, jnp.arange(n_blk, dtype=jnp.int32) * MOE_BLOCK,
                                              side='right'), N_EXPERTS - 1)
    x_pad = jnp.concatenate([xf, jnp.zeros((1, D), xf.dtype)], axis=0)

    def expert_block(args):
        toks, gt, e = args
        xb = x_pad[toks]
        act = clamped_swiglu(xb @ w_gu[e] + b_gu[e])
        yb = act @ w_down[e] + b_down[e]
        return yb * gt[:, None].astype(yb.dtype)

    yb = lax.map(expert_block, (slot_tok.reshape(n_blk, MOE_BLOCK),
                                slot_gate.reshape(n_blk, MOE_BLOCK), blk_expert))
    out = jnp.zeros((N + 1, D), h.dtype).at[slot_tok].add(yb.reshape(P, D).astype(h.dtype))
    return out[:N].reshape(B, L, D)


def trunk(x, c, ada_w, ada_b, norm_g, w_in, pool_w, pool_scale, rpb, conv_w, w_branch, w_out,
          router_w, router_b, expert_w_gu, expert_b_gu, expert_w_down, expert_b_down):
    for l in range(DEPTH):
        ada = (jax.nn.silu(c) @ ada_w[l] + ada_b[l])[:, None, :]
        sh1, sc1, gt1, sh2, sc2, gt2 = jnp.split(ada, N_MOD, axis=-1)
        h = rms_norm(x, norm_g[l, 0]) * (1 + sc1) + sh1
        mix = token_mixer(h, w_in[l], pool_w[l], pool_scale[l], rpb[l], conv_w[l], w_branch[l], w_out[l])
        x = x + gt1 * rms_norm(mix, norm_g[l, 1])
        h = rms_norm(x, norm_g[l, 2]) * (1 + sc2) + sh2
        ff = moe(h, router_w[l], router_b[l], expert_w_gu[l], expert_b_gu[l],
                 expert_w_down[l], expert_b_down[l])
        x = x + gt2 * rms_norm(ff, norm_g[l, 3])
    return x


def setup_inputs(seed: int = 0) -> dict:
    key = jax.random.key(seed)
    ks = jax.random.split(key, 20)
    f32 = jnp.float32
    nrm = lambda k, shape, s: jax.random.normal(k, shape, f32) * s
    return {
        'x_prompt': nrm(ks[0], (BATCH, SEQ, D_MODEL), 1.0),
        'x_sample': nrm(ks[1], (DEC_BATCH, DEC_SEQ, D_MODEL), 1.0),
        'c_prompt': nrm(ks[2], (BATCH, D_MODEL), 1.0),
        'c_sample': nrm(ks[3], (DEC_BATCH, D_MODEL), 1.0),
        'ada_w': nrm(ks[4], (DEPTH, D_MODEL, N_MOD * D_MODEL), 0.5 * D_MODEL ** -0.5),
        'ada_b': nrm(ks[5], (DEPTH, N_MOD * D_MODEL), 0.01),
        'norm_g': 1.0 + nrm(ks[6], (DEPTH, 4, D_MODEL), 0.05),
        'w_in': nrm(ks[7], (DEPTH, D_MODEL, IN_COLS), D_MODEL ** -0.5),
        'pool_w': nrm(ks[8], (DEPTH, N_POOL_GROUPS, POOL_GROUP, POOL_GROUP), POOL_GROUP ** -0.5),
        'pool_scale': 1.0 + nrm(ks[9], (DEPTH, POOL_WIDTH), 0.1),
        'rpb': nrm(ks[10], (DEPTH, N_HEADS, 2 * WIN_ROWS - 1, 2 * WIN_COLS - 1), 0.1),
        'conv_w': nrm(ks[11], (DEPTH, CONV_K, CONV_WIDTH), CONV_K ** -0.5),
        'w_branch': nrm(ks[12], (DEPTH, N_BRANCH, POOL_WIDTH, D_MODEL), POOL_WIDTH ** -0.5),
        'w_out': nrm(ks[13], (DEPTH, D_MODEL, D_MODEL), D_MODEL ** -0.5),
        'router_w': nrm(ks[14], (DEPTH, D_MODEL, N_EXPERTS), D_MODEL ** -0.5),
        'router_b': nrm(ks[15], (DEPTH, N_EXPERTS), 0.01),
        'expert_w_gu': nrm(ks[16], (DEPTH, N_EXPERTS, D_MODEL, 2 * D_EXPERT), D_MODEL ** -0.5),
        'expert_b_gu': nrm(ks[17], (DEPTH, N_EXPERTS, 2 * D_EXPERT), 0.01),
        'expert_w_down': nrm(ks[18], (DEPTH, N_EXPERTS, D_EXPERT, D_MODEL), D_EXPERT ** -0.5),
        'expert_b_down': nrm(ks[19], (DEPTH, N_EXPERTS, D_MODEL), 0.01),
    }


def reference(x_prompt, x_sample, c_prompt, c_sample, ada_w, ada_b, norm_g, w_in, pool_w, pool_scale,
              rpb, conv_w, w_branch, w_out, router_w, router_b, expert_w_gu, expert_b_gu,
              expert_w_down, expert_b_down):
    y_prompt = trunk(x_prompt, c_prompt, ada_w, ada_b, norm_g, w_in, pool_w, pool_scale, rpb, conv_w,
                     w_branch, w_out, router_w, router_b, expert_w_gu, expert_b_gu,
                     expert_w_down, expert_b_down)
    y_sample = trunk(x_sample, c_sample, ada_w, ada_b, norm_g, w_in, pool_w, pool_scale, rpb, conv_w,
                     w_branch, w_out, router_w, router_b, expert_w_gu, expert_b_gu,
                     expert_w_down, expert_b_down)
    return (y_prompt, y_sample)
```

```python
import functools

import numpy as np
import jax
import jax.numpy as jnp
from jax import lax
from jax.experimental import pallas as pl
from jax.experimental.pallas import tpu as pltpu

f32 = jnp.float32
bf16 = jnp.bfloat16
i32 = jnp.int32

D_MODEL = 1024
GRID_W = 64
POOL_WINDOWS = (2, 4, 8, 16)
POOL_GROUP = 128
POOL_WIDTH = 512
N_HEADS = 8
HEAD_DIM = 64
N_PAIRS = N_HEADS // 2
ATTN_WIDTH = 512
WIN_ROWS = 8
WIN_COLS = 16
CONV_WIDTH = 512
N_BRANCH = 3
IN_COLS = POOL_WIDTH + 3 * ATTN_WIDTH + 3 * CONV_WIDTH + N_BRANCH * D_MODEL
N_EXPERTS = 32
TOP_K = 4
D_EXPERT = D_MODEL
SWIGLU_ALPHA = 1.702
SWIGLU_LIMIT = 7.0
RMS_EPS = 1e-6
N_MOD = 6
NEG_BIG = -1e30

SEQ_PAD = 16
ROWS_PER_TILE = 4
TILE = ROWS_PER_TILE * GRID_W
HALO = 8
IN_TILE = 512
ROUTE_TILE = 512
SLOT_BLOCK = 256
MOVE_TILE = 256
VMEM_LIMIT = 56 * 1024 * 1024


def _cparams(sem):
    return pltpu.CompilerParams(dimension_semantics=sem, vmem_limit_bytes=VMEM_LIMIT)


def _tile_meta(seq_lens, tile):
    seq, pos, cnt = [], [], []
    sid = 0
    for nseq, length in seq_lens:
        nt = length // tile
        for _ in range(nseq):
            for p in range(nt):
                seq.append(sid)
                pos.append(p)
                cnt.append(nt)
            sid += 1
    return (jnp.asarray(np.array(seq, np.int32)), jnp.asarray(np.array(pos, np.int32)),
            jnp.asarray(np.array(cnt, np.int32)))


def _rms(x, g):
    ms = jnp.mean(x * x, axis=-1, keepdims=True)
    return x * lax.rsqrt(ms + RMS_EPS) * g


def _ada_kernel(c_ref, w_ref, b_ref, o_ref):
    c = c_ref[...]
    s = c * jax.nn.sigmoid(c)
    o_ref[...] = jnp.dot(s, w_ref[...], preferred_element_type=f32,
                         precision=lax.Precision.HIGHEST) + b_ref[...]


def _ada(c_all, ada_w, ada_b):
    depth = ada_w.shape[0]
    return pl.pallas_call(
        _ada_kernel,
        grid=(depth, N_MOD),
        in_specs=[pl.BlockSpec((SEQ_PAD, D_MODEL), lambda l, j: (0, 0)),
                  pl.BlockSpec((None, D_MODEL, D_MODEL), lambda l, j: (l, 0, j)),
                  pl.BlockSpec((None, 1, D_MODEL), lambda l, j: (l, 0, j))],
        out_specs=pl.BlockSpec((None, SEQ_PAD, D_MODEL), lambda l, j: (l, 0, j)),
        out_shape=jax.ShapeDtypeStruct((depth, SEQ_PAD, N_MOD * D_MODEL), f32),
        compiler_params=_cparams(("arbitrary", "arbitrary")),
        name="ada",
    )(c_all, ada_w, ada_b.reshape(depth, 1, N_MOD * D_MODEL))


_IN_SPLITS = (POOL_WIDTH, ATTN_WIDTH, ATTN_WIDTH, ATTN_WIDTH, CONV_WIDTH, CONV_WIDTH, CONV_WIDTH,
              N_BRANCH * D_MODEL)


def _inproj_kernel(seq_ref, x_ref, ada_ref, g_ref, w_ref, *out_refs):
    s = seq_ref[pl.program_id(0)]
    mod = ada_ref[pl.ds(s, 1), :]
    h = _rms(x_ref[...], g_ref[...]) * (1.0 + mod[:, D_MODEL:2 * D_MODEL]) + mod[:, 0:D_MODEL]
    h = h.astype(bf16)
    col = 0
    for o_ref in out_refs:
        width = o_ref.shape[1]
        for c0 in range(0, width, 512):
            o_ref[:, c0:c0 + 512] = jnp.dot(h, w_ref[:, col + c0:col + c0 + 512],
                                            preferred_element_type=f32).astype(bf16)
        col += width


def _inproj(x, ada_l, g, w_in_bf, seq_lens):
    n = x.shape[0]
    seq, _, _ = _tile_meta(seq_lens, IN_TILE)
    row = lambda i, s: (i, 0)
    const = lambda i, s: (0, 0)
    grid_spec = pltpu.PrefetchScalarGridSpec(
        num_scalar_prefetch=1,
        grid=(n // IN_TILE,),
        in_specs=[pl.BlockSpec((IN_TILE, D_MODEL), row),
                  pl.BlockSpec((SEQ_PAD, N_MOD * D_MODEL), const),
                  pl.BlockSpec((1, D_MODEL), const),
                  pl.BlockSpec((D_MODEL, IN_COLS), const, pipeline_mode=pl.Buffered(1))],
        out_specs=[pl.BlockSpec((IN_TILE, w), row) for w in _IN_SPLITS],
    )
    return pl.pallas_call(
        _inproj_kernel,
        grid_spec=grid_spec,
        out_shape=[jax.ShapeDtypeStruct((n, w), bf16) for w in _IN_SPLITS],
        compiler_params=_cparams(("arbitrary",)),
        name="inproj",
    )(seq, x, ada_l, g, w_in_bf)


def _mixer_kernel(seq_ref, pos_ref, nt_ref,
                  x_ref, u_ref, up_ref, un_ref, q_ref, kp_ref, kc_ref, kn_ref, vp_ref, vc_ref, vn_ref,
                  b_ref, c_ref, cp_ref, cn_ref, xc_ref, xcp_ref, xcn_ref, gl_ref,
                  ada_ref, ng_ref, poolw_ref, pscale_ref, bias_ref, convw_ref, wb_ref, wout_ref,
                  rw_ref, rb_ref,
                  xo_ref, h2_ref, lg_ref,
                  uext, vext, kbuf, vbuf, pb_ref):
    i = pl.program_id(0)
    s = seq_ref[i]
    pos = pos_ref[i]
    nt = nt_ref[i]
    first = pos == 0
    last = pos == nt - 1
    seq_len = nt * TILE

    uext[HALO:HALO + TILE, :] = u_ref[...].astype(f32)
    uext[0:HALO, :] = jnp.where(first, 0.0, up_ref[...].astype(f32)[GRID_W - HALO:GRID_W, :])
    uext[HALO + TILE:2 * HALO + TILE, :] = jnp.where(last, 0.0, un_ref[...].astype(f32)[0:HALO, :])

    vext[HALO:HALO + TILE, :] = c_ref[...].astype(f32) * xc_ref[...].astype(f32)
    vprev = cp_ref[...].astype(f32) * xcp_ref[...].astype(f32)
    vnext = cn_ref[...].astype(f32) * xcn_ref[...].astype(f32)
    vext[0:HALO, :] = jnp.where(first, 0.0, vprev[GRID_W - HALO:GRID_W, :])
    vext[HALO + TILE:2 * HALO + TILE, :] = jnp.where(last, 0.0, vnext[0:HALO, :])

    kbuf[0:TILE, :] = kp_ref[...]
    kbuf[TILE:2 * TILE, :] = kc_ref[...]
    kbuf[2 * TILE:3 * TILE, :] = kn_ref[...]
    vbuf[0:TILE, :] = vp_ref[...]
    vbuf[TILE:2 * TILE, :] = vc_ref[...]
    vbuf[2 * TILE:3 * TILE, :] = vn_ref[...]

    lane = lax.broadcasted_iota(i32, (1, 2 * HEAD_DIM), 1)
    low = lane < HEAD_DIM
    rows = nt * ROWS_PER_TILE
    qscale = jnp.asarray(HEAD_DIM ** -0.5, bf16)

    def attn_row(j, carry):
        r = pos * ROWS_PER_TILE + j
        rs = jnp.clip(r - WIN_ROWS // 2, 0, rows - WIN_ROWS)
        si = r - rs
        koff = pl.multiple_of((rs - pos * ROWS_PER_TILE + ROWS_PER_TILE) * GRID_W, GRID_W)
        qoff = pl.multiple_of(j * GRID_W, GRID_W)
        for p in range(N_PAIRS):
            cols = slice(p * 2 * HEAD_DIM, (p + 1) * 2 * HEAD_DIM)
            qp = q_ref[pl.ds(qoff, GRID_W), cols] * qscale
            zero = jnp.zeros_like(qp)
            qs = jnp.concatenate([jnp.where(low, qp, zero), jnp.where(low, zero, qp)], axis=0)
            kw = kbuf[pl.ds(koff, WIN_ROWS * GRID_W), cols]
            vw = vbuf[pl.ds(koff, WIN_ROWS * GRID_W), cols]
            sc = lax.dot_general(qs, kw, (((1,), (1,)), ((), ())), preferred_element_type=f32)
            sc = sc + bias_ref[si, p]
            m = jnp.max(sc, axis=-1, keepdims=True)
            e = jnp.exp(sc - m)
            l = jnp.sum(e, axis=-1, keepdims=True)
            o = jnp.dot(e.astype(bf16), vw, preferred_element_type=f32) / l
            pb_ref[pl.ds(qoff, GRID_W), cols] = jnp.where(low, o[0:GRID_W], o[GRID_W:2 * GRID_W]).astype(bf16)
        return carry

    lax.fori_loop(0, ROWS_PER_TILE, attn_row, 0)

    t = pos * TILE + lax.broadcasted_iota(i32, (TILE, 1), 0)
    pa_parts = []
    for g, w in enumerate(POOL_WINDOWS):
        cols = slice(g * POOL_GROUP, (g + 1) * POOL_GROUP)
        half = w // 2
        acc = uext[HALO - half:HALO - half + TILE, cols]
        for d in range(-half + 1, half):
            acc = acc + uext[HALO + d:HALO + d + TILE, cols]
        cnt = (jnp.minimum(t + half, seq_len) - jnp.maximum(t - half, 0)).astype(f32)
        m = acc / cnt - uext[HALO:HALO + TILE, cols]
        z = jnp.dot(m.astype(bf16), poolw_ref[g], preferred_element_type=f32)
        pa_parts.append(z * pscale_ref[:, cols])
    pa = jnp.concatenate(pa_parts, axis=1).astype(bf16)

    y = (convw_ref[0:1, :] * vext[HALO - 1:HALO - 1 + TILE, :]
         + convw_ref[1:2, :] * vext[HALO:HALO + TILE, :]
         + convw_ref[2:3, :] * vext[HALO + 1:HALO + 1 + TILE, :])
    pc = (b_ref[...].astype(f32) * y).astype(bf16)

    branches = (pa, pb_ref[...], pc)
    merged = None
    for n in range(N_BRANCH):
        proj = jnp.dot(branches[n], wb_ref[n], preferred_element_type=f32)
        gate = jax.nn.sigmoid(gl_ref[:, n * D_MODEL:(n + 1) * D_MODEL].astype(f32))
        merged = gate * proj if merged is None else merged + gate * proj
    mix = jnp.dot(merged.astype(bf16), wout_ref[...], preferred_element_type=f32)

    mod = ada_ref[pl.ds(s, 1), :]
    gt1 = mod[:, 2 * D_MODEL:3 * D_MODEL]
    sh2 = mod[:, 3 * D_MODEL:4 * D_MODEL]
    sc2 = mod[:, 4 * D_MODEL:5 * D_MODEL]
    xn = x_ref[...] + gt1 * _rms(mix, ng_ref[1:2, :])
    xo_ref[...] = xn
    h2 = _rms(xn, ng_ref[2:3, :]) * (1.0 + sc2) + sh2
    h2_ref[...] = h2
    lg_ref[...] = lax.dot_general(rw_ref[...], h2, (((1,), (1,)), ((), ())), preferred_element_type=f32,
                                  precision=lax.Precision.HIGHEST) + rb_ref[...]


def _mixer(x, zs, ada_l, ng, poolw_bf, pscale, bias_tab, convw, wb_bf, wout_bf, rw_t, rb, seq_lens):
    n = x.shape[0]
    zu, zq, zk, zv, zb, zc, zx, zg = zs
    seq, pos, cnt = _tile_meta(seq_lens, TILE)
    per = TILE // GRID_W

    cur = lambda i, sq, ps, ct: (i, 0)
    prev_t = lambda i, sq, ps, ct: (jnp.where(ps[i] > 0, i - 1, i), 0)
    next_t = lambda i, sq, ps, ct: (jnp.where(ps[i] < ct[i] - 1, i + 1, i), 0)
    prev_h = lambda i, sq, ps, ct: (jnp.where(ps[i] > 0, i * per - 1, i * per), 0)
    next_h = lambda i, sq, ps, ct: (jnp.where(ps[i] < ct[i] - 1, (i + 1) * per, i * per), 0)
    c2 = lambda i, sq, ps, ct: (0, 0)
    c3 = lambda i, sq, ps, ct: (0, 0, 0)
    c4 = lambda i, sq, ps, ct: (0, 0, 0, 0)

    tile512 = lambda im: pl.BlockSpec((TILE, 512), im)
    halo512 = lambda im: pl.BlockSpec((GRID_W, 512), im)
    once = pl.Buffered(1)
    in_specs = [
        pl.BlockSpec((TILE, D_MODEL), cur),
        tile512(cur), halo512(prev_h), halo512(next_h),
        tile512(cur),
        tile512(prev_t), tile512(cur), tile512(next_t),
        tile512(prev_t), tile512(cur), tile512(next_t),
        tile512(cur),
        tile512(cur), halo512(prev_h), halo512(next_h),
        tile512(cur), halo512(prev_h), halo512(next_h),
        pl.BlockSpec((TILE, N_BRANCH * D_MODEL), cur),
        pl.BlockSpec((SEQ_PAD, N_MOD * D_MODEL), c2),
        pl.BlockSpec((4, D_MODEL), c2),
        pl.BlockSpec((4, POOL_GROUP, POOL_GROUP), c3),
        pl.BlockSpec((1, POOL_WIDTH), c2),
        pl.BlockSpec((WIN_ROWS, N_PAIRS, 2 * GRID_W, WIN_ROWS * GRID_W), c4, pipeline_mode=once),
        pl.BlockSpec((3, CONV_WIDTH), c2),
        pl.BlockSpec((N_BRANCH, POOL_WIDTH, D_MODEL), c3, pipeline_mode=once),
        pl.BlockSpec((D_MODEL, D_MODEL), c2, pipeline_mode=once),
        pl.BlockSpec((N_EXPERTS, D_MODEL), c2),
        pl.BlockSpec((N_EXPERTS, 1), c2),
    ]
    out_specs = [
        pl.BlockSpec((TILE, D_MODEL), cur),
        pl.BlockSpec((TILE, D_MODEL), cur),
        pl.BlockSpec((N_EXPERTS, TILE), lambda i, sq, ps, ct: (0, i)),
    ]
    grid_spec = pltpu.PrefetchScalarGridSpec(
        num_scalar_prefetch=3,
        grid=(n // TILE,),
        in_specs=in_specs,
        out_specs=out_specs,
        scratch_shapes=[pltpu.VMEM((TILE + 2 * HALO, POOL_WIDTH), f32),
                        pltpu.VMEM((TILE + 2 * HALO, CONV_WIDTH), f32),
                        pltpu.VMEM((3 * TILE, ATTN_WIDTH), bf16),
                        pltpu.VMEM((3 * TILE, ATTN_WIDTH), bf16),
                        pltpu.VMEM((TILE, ATTN_WIDTH), bf16)],
    )
    return pl.pallas_call(
        _mixer_kernel,
        grid_spec=grid_spec,
        out_shape=[jax.ShapeDtypeStruct((n, D_MODEL), f32),
                   jax.ShapeDtypeStruct((n, D_MODEL), f32),
                   jax.ShapeDtypeStruct((N_EXPERTS, n), f32)],
        compiler_params=_cparams(("arbitrary",)),
        name="mixer",
    )(seq, pos, cnt,
      x, zu, zu, zu, zq, zk, zk, zk, zv, zv, zv, zb, zc, zc, zc, zx, zx, zx, zg,
      ada_l, ng, poolw_bf, pscale, bias_tab, convw, wb_bf, wout_bf, rw_t, rb)


def _bias_table(rpb_l):
    c = np.arange(GRID_W)[:, None]
    kc = np.arange(GRID_W)[None, :]
    cs = np.clip(c - WIN_COLS // 2, 0, GRID_W - WIN_COLS)
    valid = (kc >= cs) & (kc < cs + WIN_COLS)
    didx = np.clip(kc - c + WIN_COLS - 1, 0, 2 * WIN_COLS - 2)
    shift = np.arange(WIN_ROWS)[:, None]
    krow = np.arange(WIN_ROWS)[None, :]
    ridx = WIN_ROWS - 1 - shift + krow
    tab = rpb_l[:, ridx][:, :, :, didx]
    tab = jnp.where(valid[None, None, None], tab, NEG_BIG)
    tab = tab.transpose(1, 0, 3, 2, 4)
    return tab.reshape(WIN_ROWS, N_PAIRS, 2 * GRID_W, WIN_ROWS * GRID_W).astype(f32)


def _router_kernel(lg_ref, ti_ref, rk_ref, gt_ref, cnt_ref, base):
    @pl.when(pl.program_id(0) == 0)
    def _():
        base[...] = jnp.zeros_like(base)

    l = lg_ref[...]
    eio = lax.broadcasted_iota(i32, l.shape, 0)
    vals, hots = [], []
    for k in range(TOP_K):
        m = jnp.max(l, axis=0, keepdims=True)
        idx = jnp.min(jnp.where(l == m, eio, N_EXPERTS), axis=0, keepdims=True)
        hot = eio == idx
        l = jnp.where(hot, -jnp.inf, l)
        vals.append(m)
        hots.append(hot)
        ti_ref[k:k + 1, :] = idx
    ex = [jnp.exp(v - vals[0]) for v in vals]
    den = ex[0] + ex[1] + ex[2] + ex[3]
    for k in range(TOP_K):
        gt_ref[k:k + 1, :] = ex[k] / den

    hot_all = jnp.zeros(l.shape, f32)
    for hot in hots:
        hot_all = hot_all + hot.astype(f32)
    rr = lax.broadcasted_iota(i32, (ROUTE_TILE, ROUTE_TILE), 0)
    cc = lax.broadcasted_iota(i32, (ROUTE_TILE, ROUTE_TILE), 1)
    tri = (rr <= cc).astype(bf16)
    pref = jnp.dot(hot_all.astype(bf16), tri, preferred_element_type=f32)
    b = base[:, 0:1]
    tot = b + pref - 1.0
    for k in range(TOP_K):
        rk_ref[k:k + 1, :] = jnp.sum(jnp.where(hots[k], tot, 0.0), axis=0, keepdims=True).astype(i32)
    newb = jnp.broadcast_to(b + pref[:, ROUTE_TILE - 1:ROUTE_TILE], base.shape)
    base[...] = newb
    cnt_ref[...] = newb


def _router(logits_t):
    n = logits_t.shape[1]
    blk = lambda i: (0, i)
    return pl.pallas_call(
        _router_kernel,
        grid=(n // ROUTE_TILE,),
        in_specs=[pl.BlockSpec((N_EXPERTS, ROUTE_TILE), blk)],
        out_specs=[pl.BlockSpec((TOP_K, ROUTE_TILE), blk),
                   pl.BlockSpec((TOP_K, ROUTE_TILE), blk),
                   pl.BlockSpec((TOP_K, ROUTE_TILE), blk),
                   pl.BlockSpec((N_EXPERTS, 128), lambda i: (0, 0))],
        out_shape=[jax.ShapeDtypeStruct((TOP_K, n), i32),
                   jax.ShapeDtypeStruct((TOP_K, n), i32),
                   jax.ShapeDtypeStruct((TOP_K, n), f32),
                   jax.ShapeDtypeStruct((N_EXPERTS, 128), f32)],
        scratch_shapes=[pltpu.VMEM((N_EXPERTS, 128), f32)],
        compiler_params=_cparams(("arbitrary",)),
        name="router",
    )(logits_t)


def _row_copy(src, src_row, dst, dst_row, sem):
    return pltpu.make_async_copy(src.at[pl.ds(src_row, 1), :], dst.at[pl.ds(dst_row, 1), :], sem)


def _dispatch_kernel(dest_ref, h_ref, init_ref, xs_ref, sem):
    del init_ref

    def issue(t, carry):
        for k in range(TOP_K):
            _row_copy(h_ref, t, xs_ref, dest_ref[k, t], sem).start()
        return carry

    def drain(t, carry):
        for k in range(TOP_K):
            _row_copy(h_ref, t, xs_ref, dest_ref[k, t], sem).wait()
        return carry

    lax.fori_loop(0, MOVE_TILE, issue, 0)
    lax.fori_loop(0, MOVE_TILE, drain, 0)


def _dispatch(h2, dest, n_slots):
    n = h2.shape[0]
    init = jnp.zeros((n_slots, D_MODEL), f32)
    return pl.pallas_call(
        _dispatch_kernel,
        grid=(n // MOVE_TILE,),
        in_specs=[pl.BlockSpec((TOP_K, MOVE_TILE), lambda i: (0, i), memory_space=pltpu.SMEM),
                  pl.BlockSpec((MOVE_TILE, D_MODEL), lambda i: (i, 0)),
                  pl.BlockSpec(memory_space=pl.ANY)],
        out_specs=pl.BlockSpec(memory_space=pl.ANY),
        out_shape=jax.ShapeDtypeStruct((n_slots, D_MODEL), f32),
        scratch_shapes=[pltpu.SemaphoreType.DMA(())],
        input_output_aliases={2: 0},
        compiler_params=_cparams(("arbitrary",)),
        name="dispatch",
    )(dest, h2, init)


def _expert_kernel(be_ref, nu_ref, xs_ref, wgu_ref, bgu_ref, wd_ref, bd_ref, y_ref, wgu_bf, wd_bf, act):
    b = pl.program_id(0)
    e = be_ref[b]
    prev = be_ref[jnp.maximum(b - 1, 0)]

    @pl.when((b == 0) | (e != prev))
    def _():
        wgu_bf[...] = wgu_ref[...].astype(bf16)
        wd_bf[...] = wd_ref[...].astype(bf16)

    @pl.when(b < nu_ref[0])
    def _():
        x = xs_ref[...].astype(bf16)
        for c0 in range(0, D_EXPERT, 512):
            gate = jnp.dot(x, wgu_bf[:, c0:c0 + 512], preferred_element_type=f32) + bgu_ref[:, c0:c0 + 512]
            up = (jnp.dot(x, wgu_bf[:, D_EXPERT + c0:D_EXPERT + c0 + 512], preferred_element_type=f32)
                  + bgu_ref[:, D_EXPERT + c0:D_EXPERT + c0 + 512])
            gate = jnp.minimum(gate, SWIGLU_LIMIT)
            up = jnp.clip(up, -SWIGLU_LIMIT, SWIGLU_LIMIT)
            act[:, c0:c0 + 512] = ((up + 1.0) * (gate * jax.nn.sigmoid(gate * SWIGLU_ALPHA))).astype(bf16)
        y_ref[...] = jnp.dot(act[...], wd_bf[...], preferred_element_type=f32) + bd_ref[...]

    @pl.when(b >= nu_ref[0])
    def _():
        y_ref[...] = jnp.zeros_like(y_ref)


def _experts(xs, blk_expert, n_used, w_gu, b_gu, w_down, b_down):
    n_slots = xs.shape[0]
    wsel = lambda b, be, nu: (be[b], 0, 0)
    row = lambda b, be, nu: (b, 0)
    grid_spec = pltpu.PrefetchScalarGridSpec(
        num_scalar_prefetch=2,
        grid=(n_slots // SLOT_BLOCK,),
        in_specs=[pl.BlockSpec((SLOT_BLOCK, D_MODEL), row),
                  pl.BlockSpec((None, D_MODEL, 2 * D_EXPERT), wsel),
                  pl.BlockSpec((None, 1, 2 * D_EXPERT), wsel),
                  pl.BlockSpec((None, D_EXPERT, D_MODEL), wsel),
                  pl.BlockSpec((None, 1, D_MODEL), wsel)],
        out_specs=pl.BlockSpec((SLOT_BLOCK, D_MODEL), row),
        scratch_shapes=[pltpu.VMEM((D_MODEL, 2 * D_EXPERT), bf16),
                        pltpu.VMEM((D_EXPERT, D_MODEL), bf16),
                        pltpu.VMEM((SLOT_BLOCK, D_EXPERT), bf16)],
    )
    return pl.pallas_call(
        _expert_kernel,
        grid_spec=grid_spec,
        out_shape=jax.ShapeDtypeStruct((n_slots, D_MODEL), f32),
        compiler_params=_cparams(("arbitrary",)),
        name="experts",
    )(blk_expert, n_used, xs, w_gu, b_gu.reshape(N_EXPERTS, 1, 2 * D_EXPERT), w_down,
      b_down.reshape(N_EXPERTS, 1, D_MODEL))


def _combine_kernel(seq_ref, dest_ref, x_ref, gt_ref, ada_ref, ng_ref, y_ref, xo_ref, ybuf, sem):
    def issue(t, carry):
        for k in range(TOP_K):
            _row_copy(y_ref, dest_ref[k, t], ybuf.at[k], t, sem).start()
        return carry

    def drain(t, carry):
        for k in range(TOP_K):
            _row_copy(y_ref, dest_ref[k, t], ybuf.at[k], t, sem).wait()
        return carry

    lax.fori_loop(0, MOVE_TILE, issue, 0)
    lax.fori_loop(0, MOVE_TILE, drain, 0)

    ff = gt_ref[:, 0:1] * ybuf[0]
    for k in range(1, TOP_K):
        ff = ff + gt_ref[:, k:k + 1] * ybuf[k]
    s = seq_ref[pl.program_id(0)]
    gt2 = ada_ref[pl.ds(s, 1), 5 * D_MODEL:6 * D_MODEL]
    xo_ref[...] = x_ref[...] + gt2 * _rms(ff, ng_ref[3:4, :])


def _combine(x, y, dest, gates_t, ada_l, ng, seq_lens):
    n = x.shape[0]
    seq, _, _ = _tile_meta(seq_lens, MOVE_TILE)
    grid_spec = pltpu.PrefetchScalarGridSpec(
        num_scalar_prefetch=1,
        grid=(n // MOVE_TILE,),
        in_specs=[pl.BlockSpec((TOP_K, MOVE_TILE), lambda i, sq: (0, i), memory_space=pltpu.SMEM),
                  pl.BlockSpec((MOVE_TILE, D_MODEL), lambda i, sq: (i, 0)),
                  pl.BlockSpec((MOVE_TILE, TOP_K), lambda i, sq: (i, 0)),
                  pl.BlockSpec((SEQ_PAD, N_MOD * D_MODEL), lambda i, sq: (0, 0)),
                  pl.BlockSpec((4, D_MODEL), lambda i, sq: (0, 0)),
                  pl.BlockSpec(memory_space=pl.ANY)],
        out_specs=pl.BlockSpec((MOVE_TILE, D_MODEL), lambda i, sq: (i, 0)),
        scratch_shapes=[pltpu.VMEM((TOP_K, MOVE_TILE, D_MODEL), f32),
                        pltpu.SemaphoreType.DMA(())],
    )
    return pl.pallas_call(
        _combine_kernel,
        grid_spec=grid_spec,
        out_shape=jax.ShapeDtypeStruct((n, D_MODEL), f32),
        compiler_params=_cparams(("arbitrary",)),
        name="combine",
    )(seq, dest, x, gates_t, ada_l, ng, y)


def _slot_plan(counts, top_i, rank, n_slots):
    padded = (counts + SLOT_BLOCK - 1) // SLOT_BLOCK * SLOT_BLOCK
    pend = jnp.cumsum(padded)
    pstart = pend - padded
    dest = pstart[top_i] + rank
    nblk = n_slots // SLOT_BLOCK
    blk_expert = jnp.minimum(
        jnp.searchsorted(pend, jnp.arange(nblk, dtype=i32) * SLOT_BLOCK, side='right'), N_EXPERTS - 1)
    n_used = (pend[-1] // SLOT_BLOCK).reshape(1)
    return dest.astype(i32), blk_expert.astype(i32), n_used.astype(i32)


def _trunk(x, c_all, seq_lens, ada_w, ada_b, norm_g, w_in, pool_w, pool_scale, rpb, conv_w, w_branch,
           w_out, router_w, router_b, expert_w_gu, expert_b_gu, expert_w_down, expert_b_down):
    depth = ada_w.shape[0]
    n = x.shape[0]
    n_slots = n * TOP_K + N_EXPERTS * SLOT_BLOCK
    ada = _ada(c_all, ada_w, ada_b)
    for l in range(depth):
        ada_l = ada[l]
        ng = norm_g[l]
        zs = _inproj(x, ada_l, ng[0:1], w_in[l].astype(bf16), seq_lens)
        x, h2, logits_t = _mixer(
            x, zs, ada_l, ng, pool_w[l].astype(bf16), pool_scale[l].reshape(1, POOL_WIDTH),
            _bias_table(rpb[l]), conv_w[l], w_branch[l].astype(bf16), w_out[l].astype(bf16),
            router_w[l].T, router_b[l].reshape(N_EXPERTS, 1), seq_lens)
        top_i, rank, gates, cnt = _router(logits_t)
        dest, blk_expert, n_used = _slot_plan(cnt[:, 0].astype(i32), top_i, rank, n_slots)
        xs = _dispatch(h2, dest, n_slots)
        y = _experts(xs, blk_expert, n_used, expert_w_gu[l], expert_b_gu[l], expert_w_down[l],
                     expert_b_down[l])
        x = _combine(x, y, dest, gates.T, ada_l, ng, seq_lens)
    return x


def kernel(x_prompt, x_sample, c_prompt, c_sample, ada_w, ada_b, norm_g, w_in, pool_w, pool_scale, rpb,
           conv_w, w_branch, w_out, router_w, router_b, expert_w_gu, expert_b_gu, expert_w_down,
           expert_b_down):
    bp, lp, d = x_prompt.shape
    bs, ls, _ = x_sample.shape
    assert d == D_MODEL and bp + bs <= SEQ_PAD
    assert lp % IN_TILE == 0 and ls % IN_TILE == 0 and lp >= 2 * TILE and ls >= 2 * TILE
    x = jnp.concatenate([x_prompt.reshape(bp * lp, d), x_sample.reshape(bs * ls, d)], axis=0)
    c_all = jnp.concatenate([c_prompt, c_sample, jnp.zeros((SEQ_PAD - bp - bs, d), f32)], axis=0)
    seq_lens = ((bp, lp), (bs, ls))
    y = _trunk(x, c_all, seq_lens, ada_w, ada_b, norm_g, w_in, pool_w, pool_scale, rpb, conv_w, w_branch,
               w_out, router_w, router_b, expert_w_gu, expert_b_gu, expert_w_down, expert_b_down)
    return (y[:bp * lp].reshape(bp, lp, d), y[bp * lp:].reshape(bs, ls, d))
```

```python
import functools

import numpy as np
import jax
import jax.numpy as jnp
from jax import lax
from jax.experimental import pallas as pl
from jax.experimental.pallas import tpu as pltpu

f32 = jnp.float32
bf16 = jnp.bfloat16
i32 = jnp.int32

D_MODEL = 1024
GRID_W = 64
POOL_WINDOWS = (2, 4, 8, 16)
POOL_GROUP = 128
POOL_WIDTH = 512
N_HEADS = 8
HEAD_DIM = 64
N_PAIRS = N_HEADS // 2
ATTN_WIDTH = 512
WIN_ROWS = 8
WIN_COLS = 16
CONV_WIDTH = 512
N_BRANCH = 3
IN_COLS = POOL_WIDTH + 3 * ATTN_WIDTH + 3 * CONV_WIDTH + N_BRANCH * D_MODEL
N_EXPERTS = 32
TOP_K = 4
D_EXPERT = D_MODEL
SWIGLU_ALPHA = 1.702
SWIGLU_LIMIT = 7.0
RMS_EPS = 1e-6
N_MOD = 6
NEG_BIG = -1e30

SEQ_PAD = 16
ROWS_PER_TILE = 4
TILE = ROWS_PER_TILE * GRID_W
HALO = 8
IN_TILE = 512
ROUTE_TILE = 512
SLOT_BLOCK = 256
MOVE_TILE = 256
VMEM_LIMIT = 56 * 1024 * 1024


def _cparams(sem):
    return pltpu.CompilerParams(dimension_semantics=sem, vmem_limit_bytes=VMEM_LIMIT)


def _tile_meta(seq_lens, tile):
    seq, pos, cnt = [], [], []
    sid = 0
    for nseq, length in seq_lens:
        nt = length // tile
        for _ in range(nseq):
            for p in range(nt):
                seq.append(sid)
                pos.append(p)
                cnt.append(nt)
            sid += 1
    return (jnp.asarray(np.array(seq, np.int32)), jnp.asarray(np.array(pos, np.int32)),
            jnp.asarray(np.array(cnt, np.int32)))


def _rms(x, g):
    ms = jnp.mean(x * x, axis=-1, keepdims=True)
    return x * lax.rsqrt(ms + RMS_EPS) * g


def _ada_kernel(c_ref, w_ref, b_ref, o_ref):
    c = c_ref[...]
    s = c * jax.nn.sigmoid(c)
    o_ref[...] = jnp.dot(s, w_ref[...], preferred_element_type=f32,
                         precision=lax.Precision.HIGHEST) + b_ref[...]


def _ada(c_all, ada_w, ada_b):
    depth = ada_w.shape[0]
    return pl.pallas_call(
        _ada_kernel,
        grid=(depth, N_MOD),
        in_specs=[pl.BlockSpec((SEQ_PAD, D_MODEL), lambda l, j: (0, 0)),
                  pl.BlockSpec((None, D_MODEL, D_MODEL), lambda l, j: (l, 0, j)),
                  pl.BlockSpec((None, 1, D_MODEL), lambda l, j: (l, 0, j))],
        out_specs=pl.BlockSpec((None, SEQ_PAD, D_MODEL), lambda l, j: (l, 0, j)),
        out_shape=jax.ShapeDtypeStruct((depth, SEQ_PAD, N_MOD * D_MODEL), f32),
        compiler_params=_cparams(("arbitrary", "arbitrary")),
        name="ada",
    )(c_all, ada_w, ada_b.reshape(depth, 1, N_MOD * D_MODEL))


_IN_SPLITS = (POOL_WIDTH, ATTN_WIDTH, ATTN_WIDTH, ATTN_WIDTH, CONV_WIDTH, CONV_WIDTH, CONV_WIDTH,
              N_BRANCH * D_MODEL)


def _inproj_kernel(seq_ref, x_ref, ada_ref, g_ref, w_ref, *out_refs):
    s = seq_ref[pl.program_id(0)]
    mod = ada_ref[pl.ds(s, 1), :]
    h = _rms(x_ref[...], g_ref[...]) * (1.0 + mod[:, D_MODEL:2 * D_MODEL]) + mod[:, 0:D_MODEL]
    h = h.astype(bf16)
    col = 0
    for o_ref in out_refs:
        width = o_ref.shape[1]
        for c0 in range(0, width, 512):
            o_ref[:, c0:c0 + 512] = jnp.dot(h, w_ref[:, col + c0:col + c0 + 512],
                                            preferred_element_type=f32).astype(bf16)
        col += width


def _inproj(x, ada_l, g, w_in_bf, seq_lens):
    n = x.shape[0]
    seq, _, _ = _tile_meta(seq_lens, IN_TILE)
    row = lambda i, s: (i, 0)
    const = lambda i, s: (0, 0)
    grid_spec = pltpu.PrefetchScalarGridSpec(
        num_scalar_prefetch=1,
        grid=(n // IN_TILE,),
        in_specs=[pl.BlockSpec((IN_TILE, D_MODEL), row),
                  pl.BlockSpec((SEQ_PAD, N_MOD * D_MODEL), const),
                  pl.BlockSpec((1, D_MODEL), const),
                  pl.BlockSpec((D_MODEL, IN_COLS), const, pipeline_mode=pl.Buffered(1))],
        out_specs=[pl.BlockSpec((IN_TILE, w), row) for w in _IN_SPLITS],
    )
    return pl.pallas_call(
        _inproj_kernel,
        grid_spec=grid_spec,
        out_shape=[jax.ShapeDtypeStruct((n, w), bf16) for w in _IN_SPLITS],
        compiler_params=_cparams(("arbitrary",)),
        name="inproj",
    )(seq, x, ada_l, g, w_in_bf)


def _mixer_kernel(seq_ref, pos_ref, nt_ref,
                  x_ref, u_ref, up_ref, un_ref, q_ref, kp_ref, kc_ref, kn_ref, vp_ref, vc_ref, vn_ref,
                  b_ref, c_ref, cp_ref, cn_ref, xc_ref, xcp_ref, xcn_ref, gl_ref,
                  ada_ref, ng_ref, poolw_ref, pscale_ref, bias_ref, convw_ref, wb_ref, wout_ref,
                  rw_ref, rb_ref,
                  xo_ref, h2_ref, lg_ref,
                  uext, vext, kbuf, vbuf, pb_ref):
    i = pl.program_id(0)
    s = seq_ref[i]
    pos = pos_ref[i]
    nt = nt_ref[i]
    first = pos == 0
    last = pos == nt - 1
    seq_len = nt * TILE

    uext[HALO:HALO + TILE, :] = u_ref[...].astype(f32)
    uext[0:HALO, :] = jnp.where(first, 0.0, up_ref[...].astype(f32)[GRID_W - HALO:GRID_W, :])
    uext[HALO + TILE:2 * HALO + TILE, :] = jnp.where(last, 0.0, un_ref[...].astype(f32)[0:HALO, :])

    vext[HALO:HALO + TILE, :] = c_ref[...].astype(f32) * xc_ref[...].astype(f32)
    vprev = cp_ref[...].astype(f32) * xcp_ref[...].astype(f32)
    vnext = cn_ref[...].astype(f32) * xcn_ref[...].astype(f32)
    vext[0:HALO, :] = jnp.where(first, 0.0, vprev[GRID_W - HALO:GRID_W, :])
    vext[HALO + TILE:2 * HALO + TILE, :] = jnp.where(last, 0.0, vnext[0:HALO, :])

    kbuf[0:TILE, :] = kp_ref[...]
    kbuf[TILE:2 * TILE, :] = kc_ref[...]
    kbuf[2 * TILE:3 * TILE, :] = kn_ref[...]
    vbuf[0:TILE, :] = vp_ref[...]
    vbuf[TILE:2 * TILE, :] = vc_ref[...]
    vbuf[2 * TILE:3 * TILE, :] = vn_ref[...]

    lane = lax.broadcasted_iota(i32, (1, 2 * HEAD_DIM), 1)
    low = lane < HEAD_DIM
    rows = nt * ROWS_PER_TILE
    qscale = jnp.asarray(HEAD_DIM ** -0.5, bf16)

    def attn_row(j, carry):
        r = pos * ROWS_PER_TILE + j
        rs = jnp.clip(r - WIN_ROWS // 2, 0, rows - WIN_ROWS)
        si = r - rs
        koff = pl.multiple_of((rs - pos * ROWS_PER_TILE + ROWS_PER_TILE) * GRID_W, GRID_W)
        qoff = pl.multiple_of(j * GRID_W, GRID_W)
        for p in range(N_PAIRS):
            cols = slice(p * 2 * HEAD_DIM, (p + 1) * 2 * HEAD_DIM)
            qp = q_ref[pl.ds(qoff, GRID_W), cols] * qscale
            zero = jnp.zeros_like(qp)
            qs = jnp.concatenate([jnp.where(low, qp, zero), jnp.where(low, zero, qp)], axis=0)
            kw = kbuf[pl.ds(koff, WIN_ROWS * GRID_W), cols]
            vw = vbuf[pl.ds(koff, WIN_ROWS * GRID_W), cols]
            sc = lax.dot_general(qs, kw, (((1,), (1,)), ((), ())), preferred_element_type=f32)
            sc = sc + bias_ref[si, p]
            m = jnp.max(sc, axis=-1, keepdims=True)
            e = jnp.exp(sc - m)
            l = jnp.sum(e, axis=-1, keepdims=True)
            o = jnp.dot(e.astype(bf16), vw, preferred_element_type=f32) / l
            pb_ref[pl.ds(qoff, GRID_W), cols] = jnp.where(low, o[0:GRID_W], o[GRID_W:2 * GRID_W]).astype(bf16)
        return carry

    lax.fori_loop(0, ROWS_PER_TILE, attn_row, 0)

    t = pos * TILE + lax.broadcasted_iota(i32, (TILE, 1), 0)
    pa_parts = []
    for g, w in enumerate(POOL_WINDOWS):
        cols = slice(g * POOL_GROUP, (g + 1) * POOL_GROUP)
        half = w // 2
        acc = uext[HALO - half:HALO - half + TILE, cols]
        for d in range(-half + 1, half):
            acc = acc + uext[HALO + d:HALO + d + TILE, cols]
        cnt = (jnp.minimum(t + half, seq_len) - jnp.maximum(t - half, 0)).astype(f32)
        m = acc / cnt - uext[HALO:HALO + TILE, cols]
        z = jnp.dot(m.astype(bf16), poolw_ref[g], preferred_element_type=f32)
        pa_parts.append(z * pscale_ref[:, cols])
    pa = jnp.concatenate(pa_parts, axis=1).astype(bf16)

    y = (convw_ref[0:1, :] * vext[HALO - 1:HALO - 1 + TILE, :]
         + convw_ref[1:2, :] * vext[HALO:HALO + TILE, :]
         + convw_ref[2:3, :] * vext[HALO + 1:HALO + 1 + TILE, :])
    pc = (b_ref[...].astype(f32) * y).astype(bf16)

    branches = (pa, pb_ref[...], pc)
    merged = None
    for n in range(N_BRANCH):
        proj = jnp.dot(branches[n], wb_ref[n], preferred_element_type=f32)
        gate = jax.nn.sigmoid(gl_ref[:, n * D_MODEL:(n + 1) * D_MODEL].astype(f32))
        merged = gate * proj if merged is None else merged + gate * proj
    mix = jnp.dot(merged.astype(bf16), wout_ref[...], preferred_element_type=f32)

    mod = ada_ref[pl.ds(s, 1), :]
    gt1 = mod[:, 2 * D_MODEL:3 * D_MODEL]
    sh2 = mod[:, 3 * D_MODEL:4 * D_MODEL]
    sc2 = mod[:, 4 * D_MODEL:5 * D_MODEL]
    xn = x_ref[...] + gt1 * _rms(mix, ng_ref[1:2, :])
    xo_ref[...] = xn
    h2 = _rms(xn, ng_ref[2:3, :]) * (1.0 + sc2) + sh2
    h2_ref[...] = h2
    lg_ref[...] = lax.dot_general(rw_ref[...], h2, (((1,), (1,)), ((), ())), preferred_element_type=f32,
                                  precision=lax.Precision.HIGHEST) + rb_ref[...]


def _mixer(x, zs, ada_l, ng, poolw_bf, pscale, bias_tab, convw, wb_bf, wout_bf, rw_t, rb, seq_lens):
    n = x.shape[0]
    zu, zq, zk, zv, zb, zc, zx, zg = zs
    seq, pos, cnt = _tile_meta(seq_lens, TILE)
    per = TILE // GRID_W

    cur = lambda i, sq, ps, ct: (i, 0)
    prev_t = lambda i, sq, ps, ct: (jnp.where(ps[i] > 0, i - 1, i), 0)
    next_t = lambda i, sq, ps, ct: (jnp.where(ps[i] < ct[i] - 1, i + 1, i), 0)
    prev_h = lambda i, sq, ps, ct: (jnp.where(ps[i] > 0, i * per - 1, i * per), 0)
    next_h = lambda i, sq, ps, ct: (jnp.where(ps[i] < ct[i] - 1, (i + 1) * per, i * per), 0)
    c2 = lambda i, sq, ps, ct: (0, 0)
    c3 = lambda i, sq, ps, ct: (0, 0, 0)
    c4 = lambda i, sq, ps, ct: (0, 0, 0, 0)

    tile512 = lambda im: pl.BlockSpec((TILE, 512), im)
    halo512 = lambda im: pl.BlockSpec((GRID_W, 512), im)
    once = pl.Buffered(1)
    in_specs = [
        pl.BlockSpec((TILE, D_MODEL), cur),
        tile512(cur), halo512(prev_h), halo512(next_h),
        tile512(cur),
        tile512(prev_t), tile512(cur), tile512(next_t),
        tile512(prev_t), tile512(cur), tile512(next_t),
        tile512(cur),
        tile512(cur), halo512(prev_h), halo512(next_h),
        tile512(cur), halo512(prev_h), halo512(next_h),
        pl.BlockSpec((TILE, N_BRANCH * D_MODEL), cur),
        pl.BlockSpec((SEQ_PAD, N_MOD * D_MODEL), c2),
        pl.BlockSpec((4, D_MODEL), c2),
        pl.BlockSpec((4, POOL_GROUP, POOL_GROUP), c3),
        pl.BlockSpec((1, POOL_WIDTH), c2),
        pl.BlockSpec((WIN_ROWS, N_PAIRS, 2 * GRID_W, WIN_ROWS * GRID_W), c4, pipeline_mode=once),
        pl.BlockSpec((3, CONV_WIDTH), c2),
        pl.BlockSpec((N_BRANCH, POOL_WIDTH, D_MODEL), c3, pipeline_mode=once),
        pl.BlockSpec((D_MODEL, D_MODEL), c2, pipeline_mode=once),
        pl.BlockSpec((N_EXPERTS, D_MODEL), c2),
        pl.BlockSpec((N_EXPERTS, 1), c2),
    ]
    out_specs = [
        pl.BlockSpec((TILE, D_MODEL), cur),
        pl.BlockSpec((TILE, D_MODEL), cur),
        pl.BlockSpec((N_EXPERTS, TILE), lambda i, sq, ps, ct: (0, i)),
    ]
    grid_spec = pltpu.PrefetchScalarGridSpec(
        num_scalar_prefetch=3,
        grid=(n // TILE,),
        in_specs=in_specs,
        out_specs=out_specs,
        scratch_shapes=[pltpu.VMEM((TILE + 2 * HALO, POOL_WIDTH), f32),
                        pltpu.VMEM((TILE + 2 * HALO, CONV_WIDTH), f32),
                        pltpu.VMEM((3 * TILE, ATTN_WIDTH), bf16),
                        pltpu.VMEM((3 * TILE, ATTN_WIDTH), bf16),
                        pltpu.VMEM((TILE, ATTN_WIDTH), bf16)],
    )
    return pl.pallas_call(
        _mixer_kernel,
        grid_spec=grid_spec,
        out_shape=[jax.ShapeDtypeStruct((n, D_MODEL), f32),
                   jax.ShapeDtypeStruct((n, D_MODEL), f32),
                   jax.ShapeDtypeStruct((N_EXPERTS, n), f32)],
        compiler_params=_cparams(("arbitrary",)),
        name="mixer",
    )(seq, pos, cnt,
      x, zu, zu, zu, zq, zk, zk, zk, zv, zv, zv, zb, zc, zc, zc, zx, zx, zx, zg,
      ada_l, ng, poolw_bf, pscale, bias_tab, convw, wb_bf, wout_bf, rw_t, rb)


def _bias_table(rpb_l):
    c = np.arange(GRID_W)[:, None]
    kc = np.arange(GRID_W)[None, :]
    cs = np.clip(c - WIN_COLS // 2, 0, GRID_W - WIN_COLS)
    valid = (kc >= cs) & (kc < cs + WIN_COLS)
    didx = np.clip(kc - c + WIN_COLS - 1, 0, 2 * WIN_COLS - 2)
    select = (didx[:, :, None] == np.arange(2 * WIN_COLS - 1)).astype(np.float32)
    dense = jnp.einsum('hrd,ckd->hrck', rpb_l, jnp.asarray(select), precision=lax.Precision.HIGHEST)
    dense = jnp.where(valid[None, None], dense, NEG_BIG)
    slabs = [dense[:, WIN_ROWS - 1 - s:2 * WIN_ROWS - 1 - s] for s in range(WIN_ROWS)]
    tab = jnp.stack(slabs, axis=0)
    tab = tab.transpose(0, 1, 3, 2, 4)
    return tab.reshape(WIN_ROWS, N_PAIRS, 2 * GRID_W, WIN_ROWS * GRID_W).astype(f32)


def _router_kernel(lg_ref, ti_ref, rk_ref, gt_ref, cnt_ref, base):
    @pl.when(pl.program_id(0) == 0)
    def _():
        base[...] = jnp.zeros_like(base)

    l = lg_ref[...]
    eio = lax.broadcasted_iota(i32, l.shape, 0)
    vals, hots = [], []
    for k in range(TOP_K):
        m = jnp.max(l, axis=0, keepdims=True)
        idx = jnp.min(jnp.where(l == m, eio, N_EXPERTS), axis=0, keepdims=True)
        hot = eio == idx
        l = jnp.where(hot, -jnp.inf, l)
        vals.append(m)
        hots.append(hot)
        ti_ref[k:k + 1, :] = idx
    ex = [jnp.exp(v - vals[0]) for v in vals]
    den = ex[0] + ex[1] + ex[2] + ex[3]
    for k in range(TOP_K):
        gt_ref[k:k + 1, :] = ex[k] / den

    hot_all = jnp.zeros(l.shape, f32)
    for hot in hots:
        hot_all = hot_all + hot.astype(f32)
    rr = lax.broadcasted_iota(i32, (ROUTE_TILE, ROUTE_TILE), 0)
    cc = lax.broadcasted_iota(i32, (ROUTE_TILE, ROUTE_TILE), 1)
    tri = (rr <= cc).astype(bf16)
    pref = jnp.dot(hot_all.astype(bf16), tri, preferred_element_type=f32)
    b = base[:, 0:1]
    tot = b + pref - 1.0
    for k in range(TOP_K):
        rk_ref[k:k + 1, :] = jnp.sum(jnp.where(hots[k], tot, 0.0), axis=0, keepdims=True).astype(i32)
    newb = jnp.broadcast_to(b + pref[:, ROUTE_TILE - 1:ROUTE_TILE], base.shape)
    base[...] = newb
    cnt_ref[...] = newb


def _router(logits_t):
    n = logits_t.shape[1]
    blk = lambda i: (0, i)
    return pl.pallas_call(
        _router_kernel,
        grid=(n // ROUTE_TILE,),
        in_specs=[pl.BlockSpec((N_EXPERTS, ROUTE_TILE), blk)],
        out_specs=[pl.BlockSpec((TOP_K, ROUTE_TILE), blk),
                   pl.BlockSpec((TOP_K, ROUTE_TILE), blk),
                   pl.BlockSpec((TOP_K, ROUTE_TILE), blk),
                   pl.BlockSpec((N_EXPERTS, 128), lambda i: (0, 0))],
        out_shape=[jax.ShapeDtypeStruct((TOP_K, n), i32),
                   jax.ShapeDtypeStruct((TOP_K, n), i32),
                   jax.ShapeDtypeStruct((TOP_K, n), f32),
                   jax.ShapeDtypeStruct((N_EXPERTS, 128), f32)],
        scratch_shapes=[pltpu.VMEM((N_EXPERTS, 128), f32)],
        compiler_params=_cparams(("arbitrary",)),
        name="router",
    )(logits_t)


def _expert_kernel(be_ref, nu_ref, src_ref, srcn_ref, dst_ref, h_ref, wgu_ref, bgu_ref, wd_ref, bd_ref,
                   y_ref, xbuf, ybuf, wgu_bf, wd_bf, act, sem_in, sem_out):
    b = pl.program_id(0)
    nb = pl.num_programs(0)
    nu = nu_ref[0]
    slot = b % 2
    e = be_ref[b]
    prev = be_ref[jnp.maximum(b - 1, 0)]

    def gather_copy(idx_ref, r, s):
        return pltpu.make_async_copy(h_ref.at[pl.ds(idx_ref[0, 0, r], 1), :],
                                     xbuf.at[s, pl.ds(r, 1), :], sem_in.at[s])

    def scatter_copy(r, s):
        return pltpu.make_async_copy(ybuf.at[s, pl.ds(r, 1), :],
                                     y_ref.at[pl.ds(dst_ref[0, 0, r], 1), :], sem_out.at[s])

    def start_gather(idx_ref, s):
        def body(r, carry):
            gather_copy(idx_ref, r, s).start()
            return carry
        lax.fori_loop(0, SLOT_BLOCK, body, 0, unroll=8)

    def wait_gather(s):
        def body(r, carry):
            gather_copy(src_ref, r, s).wait()
            return carry
        lax.fori_loop(0, SLOT_BLOCK, body, 0, unroll=8)

    def wait_scatter(s):
        def body(r, carry):
            scatter_copy(r, s).wait()
            return carry
        lax.fori_loop(0, SLOT_BLOCK, body, 0, unroll=8)

    @pl.when(b == 0)
    def _():
        start_gather(src_ref, 0)

    @pl.when(b + 1 < nu)
    def _():
        start_gather(srcn_ref, 1 - slot)

    @pl.when((b >= 2) & (b - 2 < nu))
    def _():
        wait_scatter(slot)

    @pl.when((b == 0) | (e != prev))
    def _():
        wgu_bf[...] = wgu_ref[...].astype(bf16)
        wd_bf[...] = wd_ref[...].astype(bf16)

    @pl.when(b < nu)
    def _():
        wait_gather(slot)
        x = xbuf[slot].astype(bf16)
        for c0 in range(0, D_EXPERT, 512):
            gate = jnp.dot(x, wgu_bf[:, c0:c0 + 512], preferred_element_type=f32) + bgu_ref[:, c0:c0 + 512]
            up = (jnp.dot(x, wgu_bf[:, D_EXPERT + c0:D_EXPERT + c0 + 512], preferred_element_type=f32)
                  + bgu_ref[:, D_EXPERT + c0:D_EXPERT + c0 + 512])
            gate = jnp.minimum(gate, SWIGLU_LIMIT)
            up = jnp.clip(up, -SWIGLU_LIMIT, SWIGLU_LIMIT)
            act[:, c0:c0 + 512] = ((up + 1.0) * (gate * jax.nn.sigmoid(gate * SWIGLU_ALPHA))).astype(bf16)
        ybuf[slot] = jnp.dot(act[...], wd_bf[...], preferred_element_type=f32) + bd_ref[...]

        def body(r, carry):
            scatter_copy(r, slot).start()
            return carry
        lax.fori_loop(0, SLOT_BLOCK, body, 0, unroll=8)

    @pl.when(b == nb - 1)
    def _():
        @pl.when((b >= 1) & (b - 1 < nu))
        def _():
            wait_scatter(1 - slot)

        @pl.when(b < nu)
        def _():
            wait_scatter(slot)


def _experts(h2, slot_src, slot_dst, blk_expert, n_used, w_gu, b_gu, w_down, b_down):
    n = h2.shape[0]
    nblk = slot_src.shape[0]
    wsel = lambda b, be, nu: (be[b], 0, 0)
    cur = lambda b, be, nu: (b, 0, 0)
    nxt = lambda b, be, nu: (jnp.minimum(b + 1, nblk - 1), 0, 0)
    idx_spec = lambda im: pl.BlockSpec((1, 1, SLOT_BLOCK), im, memory_space=pltpu.SMEM)
    grid_spec = pltpu.PrefetchScalarGridSpec(
        num_scalar_prefetch=2,
        grid=(nblk,),
        in_specs=[idx_spec(cur), idx_spec(nxt), idx_spec(cur),
                  pl.BlockSpec(memory_space=pl.ANY),
                  pl.BlockSpec((None, D_MODEL, 2 * D_EXPERT), wsel),
                  pl.BlockSpec((None, 1, 2 * D_EXPERT), wsel),
                  pl.BlockSpec((None, D_EXPERT, D_MODEL), wsel),
                  pl.BlockSpec((None, 1, D_MODEL), wsel)],
        out_specs=pl.BlockSpec(memory_space=pl.ANY),
        scratch_shapes=[pltpu.VMEM((2, SLOT_BLOCK, D_MODEL), f32),
                        pltpu.VMEM((2, SLOT_BLOCK, D_MODEL), f32),
                        pltpu.VMEM((D_MODEL, 2 * D_EXPERT), bf16),
                        pltpu.VMEM((D_EXPERT, D_MODEL), bf16),
                        pltpu.VMEM((SLOT_BLOCK, D_EXPERT), bf16),
                        pltpu.SemaphoreType.DMA((2,)),
                        pltpu.SemaphoreType.DMA((2,))],
    )
    return pl.pallas_call(
        _expert_kernel,
        grid_spec=grid_spec,
        out_shape=jax.ShapeDtypeStruct((TOP_K * n + 2 * SLOT_BLOCK, D_MODEL), f32),
        compiler_params=_cparams(("arbitrary",)),
        name="experts",
    )(blk_expert, n_used, slot_src, slot_src, slot_dst, h2, w_gu,
      b_gu.reshape(N_EXPERTS, 1, 2 * D_EXPERT), w_down, b_down.reshape(N_EXPERTS, 1, D_MODEL))


def _combine_kernel(seq_ref, x_ref, y0_ref, y1_ref, y2_ref, y3_ref, gt_ref, ada_ref, ng_ref, xo_ref):
    ff = gt_ref[:, 0:1] * y0_ref[...]
    for k, y_ref in ((1, y1_ref), (2, y2_ref), (3, y3_ref)):
        ff = ff + gt_ref[:, k:k + 1] * y_ref[...]
    s = seq_ref[pl.program_id(0)]
    gt2 = ada_ref[pl.ds(s, 1), 5 * D_MODEL:6 * D_MODEL]
    xo_ref[...] = x_ref[...] + gt2 * _rms(ff, ng_ref[3:4, :])


def _combine(x, y, gates_t, ada_l, ng, seq_lens):
    n = x.shape[0]
    per = n // MOVE_TILE
    seq, _, _ = _tile_meta(seq_lens, MOVE_TILE)
    row = lambda i, sq: (i, 0)
    const = lambda i, sq: (0, 0)
    choice = lambda k: pl.BlockSpec((MOVE_TILE, D_MODEL), lambda i, sq: (k * per + i, 0))
    grid_spec = pltpu.PrefetchScalarGridSpec(
        num_scalar_prefetch=1,
        grid=(per,),
        in_specs=[pl.BlockSpec((MOVE_TILE, D_MODEL), row),
                  choice(0), choice(1), choice(2), choice(3),
                  pl.BlockSpec((MOVE_TILE, TOP_K), row),
                  pl.BlockSpec((SEQ_PAD, N_MOD * D_MODEL), const),
                  pl.BlockSpec((4, D_MODEL), const)],
        out_specs=pl.BlockSpec((MOVE_TILE, D_MODEL), row),
    )
    return pl.pallas_call(
        _combine_kernel,
        grid_spec=grid_spec,
        out_shape=jax.ShapeDtypeStruct((n, D_MODEL), f32),
        compiler_params=_cparams(("arbitrary",)),
        name="combine",
    )(seq, x, y, y, y, y, gates_t, ada_l, ng)


def _slot_plan(counts, top_i, rank, n):
    n_slots = n * TOP_K + N_EXPERTS * SLOT_BLOCK
    nblk = n_slots // SLOT_BLOCK
    padded = (counts + SLOT_BLOCK - 1) // SLOT_BLOCK * SLOT_BLOCK
    pend = jnp.cumsum(padded)
    pstart = pend - padded
    experts = jnp.arange(N_EXPERTS, dtype=i32)
    start_of = jnp.sum(jnp.where(top_i[..., None] == experts, pstart, 0), axis=-1)
    dest = (start_of + rank).reshape(-1)
    slots = jnp.arange(n_slots, dtype=i32)
    spare = TOP_K * n + slots % (2 * SLOT_BLOCK)
    slot_dst = spare.at[dest].set(jnp.arange(TOP_K * n, dtype=i32), unique_indices=True)
    slot_src = jnp.where(slot_dst < TOP_K * n, slot_dst % n, 0)
    blk_first = jnp.arange(nblk, dtype=i32) * SLOT_BLOCK
    blk_expert = jnp.minimum(jnp.sum((blk_first[:, None] >= pend[None, :]).astype(i32), axis=1),
                             N_EXPERTS - 1)
    n_used = (pend[-1] // SLOT_BLOCK).reshape(1)
    shape = (nblk, 1, SLOT_BLOCK)
    return (slot_src.astype(i32).reshape(shape), slot_dst.astype(i32).reshape(shape),
            blk_expert.astype(i32), n_used.astype(i32))


def _trunk(x, c_all, seq_lens, ada_w, ada_b, norm_g, w_in, pool_w, pool_scale, rpb, conv_w, w_branch,
           w_out, router_w, router_b, expert_w_gu, expert_b_gu, expert_w_down, expert_b_down):
    depth = ada_w.shape[0]
    n = x.shape[0]
    ada = _ada(c_all, ada_w, ada_b)
    for l in range(depth):
        ada_l = ada[l]
        ng = norm_g[l]
        zs = _inproj(x, ada_l, ng[0:1], w_in[l].astype(bf16), seq_lens)
        x, h2, logits_t = _mixer(
            x, zs, ada_l, ng, pool_w[l].astype(bf16), pool_scale[l].reshape(1, POOL_WIDTH),
            _bias_table(rpb[l]), conv_w[l], w_branch[l].astype(bf16), w_out[l].astype(bf16),
            router_w[l].T, router_b[l].reshape(N_EXPERTS, 1), seq_lens)
        top_i, rank, gates, cnt = _router(logits_t)
        slot_src, slot_dst, blk_expert, n_used = _slot_plan(cnt[:, 0].astype(i32), top_i, rank, n)
        y = _experts(h2, slot_src, slot_dst, blk_expert, n_used, expert_w_gu[l], expert_b_gu[l],
                     expert_w_down[l], expert_b_down[l])
        x = _combine(x, y, gates.T, ada_l, ng, seq_lens)
    return x


def kernel(x_prompt, x_sample, c_prompt, c_sample, ada_w, ada_b, norm_g, w_in, pool_w, pool_scale, rpb,
           conv_w, w_branch, w_out, router_w, router_b, expert_w_gu, expert_b_gu, expert_w_down,
           expert_b_down):
    bp, lp, d = x_prompt.shape
    bs, ls, _ = x_sample.shape
    assert d == D_MODEL and bp + bs <= SEQ_PAD
    assert lp % IN_TILE == 0 and ls % IN_TILE == 0 and lp >= 2 * TILE and ls >= 2 * TILE
    x = jnp.concatenate([x_prompt.reshape(bp * lp, d), x_sample.reshape(bs * ls, d)], axis=0)
    c_all = jnp.concatenate([c_prompt, c_sample, jnp.zeros((SEQ_PAD - bp - bs, d), f32)], axis=0)
    seq_lens = ((bp, lp), (bs, ls))
    y = _trunk(x, c_all, seq_lens, ada_w, ada_b, norm_g, w_in, pool_w, pool_scale, rpb, conv_w, w_branch,
               w_out, router_w, router_b, expert_w_gu, expert_b_gu, expert_w_down, expert_b_down)
    return (y[:bp * lp].reshape(bp, lp, d), y[bp * lp:].reshape(bs, ls, d))
```

```python
import functools

import numpy as np
import jax
import jax.numpy as jnp
from jax import lax
from jax.experimental import pallas as pl
from jax.experimental.pallas import tpu as pltpu

f32 = jnp.float32
bf16 = jnp.bfloat16
i32 = jnp.int32

D_MODEL = 1024
GRID_W = 64
POOL_WINDOWS = (2, 4, 8, 16)
POOL_GROUP = 128
POOL_WIDTH = 512
N_HEADS = 8
HEAD_DIM = 64
N_PAIRS = N_HEADS // 2
ATTN_WIDTH = 512
WIN_ROWS = 8
WIN_COLS = 16
CONV_WIDTH = 512
N_BRANCH = 3
IN_COLS = POOL_WIDTH + 3 * ATTN_WIDTH + 3 * CONV_WIDTH + N_BRANCH * D_MODEL
N_EXPERTS = 32
TOP_K = 4
D_EXPERT = D_MODEL
SWIGLU_ALPHA = 1.702
SWIGLU_LIMIT = 7.0
RMS_EPS = 1e-6
N_MOD = 6
NEG_BIG = -1e30

SEQ_PAD = 16
ROWS_PER_TILE = 4
TILE = ROWS_PER_TILE * GRID_W
HALO = 8
IN_TILE = 512
SORT_TILE = 512
CHUNK = 8
SLOT_BLOCK = 256
BLOCK_CHUNKS = SLOT_BLOCK // CHUNK
LOCAL_ROWS = TOP_K * SORT_TILE + N_EXPERTS * CHUNK
VMEM_LIMIT = 56 * 1024 * 1024


def _cparams(sem):
    return pltpu.CompilerParams(dimension_semantics=sem, vmem_limit_bytes=VMEM_LIMIT)


def _tile_meta(seq_lens, tile):
    seq, pos, cnt = [], [], []
    sid = 0
    for nseq, length in seq_lens:
        nt = length // tile
        for _ in range(nseq):
            for p in range(nt):
                seq.append(sid)
                pos.append(p)
                cnt.append(nt)
            sid += 1
    return (jnp.asarray(np.array(seq, np.int32)), jnp.asarray(np.array(pos, np.int32)),
            jnp.asarray(np.array(cnt, np.int32)))


def _rms(x, g):
    ms = jnp.mean(x * x, axis=-1, keepdims=True)
    return x * lax.rsqrt(ms + RMS_EPS) * g


def _ada_kernel(c_ref, w_ref, b_ref, o_ref):
    c = c_ref[...]
    s = c * jax.nn.sigmoid(c)
    o_ref[...] = jnp.dot(s, w_ref[...], preferred_element_type=f32,
                         precision=lax.Precision.HIGHEST) + b_ref[...]


def _ada(c_all, ada_w, ada_b):
    depth = ada_w.shape[0]
    return pl.pallas_call(
        _ada_kernel,
        grid=(depth, N_MOD),
        in_specs=[pl.BlockSpec((SEQ_PAD, D_MODEL), lambda l, j: (0, 0)),
                  pl.BlockSpec((None, D_MODEL, D_MODEL), lambda l, j: (l, 0, j)),
                  pl.BlockSpec((None, 1, D_MODEL), lambda l, j: (l, 0, j))],
        out_specs=pl.BlockSpec((None, SEQ_PAD, D_MODEL), lambda l, j: (l, 0, j)),
        out_shape=jax.ShapeDtypeStruct((depth, SEQ_PAD, N_MOD * D_MODEL), f32),
        compiler_params=_cparams(("arbitrary", "arbitrary")),
        name="ada",
    )(c_all, ada_w, ada_b.reshape(depth, 1, N_MOD * D_MODEL))


_IN_SPLITS = (POOL_WIDTH, ATTN_WIDTH, ATTN_WIDTH, ATTN_WIDTH, CONV_WIDTH, CONV_WIDTH, CONV_WIDTH,
              N_BRANCH * D_MODEL)


def _inproj_kernel(seq_ref, x_ref, ada_ref, g_ref, w_ref, *out_refs):
    s = seq_ref[pl.program_id(0)]
    mod = ada_ref[pl.ds(s, 1), :]
    h = _rms(x_ref[...], g_ref[...]) * (1.0 + mod[:, D_MODEL:2 * D_MODEL]) + mod[:, 0:D_MODEL]
    h = h.astype(bf16)
    col = 0
    for o_ref in out_refs:
        width = o_ref.shape[1]
        for c0 in range(0, width, 512):
            o_ref[:, c0:c0 + 512] = jnp.dot(h, w_ref[:, col + c0:col + c0 + 512],
                                            preferred_element_type=f32).astype(bf16)
        col += width


def _inproj(x, ada_l, g, w_in_bf, seq_lens):
    n = x.shape[0]
    seq, _, _ = _tile_meta(seq_lens, IN_TILE)
    row = lambda i, s: (i, 0)
    const = lambda i, s: (0, 0)
    grid_spec = pltpu.PrefetchScalarGridSpec(
        num_scalar_prefetch=1,
        grid=(n // IN_TILE,),
        in_specs=[pl.BlockSpec((IN_TILE, D_MODEL), row),
                  pl.BlockSpec((SEQ_PAD, N_MOD * D_MODEL), const),
                  pl.BlockSpec((1, D_MODEL), const),
                  pl.BlockSpec((D_MODEL, IN_COLS), const, pipeline_mode=pl.Buffered(1))],
        out_specs=[pl.BlockSpec((IN_TILE, w), row) for w in _IN_SPLITS],
    )
    return pl.pallas_call(
        _inproj_kernel,
        grid_spec=grid_spec,
        out_shape=[jax.ShapeDtypeStruct((n, w), bf16) for w in _IN_SPLITS],
        compiler_params=_cparams(("arbitrary",)),
        name="inproj",
    )(seq, x, ada_l, g, w_in_bf)


def _mixer_kernel(seq_ref, pos_ref, nt_ref,
                  x_ref, u_ref, up_ref, un_ref, q_ref, kp_ref, kc_ref, kn_ref, vp_ref, vc_ref, vn_ref,
                  b_ref, c_ref, cp_ref, cn_ref, xc_ref, xcp_ref, xcn_ref, gl_ref,
                  ada_ref, ng_ref, poolw_ref, pscale_ref, bias_ref, convw_ref, wb_ref, wout_ref,
                  rw_ref, rb_ref,
                  xo_ref, h2_ref, lg_ref,
                  uext, vext, kbuf, vbuf, pb_ref):
    i = pl.program_id(0)
    s = seq_ref[i]
    pos = pos_ref[i]
    nt = nt_ref[i]
    first = pos == 0
    last = pos == nt - 1
    seq_len = nt * TILE

    uext[HALO:HALO + TILE, :] = u_ref[...].astype(f32)
    uext[0:HALO, :] = jnp.where(first, 0.0, up_ref[...].astype(f32)[GRID_W - HALO:GRID_W, :])
    uext[HALO + TILE:2 * HALO + TILE, :] = jnp.where(last, 0.0, un_ref[...].astype(f32)[0:HALO, :])

    vext[HALO:HALO + TILE, :] = c_ref[...].astype(f32) * xc_ref[...].astype(f32)
    vprev = cp_ref[...].astype(f32) * xcp_ref[...].astype(f32)
    vnext = cn_ref[...].astype(f32) * xcn_ref[...].astype(f32)
    vext[0:HALO, :] = jnp.where(first, 0.0, vprev[GRID_W - HALO:GRID_W, :])
    vext[HALO + TILE:2 * HALO + TILE, :] = jnp.where(last, 0.0, vnext[0:HALO, :])

    kbuf[0:TILE, :] = kp_ref[...]
    kbuf[TILE:2 * TILE, :] = kc_ref[...]
    kbuf[2 * TILE:3 * TILE, :] = kn_ref[...]
    vbuf[0:TILE, :] = vp_ref[...]
    vbuf[TILE:2 * TILE, :] = vc_ref[...]
    vbuf[2 * TILE:3 * TILE, :] = vn_ref[...]

    lane = lax.broadcasted_iota(i32, (1, 2 * HEAD_DIM), 1)
    low = lane < HEAD_DIM
    rows = nt * ROWS_PER_TILE
    qscale = jnp.asarray(HEAD_DIM ** -0.5, bf16)

    def attn_row(j, carry):
        r = pos * ROWS_PER_TILE + j
        rs = jnp.clip(r - WIN_ROWS // 2, 0, rows - WIN_ROWS)
        si = r - rs
        koff = pl.multiple_of((rs - pos * ROWS_PER_TILE + ROWS_PER_TILE) * GRID_W, GRID_W)
        qoff = pl.multiple_of(j * GRID_W, GRID_W)
        for p in range(N_PAIRS):
            cols = slice(p * 2 * HEAD_DIM, (p + 1) * 2 * HEAD_DIM)
            qp = q_ref[pl.ds(qoff, GRID_W), cols] * qscale
            zero = jnp.zeros_like(qp)
            qs = jnp.concatenate([jnp.where(low, qp, zero), jnp.where(low, zero, qp)], axis=0)
            kw = kbuf[pl.ds(koff, WIN_ROWS * GRID_W), cols]
            vw = vbuf[pl.ds(koff, WIN_ROWS * GRID_W), cols]
            sc = lax.dot_general(qs, kw, (((1,), (1,)), ((), ())), preferred_element_type=f32)
            sc = sc + bias_ref[si, p]
            m = jnp.max(sc, axis=-1, keepdims=True)
            e = jnp.exp(sc - m)
            l = jnp.sum(e, axis=-1, keepdims=True)
            o = jnp.dot(e.astype(bf16), vw, preferred_element_type=f32) / l
            pb_ref[pl.ds(qoff, GRID_W), cols] = jnp.where(low, o[0:GRID_W], o[GRID_W:2 * GRID_W]).astype(bf16)
        return carry

    lax.fori_loop(0, ROWS_PER_TILE, attn_row, 0)

    t = pos * TILE + lax.broadcasted_iota(i32, (TILE, 1), 0)
    pa_parts = []
    for g, w in enumerate(POOL_WINDOWS):
        cols = slice(g * POOL_GROUP, (g + 1) * POOL_GROUP)
        half = w // 2
        acc = uext[HALO - half:HALO - half + TILE, cols]
        for d in range(-half + 1, half):
            acc = acc + uext[HALO + d:HALO + d + TILE, cols]
        cnt = (jnp.minimum(t + half, seq_len) - jnp.maximum(t - half, 0)).astype(f32)
        m = acc / cnt - uext[HALO:HALO + TILE, cols]
        z = jnp.dot(m.astype(bf16), poolw_ref[g], preferred_element_type=f32)
        pa_parts.append(z * pscale_ref[:, cols])
    pa = jnp.concatenate(pa_parts, axis=1).astype(bf16)

    y = (convw_ref[0:1, :] * vext[HALO - 1:HALO - 1 + TILE, :]
         + convw_ref[1:2, :] * vext[HALO:HALO + TILE, :]
         + convw_ref[2:3, :] * vext[HALO + 1:HALO + 1 + TILE, :])
    pc = (b_ref[...].astype(f32) * y).astype(bf16)

    branches = (pa, pb_ref[...], pc)
    merged = None
    for n in range(N_BRANCH):
        proj = jnp.dot(branches[n], wb_ref[n], preferred_element_type=f32)
        gate = jax.nn.sigmoid(gl_ref[:, n * D_MODEL:(n + 1) * D_MODEL].astype(f32))
        merged = gate * proj if merged is None else merged + gate * proj
    mix = jnp.dot(merged.astype(bf16), wout_ref[...], preferred_element_type=f32)

    mod = ada_ref[pl.ds(s, 1), :]
    gt1 = mod[:, 2 * D_MODEL:3 * D_MODEL]
    sh2 = mod[:, 3 * D_MODEL:4 * D_MODEL]
    sc2 = mod[:, 4 * D_MODEL:5 * D_MODEL]
    xn = x_ref[...] + gt1 * _rms(mix, ng_ref[1:2, :])
    xo_ref[...] = xn
    h2 = _rms(xn, ng_ref[2:3, :]) * (1.0 + sc2) + sh2
    h2_ref[...] = h2
    lg_ref[...] = lax.dot_general(rw_ref[...], h2, (((1,), (1,)), ((), ())), preferred_element_type=f32,
                                  precision=lax.Precision.HIGHEST) + rb_ref[...]


def _mixer(x, zs, ada_l, ng, poolw_bf, pscale, bias_tab, convw, wb_bf, wout_bf, rw_t, rb, seq_lens):
    n = x.shape[0]
    zu, zq, zk, zv, zb, zc, zx, zg = zs
    seq, pos, cnt = _tile_meta(seq_lens, TILE)
    per = TILE // GRID_W

    cur = lambda i, sq, ps, ct: (i, 0)
    prev_t = lambda i, sq, ps, ct: (jnp.where(ps[i] > 0, i - 1, i), 0)
    next_t = lambda i, sq, ps, ct: (jnp.where(ps[i] < ct[i] - 1, i + 1, i), 0)
    prev_h = lambda i, sq, ps, ct: (jnp.where(ps[i] > 0, i * per - 1, i * per), 0)
    next_h = lambda i, sq, ps, ct: (jnp.where(ps[i] < ct[i] - 1, (i + 1) * per, i * per), 0)
    c2 = lambda i, sq, ps, ct: (0, 0)
    c3 = lambda i, sq, ps, ct: (0, 0, 0)
    c4 = lambda i, sq, ps, ct: (0, 0, 0, 0)

    tile512 = lambda im: pl.BlockSpec((TILE, 512), im)
    halo512 = lambda im: pl.BlockSpec((GRID_W, 512), im)
    once = pl.Buffered(1)
    in_specs = [
        pl.BlockSpec((TILE, D_MODEL), cur),
        tile512(cur), halo512(prev_h), halo512(next_h),
        tile512(cur),
        tile512(prev_t), tile512(cur), tile512(next_t),
        tile512(prev_t), tile512(cur), tile512(next_t),
        tile512(cur),
        tile512(cur), halo512(prev_h), halo512(next_h),
        tile512(cur), halo512(prev_h), halo512(next_h),
        pl.BlockSpec((TILE, N_BRANCH * D_MODEL), cur),
        pl.BlockSpec((SEQ_PAD, N_MOD * D_MODEL), c2),
        pl.BlockSpec((4, D_MODEL), c2),
        pl.BlockSpec((4, POOL_GROUP, POOL_GROUP), c3),
        pl.BlockSpec((1, POOL_WIDTH), c2),
        pl.BlockSpec((WIN_ROWS, N_PAIRS, 2 * GRID_W, WIN_ROWS * GRID_W), c4, pipeline_mode=once),
        pl.BlockSpec((3, CONV_WIDTH), c2),
        pl.BlockSpec((N_BRANCH, POOL_WIDTH, D_MODEL), c3, pipeline_mode=once),
        pl.BlockSpec((D_MODEL, D_MODEL), c2, pipeline_mode=once),
        pl.BlockSpec((N_EXPERTS, D_MODEL), c2),
        pl.BlockSpec((N_EXPERTS, 1), c2),
    ]
    out_specs = [
        pl.BlockSpec((TILE, D_MODEL), cur),
        pl.BlockSpec((TILE, D_MODEL), cur),
        pl.BlockSpec((N_EXPERTS, TILE), lambda i, sq, ps, ct: (0, i)),
    ]
    grid_spec = pltpu.PrefetchScalarGridSpec(
        num_scalar_prefetch=3,
        grid=(n // TILE,),
        in_specs=in_specs,
        out_specs=out_specs,
        scratch_shapes=[pltpu.VMEM((TILE + 2 * HALO, POOL_WIDTH), f32),
                        pltpu.VMEM((TILE + 2 * HALO, CONV_WIDTH), f32),
                        pltpu.VMEM((3 * TILE, ATTN_WIDTH), bf16),
                        pltpu.VMEM((3 * TILE, ATTN_WIDTH), bf16),
                        pltpu.VMEM((TILE, ATTN_WIDTH), bf16)],
    )
    return pl.pallas_call(
        _mixer_kernel,
        grid_spec=grid_spec,
        out_shape=[jax.ShapeDtypeStruct((n, D_MODEL), f32),
                   jax.ShapeDtypeStruct((n, D_MODEL), f32),
                   jax.ShapeDtypeStruct((N_EXPERTS, n), f32)],
        compiler_params=_cparams(("arbitrary",)),
        name="mixer",
    )(seq, pos, cnt,
      x, zu, zu, zu, zq, zk, zk, zk, zv, zv, zv, zb, zc, zc, zc, zx, zx, zx, zg,
      ada_l, ng, poolw_bf, pscale, bias_tab, convw, wb_bf, wout_bf, rw_t, rb)


def _bias_table(rpb_l):
    c = np.arange(GRID_W)[:, None]
    kc = np.arange(GRID_W)[None, :]
    cs = np.clip(c - WIN_COLS // 2, 0, GRID_W - WIN_COLS)
    valid = (kc >= cs) & (kc < cs + WIN_COLS)
    didx = np.clip(kc - c + WIN_COLS - 1, 0, 2 * WIN_COLS - 2)
    select = (didx[:, :, None] == np.arange(2 * WIN_COLS - 1)).astype(np.float32)
    dense = jnp.einsum('hrd,ckd->hrck', rpb_l, jnp.asarray(select), precision=lax.Precision.HIGHEST)
    dense = jnp.where(valid[None, None], dense, NEG_BIG)
    slabs = [dense[:, WIN_ROWS - 1 - s:2 * WIN_ROWS - 1 - s] for s in range(WIN_ROWS)]
    tab = jnp.stack(slabs, axis=0)
    tab = tab.transpose(0, 1, 3, 2, 4)
    return tab.reshape(WIN_ROWS, N_PAIRS, 2 * GRID_W, WIN_ROWS * GRID_W).astype(f32)


def _router_kernel(lg_ref, lp_ref, gt_ref, tc_ref):
    l = lg_ref[...]
    eio = lax.broadcasted_iota(i32, l.shape, 0)
    vals, hots = [], []
    for k in range(TOP_K):
        m = jnp.max(l, axis=0, keepdims=True)
        idx = jnp.min(jnp.where(l == m, eio, N_EXPERTS), axis=0, keepdims=True)
        hot = eio == idx
        l = jnp.where(hot, -jnp.inf, l)
        vals.append(m)
        hots.append(hot)
    ex = [jnp.exp(v - vals[0]) for v in vals]
    den = ex[0] + ex[1] + ex[2] + ex[3]
    for k in range(TOP_K):
        gt_ref[k:k + 1, :] = ex[k] / den

    hot_all = jnp.zeros(l.shape, f32)
    for hot in hots:
        hot_all = hot_all + hot.astype(f32)
    rr = lax.broadcasted_iota(i32, (SORT_TILE, SORT_TILE), 0)
    cc = lax.broadcasted_iota(i32, (SORT_TILE, SORT_TILE), 1)
    tri = (rr <= cc).astype(bf16)
    pref = jnp.dot(hot_all.astype(bf16), tri, preferred_element_type=f32)
    cnt = jnp.broadcast_to(pref[:, SORT_TILE - 1:SORT_TILE], (N_EXPERTS, 128))
    padded = jnp.floor((cnt + (CHUNK - 1)) * (1.0 / CHUNK)) * CHUNK
    er = lax.broadcasted_iota(i32, (N_EXPERTS, N_EXPERTS), 0)
    ec = lax.broadcasted_iota(i32, (N_EXPERTS, N_EXPERTS), 1)
    before = (ec < er).astype(f32)
    start = jnp.dot(before, padded, preferred_element_type=f32, precision=lax.Precision.HIGHEST)
    tot = start[:, 0:1] + pref - 1.0
    for k in range(TOP_K):
        lp_ref[k:k + 1, :] = jnp.sum(jnp.where(hots[k], tot, 0.0), axis=0, keepdims=True).astype(i32)
    tc_ref[...] = cnt.astype(i32)


def _router(logits_t):
    n = logits_t.shape[1]
    blk = lambda i: (0, i)
    return pl.pallas_call(
        _router_kernel,
        grid=(n // SORT_TILE,),
        in_specs=[pl.BlockSpec((N_EXPERTS, SORT_TILE), blk)],
        out_specs=[pl.BlockSpec((TOP_K, SORT_TILE), blk),
                   pl.BlockSpec((TOP_K, SORT_TILE), blk),
                   pl.BlockSpec((None, N_EXPERTS, 128), lambda i: (i, 0, 0))],
        out_shape=[jax.ShapeDtypeStruct((TOP_K, n), i32),
                   jax.ShapeDtypeStruct((TOP_K, n), f32),
                   jax.ShapeDtypeStruct((n // SORT_TILE, N_EXPERTS, 128), i32)],
        compiler_params=_cparams(("arbitrary",)),
        name="router",
    )(logits_t)


def _to_token_tiles(dst_ref, val):
    rows = val.shape[0]
    for s in range(8):
        dst_ref[pl.ds(s, rows, stride=8), :] = val[:, s * 128:(s + 1) * 128]


def _from_token_tiles(src_ref, rows):
    return jnp.concatenate([src_ref[pl.ds(s, rows, stride=8), :] for s in range(8)], axis=1)


def _sort_kernel(lp_ref, h_ref, o_ref, tmp):
    _to_token_tiles(tmp, h_ref[...])
    o_ref[...] = jnp.zeros_like(o_ref)

    def body(t, carry):
        v = tmp[pl.ds(pl.multiple_of(t * 8, 8), 8), :]
        for k in range(TOP_K):
            o_ref[pl.ds(pl.multiple_of(lp_ref[k, t] * 8, 8), 8), :] = v
        return carry

    lax.fori_loop(0, SORT_TILE, body, 0, unroll=8)


def _local_sort(h2, lpos):
    n = h2.shape[0]
    nt = n // SORT_TILE
    return pl.pallas_call(
        _sort_kernel,
        grid=(nt,),
        in_specs=[pl.BlockSpec((TOP_K, SORT_TILE), lambda i: (0, i), memory_space=pltpu.SMEM),
                  pl.BlockSpec((SORT_TILE, D_MODEL), lambda i: (i, 0))],
        out_specs=pl.BlockSpec((LOCAL_ROWS * 8, 128), lambda i: (i, 0)),
        out_shape=jax.ShapeDtypeStruct((nt * LOCAL_ROWS * 8, 128), f32),
        scratch_shapes=[pltpu.VMEM((SORT_TILE * 8, 128), f32)],
        compiler_params=_cparams(("arbitrary",)),
        name="local_sort",
    )(lpos, h2)


def _expert_kernel(be_ref, nu_ref, src_ref, srcn_ref, dst_ref, xs_ref, wgu_ref, bgu_ref, wd_ref, bd_ref,
                   ys_ref, xbuf, ybuf, wgu_bf, wd_bf, act, sem_in, sem_out):
    b = pl.program_id(0)
    nb = pl.num_programs(0)
    nu = nu_ref[0]
    slot = b % 2
    e = be_ref[b]
    prev = be_ref[jnp.maximum(b - 1, 0)]
    chunk_rows = CHUNK * 8

    def gather_copy(idx_ref, c, s):
        src = pl.multiple_of(idx_ref[0, 0, c] * 8, chunk_rows)
        return pltpu.make_async_copy(xs_ref.at[pl.ds(src, chunk_rows), :],
                                     xbuf.at[s, pl.ds(c * chunk_rows, chunk_rows), :], sem_in.at[s])

    def scatter_copy(c, s):
        dst = pl.multiple_of(dst_ref[0, 0, c] * 8, chunk_rows)
        return pltpu.make_async_copy(ybuf.at[s, pl.ds(c * chunk_rows, chunk_rows), :],
                                     ys_ref.at[pl.ds(dst, chunk_rows), :], sem_out.at[s])

    def start_gather(idx_ref, s):
        for c in range(BLOCK_CHUNKS):
            gather_copy(idx_ref, c, s).start()

    def wait_gather(s):
        for c in range(BLOCK_CHUNKS):
            gather_copy(src_ref, c, s).wait()

    def wait_scatter(s):
        for c in range(BLOCK_CHUNKS):
            scatter_copy(c, s).wait()

    @pl.when(b == 0)
    def _():
        start_gather(src_ref, 0)

    @pl.when(b + 1 < nu)
    def _():
        start_gather(srcn_ref, 1 - slot)

    @pl.when((b >= 2) & (b - 2 < nu))
    def _():
        wait_scatter(slot)

    @pl.when((b == 0) | (e != prev))
    def _():
        wgu_bf[...] = wgu_ref[...].astype(bf16)
        wd_bf[...] = wd_ref[...].astype(bf16)

    @pl.when(b < nu)
    def _():
        wait_gather(slot)
        x = _from_token_tiles(xbuf.at[slot], SLOT_BLOCK).astype(bf16)
        for c0 in range(0, D_EXPERT, 512):
            gate = jnp.dot(x, wgu_bf[:, c0:c0 + 512], preferred_element_type=f32) + bgu_ref[:, c0:c0 + 512]
            up = (jnp.dot(x, wgu_bf[:, D_EXPERT + c0:D_EXPERT + c0 + 512], preferred_element_type=f32)
                  + bgu_ref[:, D_EXPERT + c0:D_EXPERT + c0 + 512])
            gate = jnp.minimum(gate, SWIGLU_LIMIT)
            up = jnp.clip(up, -SWIGLU_LIMIT, SWIGLU_LIMIT)
            act[:, c0:c0 + 512] = ((up + 1.0) * (gate * jax.nn.sigmoid(gate * SWIGLU_ALPHA))).astype(bf16)
        y = jnp.dot(act[...], wd_bf[...], preferred_element_type=f32) + bd_ref[...]
        _to_token_tiles(ybuf.at[slot], y)
        for c in range(BLOCK_CHUNKS):
            scatter_copy(c, slot).start()

    @pl.when(b == nb - 1)
    def _():
        @pl.when((b >= 1) & (b - 1 < nu))
        def _():
            wait_scatter(1 - slot)

        @pl.when(b < nu)
        def _():
            wait_scatter(slot)


def _experts(xs, chunk_src, chunk_dst, blk_expert, n_used, w_gu, b_gu, w_down, b_down):
    nblk = chunk_src.shape[0]
    wsel = lambda b, be, nu: (be[b], 0, 0)
    cur = lambda b, be, nu: (b, 0, 0)
    nxt = lambda b, be, nu: (jnp.minimum(b + 1, nblk - 1), 0, 0)
    idx_spec = lambda im: pl.BlockSpec((1, 1, BLOCK_CHUNKS), im, memory_space=pltpu.SMEM)
    grid_spec = pltpu.PrefetchScalarGridSpec(
        num_scalar_prefetch=2,
        grid=(nblk,),
        in_specs=[idx_spec(cur), idx_spec(nxt), idx_spec(cur),
                  pl.BlockSpec(memory_space=pl.ANY),
                  pl.BlockSpec((None, D_MODEL, 2 * D_EXPERT), wsel),
                  pl.BlockSpec((None, 1, 2 * D_EXPERT), wsel),
                  pl.BlockSpec((None, D_EXPERT, D_MODEL), wsel),
                  pl.BlockSpec((None, 1, D_MODEL), wsel)],
        out_specs=pl.BlockSpec(memory_space=pl.ANY),
        scratch_shapes=[pltpu.VMEM((2, SLOT_BLOCK * 8, 128), f32),
                        pltpu.VMEM((2, SLOT_BLOCK * 8, 128), f32),
                        pltpu.VMEM((D_MODEL, 2 * D_EXPERT), bf16),
                        pltpu.VMEM((D_EXPERT, D_MODEL), bf16),
                        pltpu.VMEM((SLOT_BLOCK, D_EXPERT), bf16),
                        pltpu.SemaphoreType.DMA((2,)),
                        pltpu.SemaphoreType.DMA((2,))],
    )
    return pl.pallas_call(
        _expert_kernel,
        grid_spec=grid_spec,
        out_shape=jax.ShapeDtypeStruct((xs.shape[0] + 2 * SLOT_BLOCK * 8, 128), f32),
        compiler_params=_cparams(("arbitrary",)),
        name="experts",
    )(blk_expert, n_used, chunk_src, chunk_src, chunk_dst, xs, w_gu,
      b_gu.reshape(N_EXPERTS, 1, 2 * D_EXPERT), w_down, b_down.reshape(N_EXPERTS, 1, D_MODEL))


def _combine_kernel(seq_ref, lp_ref, gt_ref, x_ref, ys_ref, ada_ref, ng_ref, xo_ref, ff):
    def body(t, carry):
        acc = gt_ref[0, t] * ys_ref[pl.ds(pl.multiple_of(lp_ref[0, t] * 8, 8), 8), :]
        for k in range(1, TOP_K):
            acc = acc + gt_ref[k, t] * ys_ref[pl.ds(pl.multiple_of(lp_ref[k, t] * 8, 8), 8), :]
        ff[pl.ds(pl.multiple_of(t * 8, 8), 8), :] = acc
        return carry

    lax.fori_loop(0, SORT_TILE, body, 0, unroll=8)
    s = seq_ref[pl.program_id(0)]
    gt2 = ada_ref[pl.ds(s, 1), 5 * D_MODEL:6 * D_MODEL]
    xo_ref[...] = x_ref[...] + gt2 * _rms(_from_token_tiles(ff, SORT_TILE), ng_ref[3:4, :])


def _combine(x, ys, lpos, gates, ada_l, ng, seq_lens):
    n = x.shape[0]
    seq, _, _ = _tile_meta(seq_lens, SORT_TILE)
    row = lambda i, sq: (i, 0)
    col = lambda i, sq: (0, i)
    const = lambda i, sq: (0, 0)
    grid_spec = pltpu.PrefetchScalarGridSpec(
        num_scalar_prefetch=1,
        grid=(n // SORT_TILE,),
        in_specs=[pl.BlockSpec((TOP_K, SORT_TILE), col, memory_space=pltpu.SMEM),
                  pl.BlockSpec((TOP_K, SORT_TILE), col, memory_space=pltpu.SMEM),
                  pl.BlockSpec((SORT_TILE, D_MODEL), row),
                  pl.BlockSpec((LOCAL_ROWS * 8, 128), row),
                  pl.BlockSpec((SEQ_PAD, N_MOD * D_MODEL), const),
                  pl.BlockSpec((4, D_MODEL), const)],
        out_specs=pl.BlockSpec((SORT_TILE, D_MODEL), row),
        scratch_shapes=[pltpu.VMEM((SORT_TILE * 8, 128), f32)],
    )
    return pl.pallas_call(
        _combine_kernel,
        grid_spec=grid_spec,
        out_shape=jax.ShapeDtypeStruct((n, D_MODEL), f32),
        compiler_params=_cparams(("arbitrary",)),
        name="combine",
    )(seq, lpos, gates, x, ys, ada_l, ng)


def _chunk_plan(tile_counts):
    nt = tile_counts.shape[0]
    nch = (tile_counts + CHUNK - 1) // CHUNK
    rows = nch * CHUNK
    loff = jnp.cumsum(rows, axis=1) - rows
    cend = jnp.cumsum(nch, axis=0)
    cstart = cend - nch
    total = cend[-1]
    nblk_e = (total + BLOCK_CHUNKS - 1) // BLOCK_CHUNKS
    bend = jnp.cumsum(nblk_e)
    bstart = bend - nblk_e
    n_used = bend[-1].reshape(1)
    max_chunks = nt * (TOP_K * SORT_TILE // CHUNK + N_EXPERTS)
    nblk = max_chunks // BLOCK_CHUNKS + N_EXPERTS
    b = jnp.arange(nblk, dtype=i32)
    blk_expert = jnp.minimum(jnp.sum((b[:, None] >= bend[None, :]).astype(i32), axis=1), N_EXPERTS - 1)
    onehot = (blk_expert[:, None] == jnp.arange(N_EXPERTS, dtype=i32)).astype(f32)

    def of_block(per_expert):
        table = per_expert.astype(f32).reshape(-1, N_EXPERTS).T
        return jnp.dot(onehot, table, precision=lax.Precision.HIGHEST).astype(i32)

    cend_b, cstart_b, loff_b = of_block(cend), of_block(cstart), of_block(loff)
    total_b, bstart_b = of_block(total)[:, 0], of_block(bstart)[:, 0]
    q = (b - bstart_b)[:, None] * BLOCK_CHUNKS + jnp.arange(BLOCK_CHUNKS, dtype=i32)[None, :]
    valid = (q < total_b[:, None]) & (b < n_used[0])[:, None]
    tile = jnp.minimum(jnp.sum((cend_b[:, None, :] <= q[:, :, None]).astype(i32), axis=2), nt - 1)
    pick = tile[:, :, None] == jnp.arange(nt, dtype=i32)
    cs = jnp.sum(jnp.where(pick, cstart_b[:, None, :], 0), axis=2)
    lo = jnp.sum(jnp.where(pick, loff_b[:, None, :], 0), axis=2)
    row = tile * LOCAL_ROWS + lo + (q - cs) * CHUNK
    spare = (nt * LOCAL_ROWS + (b % 2)[:, None] * SLOT_BLOCK
             + jnp.arange(BLOCK_CHUNKS, dtype=i32)[None, :] * CHUNK)
    shape = (nblk, 1, BLOCK_CHUNKS)
    chunk_src = jnp.where(valid, row, 0).astype(i32).reshape(shape)
    chunk_dst = jnp.where(valid, row, spare).astype(i32).reshape(shape)
    return chunk_src, chunk_dst, blk_expert.astype(i32), n_used.astype(i32)


def _trunk(x, c_all, seq_lens, ada_w, ada_b, norm_g, w_in, pool_w, pool_scale, rpb, conv_w, w_branch,
           w_out, router_w, router_b, expert_w_gu, expert_b_gu, expert_w_down, expert_b_down):
    depth = ada_w.shape[0]
    ada = _ada(c_all, ada_w, ada_b)
    for l in range(depth):
        ada_l = ada[l]
        ng = norm_g[l]
        zs = _inproj(x, ada_l, ng[0:1], w_in[l].astype(bf16), seq_lens)
        x, h2, logits_t = _mixer(
            x, zs, ada_l, ng, pool_w[l].astype(bf16), pool_scale[l].reshape(1, POOL_WIDTH),
            _bias_table(rpb[l]), conv_w[l], w_branch[l].astype(bf16), w_out[l].astype(bf16),
            router_w[l].T, router_b[l].reshape(N_EXPERTS, 1), seq_lens)
        lpos, gates, tile_counts = _router(logits_t)
        chunk_src, chunk_dst, blk_expert, n_used = _chunk_plan(tile_counts[:, :, 0])
        xs = _local_sort(h2, lpos)
        ys = _experts(xs, chunk_src, chunk_dst, blk_expert, n_used, expert_w_gu[l], expert_b_gu[l],
                      expert_w_down[l], expert_b_down[l])
        x = _combine(x, ys, lpos, gates, ada_l, ng, seq_lens)
    return x


def kernel(x_prompt, x_sample, c_prompt, c_sample, ada_w, ada_b, norm_g, w_in, pool_w, pool_scale, rpb,
           conv_w, w_branch, w_out, router_w, router_b, expert_w_gu, expert_b_gu, expert_w_down,
           expert_b_down):
    bp, lp, d = x_prompt.shape
    bs, ls, _ = x_sample.shape
    assert d == D_MODEL and bp + bs <= SEQ_PAD
    assert lp % IN_TILE == 0 and ls % IN_TILE == 0 and lp >= 2 * TILE and ls >= 2 * TILE
    x = jnp.concatenate([x_prompt.reshape(bp * lp, d), x_sample.reshape(bs * ls, d)], axis=0)
    c_all = jnp.concatenate([c_prompt, c_sample, jnp.zeros((SEQ_PAD - bp - bs, d), f32)], axis=0)
    seq_lens = ((bp, lp), (bs, ls))
    y = _trunk(x, c_all, seq_lens, ada_w, ada_b, norm_g, w_in, pool_w, pool_scale, rpb, conv_w, w_branch,
               w_out, router_w, router_b, expert_w_gu, expert_b_gu, expert_w_down, expert_b_down)
    return (y[:bp * lp].reshape(bp, lp, d), y[bp * lp:].reshape(bs, ls, d))
```

```python
import functools

import numpy as np
import jax
import jax.numpy as jnp
from jax import lax
from jax.experimental import pallas as pl
from jax.experimental.pallas import tpu as pltpu

f32 = jnp.float32
bf16 = jnp.bfloat16
i32 = jnp.int32

D_MODEL = 1024
GRID_W = 64
POOL_WINDOWS = (2, 4, 8, 16)
POOL_GROUP = 128
POOL_WIDTH = 512
N_HEADS = 8
HEAD_DIM = 64
N_PAIRS = N_HEADS // 2
ATTN_WIDTH = 512
WIN_ROWS = 8
WIN_COLS = 16
CONV_WIDTH = 512
N_BRANCH = 3
IN_COLS = POOL_WIDTH + 3 * ATTN_WIDTH + 3 * CONV_WIDTH + N_BRANCH * D_MODEL
N_EXPERTS = 32
TOP_K = 4
D_EXPERT = D_MODEL
SWIGLU_ALPHA = 1.702
SWIGLU_LIMIT = 7.0
RMS_EPS = 1e-6
N_MOD = 6
NEG_BIG = -1e30

SEQ_PAD = 16
ROWS_PER_TILE = 4
TILE = ROWS_PER_TILE * GRID_W
HALO = 8
IN_TILE = 512
SORT_TILE = 512
CHUNK = 8
SLOT_BLOCK = 512
BLOCK_CHUNKS = SLOT_BLOCK // CHUNK
LOCAL_ROWS = TOP_K * SORT_TILE + N_EXPERTS * CHUNK
VMEM_LIMIT = 56 * 1024 * 1024


def _cparams(sem):
    return pltpu.CompilerParams(dimension_semantics=sem, vmem_limit_bytes=VMEM_LIMIT)


def _tile_meta(seq_lens, tile):
    seq, pos, cnt = [], [], []
    sid = 0
    for nseq, length in seq_lens:
        nt = length // tile
        for _ in range(nseq):
            for p in range(nt):
                seq.append(sid)
                pos.append(p)
                cnt.append(nt)
            sid += 1
    return (jnp.asarray(np.array(seq, np.int32)), jnp.asarray(np.array(pos, np.int32)),
            jnp.asarray(np.array(cnt, np.int32)))


def _rms(x, g):
    ms = jnp.mean(x * x, axis=-1, keepdims=True)
    return x * lax.rsqrt(ms + RMS_EPS) * g


def _ada_kernel(c_ref, w_ref, b_ref, o_ref):
    c = c_ref[...]
    s = c * jax.nn.sigmoid(c)
    o_ref[...] = jnp.dot(s, w_ref[...], preferred_element_type=f32,
                         precision=lax.Precision.HIGHEST) + b_ref[...]


def _ada(c_all, ada_w, ada_b):
    depth = ada_w.shape[0]
    return pl.pallas_call(
        _ada_kernel,
        grid=(depth, N_MOD),
        in_specs=[pl.BlockSpec((SEQ_PAD, D_MODEL), lambda l, j: (0, 0)),
                  pl.BlockSpec((None, D_MODEL, D_MODEL), lambda l, j: (l, 0, j)),
                  pl.BlockSpec((None, 1, D_MODEL), lambda l, j: (l, 0, j))],
        out_specs=pl.BlockSpec((None, SEQ_PAD, D_MODEL), lambda l, j: (l, 0, j)),
        out_shape=jax.ShapeDtypeStruct((depth, SEQ_PAD, N_MOD * D_MODEL), f32),
        compiler_params=_cparams(("arbitrary", "arbitrary")),
        name="ada",
    )(c_all, ada_w, ada_b.reshape(depth, 1, N_MOD * D_MODEL))


_IN_SPLITS = (POOL_WIDTH, ATTN_WIDTH, ATTN_WIDTH, ATTN_WIDTH, CONV_WIDTH, CONV_WIDTH, CONV_WIDTH,
              N_BRANCH * D_MODEL)


def _inproj_kernel(seq_ref, x_ref, ada_ref, g_ref, w_ref, *out_refs):
    s = seq_ref[pl.program_id(0)]
    mod = ada_ref[pl.ds(s, 1), :]
    h = _rms(x_ref[...], g_ref[...]) * (1.0 + mod[:, D_MODEL:2 * D_MODEL]) + mod[:, 0:D_MODEL]
    h = h.astype(bf16)
    col = 0
    for o_ref in out_refs:
        width = o_ref.shape[1]
        for c0 in range(0, width, 512):
            o_ref[:, c0:c0 + 512] = jnp.dot(h, w_ref[:, col + c0:col + c0 + 512],
                                            preferred_element_type=f32).astype(bf16)
        col += width


def _inproj(x, ada_l, g, w_in_bf, seq_lens):
    n = x.shape[0]
    seq, _, _ = _tile_meta(seq_lens, IN_TILE)
    row = lambda i, s: (i, 0)
    const = lambda i, s: (0, 0)
    grid_spec = pltpu.PrefetchScalarGridSpec(
        num_scalar_prefetch=1,
        grid=(n // IN_TILE,),
        in_specs=[pl.BlockSpec((IN_TILE, D_MODEL), row),
                  pl.BlockSpec((SEQ_PAD, N_MOD * D_MODEL), const),
                  pl.BlockSpec((1, D_MODEL), const),
                  pl.BlockSpec((D_MODEL, IN_COLS), const, pipeline_mode=pl.Buffered(1))],
        out_specs=[pl.BlockSpec((IN_TILE, w), row) for w in _IN_SPLITS],
    )
    return pl.pallas_call(
        _inproj_kernel,
        grid_spec=grid_spec,
        out_shape=[jax.ShapeDtypeStruct((n, w), bf16) for w in _IN_SPLITS],
        compiler_params=_cparams(("arbitrary",)),
        name="inproj",
    )(seq, x, ada_l, g, w_in_bf)


def _mixer_kernel(seq_ref, pos_ref, nt_ref,
                  x_ref, u_ref, up_ref, un_ref, q_ref, kp_ref, kc_ref, kn_ref, vp_ref, vc_ref, vn_ref,
                  b_ref, c_ref, cp_ref, cn_ref, xc_ref, xcp_ref, xcn_ref, gl_ref,
                  ada_ref, ng_ref, poolw_ref, pscale_ref, bias_ref, convw_ref, wb_ref, wout_ref,
                  rwh_ref, rwl_ref, rb_ref,
                  xo_ref, h2_ref, lg_ref,
                  uext, vext, kbuf, vbuf, pb_ref, sc_ref, e_ref):
    i = pl.program_id(0)
    s = seq_ref[i]
    pos = pos_ref[i]
    nt = nt_ref[i]
    first = pos == 0
    last = pos == nt - 1
    seq_len = nt * TILE

    uext[HALO:HALO + TILE, :] = u_ref[...].astype(f32)
    uext[0:HALO, :] = jnp.where(first, 0.0, up_ref[...].astype(f32)[GRID_W - HALO:GRID_W, :])
    uext[HALO + TILE:2 * HALO + TILE, :] = jnp.where(last, 0.0, un_ref[...].astype(f32)[0:HALO, :])

    vext[HALO:HALO + TILE, :] = c_ref[...].astype(f32) * xc_ref[...].astype(f32)
    vprev = cp_ref[...].astype(f32) * xcp_ref[...].astype(f32)
    vnext = cn_ref[...].astype(f32) * xcn_ref[...].astype(f32)
    vext[0:HALO, :] = jnp.where(first, 0.0, vprev[GRID_W - HALO:GRID_W, :])
    vext[HALO + TILE:2 * HALO + TILE, :] = jnp.where(last, 0.0, vnext[0:HALO, :])

    kbuf[0:TILE, :] = kp_ref[...]
    kbuf[TILE:2 * TILE, :] = kc_ref[...]
    kbuf[2 * TILE:3 * TILE, :] = kn_ref[...]
    vbuf[0:TILE, :] = vp_ref[...]
    vbuf[TILE:2 * TILE, :] = vc_ref[...]
    vbuf[2 * TILE:3 * TILE, :] = vn_ref[...]

    width = 2 * HEAD_DIM
    lane = lax.broadcasted_iota(i32, (1, width), 1)
    low = lane < HEAD_DIM
    rows = nt * ROWS_PER_TILE
    qscale = jnp.asarray(HEAD_DIM ** -0.5, bf16)
    eye = (lax.broadcasted_iota(i32, (width, width), 0)
           == lax.broadcasted_iota(i32, (width, width), 1)).astype(bf16)
    ones = jnp.ones((WIN_ROWS * GRID_W, width), bf16)

    def window(j):
        r = pos * ROWS_PER_TILE + j
        rs = jnp.clip(r - WIN_ROWS // 2, 0, rows - WIN_ROWS)
        koff = pl.multiple_of((rs - pos * ROWS_PER_TILE + ROWS_PER_TILE) * GRID_W, GRID_W)
        return r - rs, koff

    items = [(j, p) for j in range(ROWS_PER_TILE) for p in range(N_PAIRS)]
    for n, (j, p) in enumerate(items):
        si, koff = window(j)
        cols = slice(p * width, (p + 1) * width)
        qp = q_ref[j * GRID_W:(j + 1) * GRID_W, cols] * qscale
        zero = jnp.zeros_like(qp)
        qs = jnp.concatenate([jnp.where(low, qp, zero), jnp.where(low, zero, qp)], axis=0)
        lhs = jnp.concatenate([qs, eye], axis=1)
        rhs = jnp.concatenate([kbuf[pl.ds(koff, WIN_ROWS * GRID_W), cols], bias_ref[si, p]], axis=1)
        sc_ref[n] = lax.dot_general(lhs, rhs, (((1,), (1,)), ((), ())), preferred_element_type=f32)
    for n in range(len(items)):
        sc = sc_ref[n]
        e_ref[n] = jnp.exp(sc - jnp.max(sc, axis=-1, keepdims=True)).astype(bf16)
    for n, (j, p) in enumerate(items):
        _, koff = window(j)
        cols = slice(p * width, (p + 1) * width)
        rhs = jnp.concatenate([vbuf[pl.ds(koff, WIN_ROWS * GRID_W), cols], ones], axis=1)
        oe = jnp.dot(e_ref[n], rhs, preferred_element_type=f32)
        o = oe[:, 0:width] / oe[:, width:2 * width]
        pb_ref[j * GRID_W:(j + 1) * GRID_W, cols] = jnp.where(low, o[0:GRID_W], o[GRID_W:2 * GRID_W]).astype(bf16)

    t = pos * TILE + lax.broadcasted_iota(i32, (TILE, 1), 0)
    pa_parts = []
    for g, w in enumerate(POOL_WINDOWS):
        cols = slice(g * POOL_GROUP, (g + 1) * POOL_GROUP)
        half = w // 2
        acc = uext[HALO - half:HALO - half + TILE, cols]
        for d in range(-half + 1, half):
            acc = acc + uext[HALO + d:HALO + d + TILE, cols]
        cnt = (jnp.minimum(t + half, seq_len) - jnp.maximum(t - half, 0)).astype(f32)
        m = acc / cnt - uext[HALO:HALO + TILE, cols]
        z = jnp.dot(m.astype(bf16), poolw_ref[g], preferred_element_type=f32)
        pa_parts.append(z * pscale_ref[:, cols])
    pa = jnp.concatenate(pa_parts, axis=1).astype(bf16)

    y = (convw_ref[0:1, :] * vext[HALO - 1:HALO - 1 + TILE, :]
         + convw_ref[1:2, :] * vext[HALO:HALO + TILE, :]
         + convw_ref[2:3, :] * vext[HALO + 1:HALO + 1 + TILE, :])
    pc = (b_ref[...].astype(f32) * y).astype(bf16)

    branches = (pa, pb_ref[...], pc)
    merged = None
    for n in range(N_BRANCH):
        proj = jnp.dot(branches[n], wb_ref[n], preferred_element_type=f32)
        gate = jax.nn.sigmoid(gl_ref[:, n * D_MODEL:(n + 1) * D_MODEL].astype(f32))
        merged = gate * proj if merged is None else merged + gate * proj
    mix = jnp.dot(merged.astype(bf16), wout_ref[...], preferred_element_type=f32)

    mod = ada_ref[pl.ds(s, 1), :]
    gt1 = mod[:, 2 * D_MODEL:3 * D_MODEL]
    sh2 = mod[:, 3 * D_MODEL:4 * D_MODEL]
    sc2 = mod[:, 4 * D_MODEL:5 * D_MODEL]
    xn = x_ref[...] + gt1 * _rms(mix, ng_ref[1:2, :])
    xo_ref[...] = xn
    h2 = _rms(xn, ng_ref[2:3, :]) * (1.0 + sc2) + sh2
    h2_ref[...] = h2
    h_hi = h2.astype(bf16)
    h_lo = (h2 - h_hi.astype(f32)).astype(bf16)
    logits = (jnp.dot(h_hi, rwh_ref[...], preferred_element_type=f32)
              + jnp.dot(h_lo, rwh_ref[...], preferred_element_type=f32)
              + jnp.dot(h_hi, rwl_ref[...], preferred_element_type=f32)
              + rb_ref[...])
    lg_ref[...] = logits.T[0:N_EXPERTS, :]


def _mixer(x, zs, ada_l, ng, poolw_bf, pscale, bias_tab, convw, wb_bf, wout_bf, rw_hi, rw_lo, rb,
           seq_lens):
    n = x.shape[0]
    zu, zq, zk, zv, zb, zc, zx, zg = zs
    seq, pos, cnt = _tile_meta(seq_lens, TILE)
    per = TILE // GRID_W

    cur = lambda i, sq, ps, ct: (i, 0)
    prev_t = lambda i, sq, ps, ct: (jnp.where(ps[i] > 0, i - 1, i), 0)
    next_t = lambda i, sq, ps, ct: (jnp.where(ps[i] < ct[i] - 1, i + 1, i), 0)
    prev_h = lambda i, sq, ps, ct: (jnp.where(ps[i] > 0, i * per - 1, i * per), 0)
    next_h = lambda i, sq, ps, ct: (jnp.where(ps[i] < ct[i] - 1, (i + 1) * per, i * per), 0)
    c2 = lambda i, sq, ps, ct: (0, 0)
    c3 = lambda i, sq, ps, ct: (0, 0, 0)
    c4 = lambda i, sq, ps, ct: (0, 0, 0, 0)

    tile512 = lambda im: pl.BlockSpec((TILE, 512), im)
    halo512 = lambda im: pl.BlockSpec((GRID_W, 512), im)
    once = pl.Buffered(1)
    in_specs = [
        pl.BlockSpec((TILE, D_MODEL), cur),
        tile512(cur), halo512(prev_h), halo512(next_h),
        tile512(cur),
        tile512(prev_t), tile512(cur), tile512(next_t),
        tile512(prev_t), tile512(cur), tile512(next_t),
        tile512(cur),
        tile512(cur), halo512(prev_h), halo512(next_h),
        tile512(cur), halo512(prev_h), halo512(next_h),
        pl.BlockSpec((TILE, N_BRANCH * D_MODEL), cur),
        pl.BlockSpec((SEQ_PAD, N_MOD * D_MODEL), c2),
        pl.BlockSpec((4, D_MODEL), c2),
        pl.BlockSpec((4, POOL_GROUP, POOL_GROUP), c3),
        pl.BlockSpec((1, POOL_WIDTH), c2),
        pl.BlockSpec((WIN_ROWS, N_PAIRS, WIN_ROWS * GRID_W, 2 * GRID_W), c4, pipeline_mode=once),
        pl.BlockSpec((3, CONV_WIDTH), c2),
        pl.BlockSpec((N_BRANCH, POOL_WIDTH, D_MODEL), c3, pipeline_mode=once),
        pl.BlockSpec((D_MODEL, D_MODEL), c2, pipeline_mode=once),
        pl.BlockSpec((D_MODEL, 128), c2),
        pl.BlockSpec((D_MODEL, 128), c2),
        pl.BlockSpec((1, 128), c2),
    ]
    out_specs = [
        pl.BlockSpec((TILE, D_MODEL), cur),
        pl.BlockSpec((TILE, D_MODEL), cur),
        pl.BlockSpec((N_EXPERTS, TILE), lambda i, sq, ps, ct: (0, i)),
    ]
    grid_spec = pltpu.PrefetchScalarGridSpec(
        num_scalar_prefetch=3,
        grid=(n // TILE,),
        in_specs=in_specs,
        out_specs=out_specs,
        scratch_shapes=[pltpu.VMEM((TILE + 2 * HALO, POOL_WIDTH), f32),
                        pltpu.VMEM((TILE + 2 * HALO, CONV_WIDTH), f32),
                        pltpu.VMEM((3 * TILE, ATTN_WIDTH), bf16),
                        pltpu.VMEM((3 * TILE, ATTN_WIDTH), bf16),
                        pltpu.VMEM((TILE, ATTN_WIDTH), bf16),
                        pltpu.VMEM((ROWS_PER_TILE * N_PAIRS, 2 * GRID_W, WIN_ROWS * GRID_W), f32),
                        pltpu.VMEM((ROWS_PER_TILE * N_PAIRS, 2 * GRID_W, WIN_ROWS * GRID_W), bf16)],
    )
    return pl.pallas_call(
        _mixer_kernel,
        grid_spec=grid_spec,
        out_shape=[jax.ShapeDtypeStruct((n, D_MODEL), f32),
                   jax.ShapeDtypeStruct((n, D_MODEL), f32),
                   jax.ShapeDtypeStruct((N_EXPERTS, n), f32)],
        compiler_params=_cparams(("arbitrary",)),
        name="mixer",
    )(seq, pos, cnt,
      x, zu, zu, zu, zq, zk, zk, zk, zv, zv, zv, zb, zc, zc, zc, zx, zx, zx, zg,
      ada_l, ng, poolw_bf, pscale, bias_tab, convw, wb_bf, wout_bf, rw_hi, rw_lo, rb)


def _bias_table(rpb_l):
    c = np.arange(GRID_W)[:, None]
    kc = np.arange(GRID_W)[None, :]
    cs = np.clip(c - WIN_COLS // 2, 0, GRID_W - WIN_COLS)
    valid = (kc >= cs) & (kc < cs + WIN_COLS)
    didx = np.clip(kc - c + WIN_COLS - 1, 0, 2 * WIN_COLS - 2)
    select = (didx[:, :, None] == np.arange(2 * WIN_COLS - 1)).astype(np.float32)
    dense = jnp.einsum('hrd,ckd->hrck', rpb_l, jnp.asarray(select), precision=lax.Precision.HIGHEST)
    dense = jnp.where(valid[None, None], dense, NEG_BIG)
    slabs = [dense[:, WIN_ROWS - 1 - s:2 * WIN_ROWS - 1 - s] for s in range(WIN_ROWS)]
    tab = jnp.stack(slabs, axis=0)
    tab = tab.transpose(0, 1, 2, 4, 3)
    tab = tab.reshape(WIN_ROWS, N_PAIRS, 2, WIN_ROWS * GRID_W, GRID_W).transpose(0, 1, 3, 2, 4)
    return tab.reshape(WIN_ROWS, N_PAIRS, WIN_ROWS * GRID_W, 2 * GRID_W).astype(bf16)


def _router_kernel(lg_ref, lp_ref, gt_ref, tc_ref):
    l = lg_ref[...]
    eio = lax.broadcasted_iota(i32, l.shape, 0)
    vals, hots = [], []
    for k in range(TOP_K):
        m = jnp.max(l, axis=0, keepdims=True)
        idx = jnp.min(jnp.where(l == m, eio, N_EXPERTS), axis=0, keepdims=True)
        hot = eio == idx
        l = jnp.where(hot, -jnp.inf, l)
        vals.append(m)
        hots.append(hot)
    ex = [jnp.exp(v - vals[0]) for v in vals]
    den = ex[0] + ex[1] + ex[2] + ex[3]
    for k in range(TOP_K):
        gt_ref[k:k + 1, :] = ex[k] / den

    hot_all = jnp.zeros(l.shape, f32)
    for hot in hots:
        hot_all = hot_all + hot.astype(f32)
    rr = lax.broadcasted_iota(i32, (SORT_TILE, SORT_TILE), 0)
    cc = lax.broadcasted_iota(i32, (SORT_TILE, SORT_TILE), 1)
    tri = (rr <= cc).astype(bf16)
    pref = jnp.dot(hot_all.astype(bf16), tri, preferred_element_type=f32)
    cnt = jnp.broadcast_to(pref[:, SORT_TILE - 1:SORT_TILE], (N_EXPERTS, 128))
    padded = jnp.floor((cnt + (CHUNK - 1)) * (1.0 / CHUNK)) * CHUNK
    er = lax.broadcasted_iota(i32, (N_EXPERTS, N_EXPERTS), 0)
    ec = lax.broadcasted_iota(i32, (N_EXPERTS, N_EXPERTS), 1)
    before = (ec < er).astype(f32)
    start = jnp.dot(before, padded, preferred_element_type=f32, precision=lax.Precision.HIGHEST)
    tot = start[:, 0:1] + pref - 1.0
    for k in range(TOP_K):
        lp_ref[k:k + 1, :] = jnp.sum(jnp.where(hots[k], tot, 0.0), axis=0, keepdims=True).astype(i32)
    tc_ref[...] = cnt.astype(i32)


def _router(logits_t):
    n = logits_t.shape[1]
    blk = lambda i: (0, i)
    return pl.pallas_call(
        _router_kernel,
        grid=(n // SORT_TILE,),
        in_specs=[pl.BlockSpec((N_EXPERTS, SORT_TILE), blk)],
        out_specs=[pl.BlockSpec((TOP_K, SORT_TILE), blk),
                   pl.BlockSpec((TOP_K, SORT_TILE), blk),
                   pl.BlockSpec((None, N_EXPERTS, 128), lambda i: (i, 0, 0))],
        out_shape=[jax.ShapeDtypeStruct((TOP_K, n), i32),
                   jax.ShapeDtypeStruct((TOP_K, n), f32),
                   jax.ShapeDtypeStruct((n // SORT_TILE, N_EXPERTS, 128), i32)],
        compiler_params=_cparams(("arbitrary",)),
        name="router",
    )(logits_t)


def _to_token_tiles(dst_ref, val):
    rows = val.shape[0]
    for s in range(8):
        dst_ref[pl.ds(s, rows, stride=8), :] = val[:, s * 128:(s + 1) * 128]


def _from_token_tiles(src_ref, rows):
    return jnp.concatenate([src_ref[pl.ds(s, rows, stride=8), :] for s in range(8)], axis=1)


def _sort_kernel(lp_ref, h_ref, o_ref, tmp):
    _to_token_tiles(tmp, h_ref[...])
    o_ref[...] = jnp.zeros_like(o_ref)

    def body(t, carry):
        v = tmp[pl.ds(pl.multiple_of(t * 8, 8), 8), :]
        for k in range(TOP_K):
            o_ref[pl.ds(pl.multiple_of(lp_ref[k, t] * 8, 8), 8), :] = v
        return carry

    lax.fori_loop(0, SORT_TILE, body, 0, unroll=8)


def _local_sort(h2, lpos):
    n = h2.shape[0]
    nt = n // SORT_TILE
    return pl.pallas_call(
        _sort_kernel,
        grid=(nt,),
        in_specs=[pl.BlockSpec((TOP_K, SORT_TILE), lambda i: (0, i), memory_space=pltpu.SMEM),
                  pl.BlockSpec((SORT_TILE, D_MODEL), lambda i: (i, 0))],
        out_specs=pl.BlockSpec((LOCAL_ROWS * 8, 128), lambda i: (i, 0)),
        out_shape=jax.ShapeDtypeStruct((nt * LOCAL_ROWS * 8, 128), f32),
        scratch_shapes=[pltpu.VMEM((SORT_TILE * 8, 128), f32)],
        compiler_params=_cparams(("arbitrary",)),
        name="local_sort",
    )(lpos, h2)


def _expert_kernel(be_ref, nu_ref, src_ref, srcn_ref, dst_ref, xs_ref, wgu_ref, bgu_ref, wd_ref, bd_ref,
                   ys_ref, xbuf, ybuf, wgu_bf, wd_bf, act, sem_in, sem_out):
    b = pl.program_id(0)
    nb = pl.num_programs(0)
    nu = nu_ref[0]
    slot = b % 2
    e = be_ref[b]
    prev = be_ref[jnp.maximum(b - 1, 0)]
    chunk_rows = CHUNK * 8

    def gather_copy(idx_ref, c, s):
        src = pl.multiple_of(idx_ref[0, 0, c] * 8, chunk_rows)
        return pltpu.make_async_copy(xs_ref.at[pl.ds(src, chunk_rows), :],
                                     xbuf.at[s, pl.ds(c * chunk_rows, chunk_rows), :], sem_in.at[s])

    def scatter_copy(c, s):
        dst = pl.multiple_of(dst_ref[0, 0, c] * 8, chunk_rows)
        return pltpu.make_async_copy(ybuf.at[s, pl.ds(c * chunk_rows, chunk_rows), :],
                                     ys_ref.at[pl.ds(dst, chunk_rows), :], sem_out.at[s])

    def start_gather(idx_ref, s):
        for c in range(BLOCK_CHUNKS):
            gather_copy(idx_ref, c, s).start()

    def wait_gather(s):
        for c in range(BLOCK_CHUNKS):
            gather_copy(src_ref, c, s).wait()

    def wait_scatter(s):
        for c in range(BLOCK_CHUNKS):
            scatter_copy(c, s).wait()

    @pl.when(b == 0)
    def _():
        start_gather(src_ref, 0)

    @pl.when(b + 1 < nu)
    def _():
        start_gather(srcn_ref, 1 - slot)

    @pl.when((b >= 2) & (b - 2 < nu))
    def _():
        wait_scatter(slot)

    @pl.when((b == 0) | (e != prev))
    def _():
        wgu_bf[...] = wgu_ref[...].astype(bf16)
        wd_bf[...] = wd_ref[...].astype(bf16)

    @pl.when(b < nu)
    def _():
        wait_gather(slot)
        x = _from_token_tiles(xbuf.at[slot], SLOT_BLOCK).astype(bf16)
        for c0 in range(0, D_EXPERT, 512):
            gate = jnp.dot(x, wgu_bf[:, c0:c0 + 512], preferred_element_type=f32) + bgu_ref[:, c0:c0 + 512]
            up = (jnp.dot(x, wgu_bf[:, D_EXPERT + c0:D_EXPERT + c0 + 512], preferred_element_type=f32)
                  + bgu_ref[:, D_EXPERT + c0:D_EXPERT + c0 + 512])
            gate = jnp.minimum(gate, SWIGLU_LIMIT)
            up = jnp.clip(up, -SWIGLU_LIMIT, SWIGLU_LIMIT)
            act[:, c0:c0 + 512] = ((up + 1.0) * (gate * jax.nn.sigmoid(gate * SWIGLU_ALPHA))).astype(bf16)
        y = jnp.dot(act[...], wd_bf[...], preferred_element_type=f32) + bd_ref[...]
        _to_token_tiles(ybuf.at[slot], y)
        for c in range(BLOCK_CHUNKS):
            scatter_copy(c, slot).start()

    @pl.when(b == nb - 1)
    def _():
        @pl.when((b >= 1) & (b - 1 < nu))
        def _():
            wait_scatter(1 - slot)

        @pl.when(b < nu)
        def _():
            wait_scatter(slot)


def _experts(xs, chunk_src, chunk_dst, blk_expert, n_used, layer, w_gu, b_gu, w_down, b_down):
    nblk = chunk_src.shape[0]
    depth = w_gu.shape[0]
    wsel = lambda b, be, nu: (layer, be[b], 0, 0)
    cur = lambda b, be, nu: (b, 0, 0)
    nxt = lambda b, be, nu: (jnp.minimum(b + 1, nblk - 1), 0, 0)
    idx_spec = lambda im: pl.BlockSpec((1, 1, BLOCK_CHUNKS), im, memory_space=pltpu.SMEM)
    grid_spec = pltpu.PrefetchScalarGridSpec(
        num_scalar_prefetch=2,
        grid=(nblk,),
        in_specs=[idx_spec(cur), idx_spec(nxt), idx_spec(cur),
                  pl.BlockSpec(memory_space=pl.ANY),
                  pl.BlockSpec((None, None, D_MODEL, 2 * D_EXPERT), wsel),
                  pl.BlockSpec((None, None, 1, 2 * D_EXPERT), wsel),
                  pl.BlockSpec((None, None, D_EXPERT, D_MODEL), wsel),
                  pl.BlockSpec((None, None, 1, D_MODEL), wsel)],
        out_specs=pl.BlockSpec(memory_space=pl.ANY),
        scratch_shapes=[pltpu.VMEM((2, SLOT_BLOCK * 8, 128), f32),
                        pltpu.VMEM((2, SLOT_BLOCK * 8, 128), f32),
                        pltpu.VMEM((D_MODEL, 2 * D_EXPERT), bf16),
                        pltpu.VMEM((D_EXPERT, D_MODEL), bf16),
                        pltpu.VMEM((SLOT_BLOCK, D_EXPERT), bf16),
                        pltpu.SemaphoreType.DMA((2,)),
                        pltpu.SemaphoreType.DMA((2,))],
    )
    return pl.pallas_call(
        _expert_kernel,
        grid_spec=grid_spec,
        out_shape=jax.ShapeDtypeStruct((xs.shape[0] + 2 * SLOT_BLOCK * 8, 128), f32),
        compiler_params=_cparams(("arbitrary",)),
        name="experts",
    )(blk_expert, n_used, chunk_src, chunk_src, chunk_dst, xs, w_gu,
      b_gu.reshape(depth, N_EXPERTS, 1, 2 * D_EXPERT), w_down,
      b_down.reshape(depth, N_EXPERTS, 1, D_MODEL))


def _combine_kernel(seq_ref, lp_ref, gt_ref, x_ref, ys_ref, ada_ref, ng_ref, xo_ref, ff):
    def body(t, carry):
        acc = gt_ref[0, t] * ys_ref[pl.ds(pl.multiple_of(lp_ref[0, t] * 8, 8), 8), :]
        for k in range(1, TOP_K):
            acc = acc + gt_ref[k, t] * ys_ref[pl.ds(pl.multiple_of(lp_ref[k, t] * 8, 8), 8), :]
        ff[pl.ds(pl.multiple_of(t * 8, 8), 8), :] = acc
        return carry

    lax.fori_loop(0, SORT_TILE, body, 0, unroll=8)
    s = seq_ref[pl.program_id(0)]
    gt2 = ada_ref[pl.ds(s, 1), 5 * D_MODEL:6 * D_MODEL]
    xo_ref[...] = x_ref[...] + gt2 * _rms(_from_token_tiles(ff, SORT_TILE), ng_ref[3:4, :])


def _combine(x, ys, lpos, gates, ada_l, ng, seq_lens):
    n = x.shape[0]
    seq, _, _ = _tile_meta(seq_lens, SORT_TILE)
    row = lambda i, sq: (i, 0)
    col = lambda i, sq: (0, i)
    const = lambda i, sq: (0, 0)
    grid_spec = pltpu.PrefetchScalarGridSpec(
        num_scalar_prefetch=1,
        grid=(n // SORT_TILE,),
        in_specs=[pl.BlockSpec((TOP_K, SORT_TILE), col, memory_space=pltpu.SMEM),
                  pl.BlockSpec((TOP_K, SORT_TILE), col, memory_space=pltpu.SMEM),
                  pl.BlockSpec((SORT_TILE, D_MODEL), row),
                  pl.BlockSpec((LOCAL_ROWS * 8, 128), row),
                  pl.BlockSpec((SEQ_PAD, N_MOD * D_MODEL), const),
                  pl.BlockSpec((4, D_MODEL), const)],
        out_specs=pl.BlockSpec((SORT_TILE, D_MODEL), row),
        scratch_shapes=[pltpu.VMEM((SORT_TILE * 8, 128), f32)],
    )
    return pl.pallas_call(
        _combine_kernel,
        grid_spec=grid_spec,
        out_shape=jax.ShapeDtypeStruct((n, D_MODEL), f32),
        compiler_params=_cparams(("arbitrary",)),
        name="combine",
    )(seq, lpos, gates, x, ys, ada_l, ng)


def _chunk_plan(tile_counts):
    nt = tile_counts.shape[0]
    nch = (tile_counts + CHUNK - 1) // CHUNK
    rows = nch * CHUNK
    loff = jnp.cumsum(rows, axis=1) - rows
    cend = jnp.cumsum(nch, axis=0)
    cstart = cend - nch
    total = cend[-1]
    nblk_e = (total + BLOCK_CHUNKS - 1) // BLOCK_CHUNKS
    bend = jnp.cumsum(nblk_e)
    bstart = bend - nblk_e
    n_used = bend[-1].reshape(1)
    max_chunks = nt * (TOP_K * SORT_TILE // CHUNK + N_EXPERTS)
    nblk = max_chunks // BLOCK_CHUNKS + N_EXPERTS
    b = jnp.arange(nblk, dtype=i32)
    blk_expert = jnp.minimum(jnp.sum((b[:, None] >= bend[None, :]).astype(i32), axis=1), N_EXPERTS - 1)
    onehot = (blk_expert[:, None] == jnp.arange(N_EXPERTS, dtype=i32)).astype(f32)

    def of_block(per_expert):
        table = per_expert.astype(f32).reshape(-1, N_EXPERTS).T
        return jnp.dot(onehot, table, precision=lax.Precision.HIGHEST).astype(i32)

    cend_b, cstart_b, loff_b = of_block(cend), of_block(cstart), of_block(loff)
    total_b, bstart_b = of_block(total)[:, 0], of_block(bstart)[:, 0]
    q = (b - bstart_b)[:, None] * BLOCK_CHUNKS + jnp.arange(BLOCK_CHUNKS, dtype=i32)[None, :]
    valid = (q < total_b[:, None]) & (b < n_used[0])[:, None]
    tile = jnp.minimum(jnp.sum((cend_b[:, None, :] <= q[:, :, None]).astype(i32), axis=2), nt - 1)
    pick = tile[:, :, None] == jnp.arange(nt, dtype=i32)
    cs = jnp.sum(jnp.where(pick, cstart_b[:, None, :], 0), axis=2)
    lo = jnp.sum(jnp.where(pick, loff_b[:, None, :], 0), axis=2)
    row = tile * LOCAL_ROWS + lo + (q - cs) * CHUNK
    spare = (nt * LOCAL_ROWS + (b % 2)[:, None] * SLOT_BLOCK
             + jnp.arange(BLOCK_CHUNKS, dtype=i32)[None, :] * CHUNK)
    shape = (nblk, 1, BLOCK_CHUNKS)
    chunk_src = jnp.where(valid, row, 0).astype(i32).reshape(shape)
    chunk_dst = jnp.where(valid, row, spare).astype(i32).reshape(shape)
    return chunk_src, chunk_dst, blk_expert.astype(i32), n_used.astype(i32)


def _trunk(x, c_all, seq_lens, ada_w, ada_b, norm_g, w_in, pool_w, pool_scale, rpb, conv_w, w_branch,
           w_out, router_w, router_b, expert_w_gu, expert_b_gu, expert_w_down, expert_b_down):
    depth = ada_w.shape[0]
    ada = _ada(c_all, ada_w, ada_b)
    for l in range(depth):
        ada_l = ada[l]
        ng = norm_g[l]
        zs = _inproj(x, ada_l, ng[0:1], w_in[l].astype(bf16), seq_lens)
        rw = jnp.pad(router_w[l], ((0, 0), (0, 128 - N_EXPERTS)))
        rw_hi = rw.astype(bf16)
        rw_lo = (rw - rw_hi.astype(f32)).astype(bf16)
        x, h2, logits_t = _mixer(
            x, zs, ada_l, ng, pool_w[l].astype(bf16), pool_scale[l].reshape(1, POOL_WIDTH),
            _bias_table(rpb[l]), conv_w[l], w_branch[l].astype(bf16), w_out[l].astype(bf16),
            rw_hi, rw_lo,
            jnp.pad(router_b[l], (0, 128 - N_EXPERTS)).reshape(1, 128), seq_lens)
        lpos, gates, tile_counts = _router(logits_t)
        chunk_src, chunk_dst, blk_expert, n_used = _chunk_plan(tile_counts[:, :, 0])
        xs = _local_sort(h2, lpos)
        ys = _experts(xs, chunk_src, chunk_dst, blk_expert, n_used, l, expert_w_gu, expert_b_gu,
                      expert_w_down, expert_b_down)
        x = _combine(x, ys, lpos, gates, ada_l, ng, seq_lens)
    return x


def kernel(x_prompt, x_sample, c_prompt, c_sample, ada_w, ada_b, norm_g, w_in, pool_w, pool_scale, rpb,
           conv_w, w_branch, w_out, router_w, router_b, expert_w_gu, expert_b_gu, expert_w_down,
           expert_b_down):
    bp, lp, d = x_prompt.shape
    bs, ls, _ = x_sample.shape
    assert d == D_MODEL and bp + bs <= SEQ_PAD
    assert lp % IN_TILE == 0 and ls % IN_TILE == 0 and lp >= 2 * TILE and ls >= 2 * TILE
    x = jnp.concatenate([x_prompt.reshape(bp * lp, d), x_sample.reshape(bs * ls, d)], axis=0)
    c_all = jnp.concatenate([c_prompt, c_sample, jnp.zeros((SEQ_PAD - bp - bs, d), f32)], axis=0)
    seq_lens = ((bp, lp), (bs, ls))
    y = _trunk(x, c_all, seq_lens, ada_w, ada_b, norm_g, w_in, pool_w, pool_scale, rpb, conv_w, w_branch,
               w_out, router_w, router_b, expert_w_gu, expert_b_gu, expert_w_down, expert_b_down)
    return (y[:bp * lp].reshape(bp, lp, d), y[bp * lp:].reshape(bs, ls, d))
```

```python
import functools

import numpy as np
import jax
import jax.numpy as jnp
from jax import lax
from jax.experimental import pallas as pl
from jax.experimental.pallas import tpu as pltpu

f32 = jnp.float32
bf16 = jnp.bfloat16
i32 = jnp.int32

D_MODEL = 1024
GRID_W = 64
POOL_WINDOWS = (2, 4, 8, 16)
POOL_GROUP = 128
POOL_WIDTH = 512
N_HEADS = 8
HEAD_DIM = 64
N_PAIRS = N_HEADS // 2
ATTN_WIDTH = 512
WIN_ROWS = 8
WIN_COLS = 16
CONV_WIDTH = 512
N_BRANCH = 3
IN_COLS = POOL_WIDTH + 3 * ATTN_WIDTH + 3 * CONV_WIDTH + N_BRANCH * D_MODEL
N_EXPERTS = 32
TOP_K = 4
D_EXPERT = D_MODEL
SWIGLU_ALPHA = 1.702
SWIGLU_LIMIT = 7.0
RMS_EPS = 1e-6
N_MOD = 6
NEG_BIG = -1e30

SEQ_PAD = 16
ROWS_PER_TILE = 4
TILE = ROWS_PER_TILE * GRID_W
HALO = 8
IN_TILE = 512
SORT_TILE = 512
CHUNK = 8
SLOT_BLOCK = 512
BLOCK_CHUNKS = SLOT_BLOCK // CHUNK
LOCAL_ROWS = TOP_K * SORT_TILE + N_EXPERTS * CHUNK
VMEM_LIMIT = 56 * 1024 * 1024


def _cparams(sem):
    return pltpu.CompilerParams(dimension_semantics=sem, vmem_limit_bytes=VMEM_LIMIT)


def _tile_meta(seq_lens, tile):
    seq, pos, cnt = [], [], []
    sid = 0
    for nseq, length in seq_lens:
        nt = length // tile
        for _ in range(nseq):
            for p in range(nt):
                seq.append(sid)
                pos.append(p)
                cnt.append(nt)
            sid += 1
    return (jnp.asarray(np.array(seq, np.int32)), jnp.asarray(np.array(pos, np.int32)),
            jnp.asarray(np.array(cnt, np.int32)))


def _rms(x, g):
    ms = jnp.mean(x * x, axis=-1, keepdims=True)
    return x * lax.rsqrt(ms + RMS_EPS) * g


def _ada_kernel(c_ref, w_ref, b_ref, o_ref):
    c = c_ref[...]
    s = c * jax.nn.sigmoid(c)
    o_ref[...] = jnp.dot(s, w_ref[...], preferred_element_type=f32,
                         precision=lax.Precision.HIGHEST) + b_ref[...]


def _ada(c_all, ada_w, ada_b):
    depth = ada_w.shape[0]
    return pl.pallas_call(
        _ada_kernel,
        grid=(depth, N_MOD),
        in_specs=[pl.BlockSpec((SEQ_PAD, D_MODEL), lambda l, j: (0, 0)),
                  pl.BlockSpec((None, D_MODEL, D_MODEL), lambda l, j: (l, 0, j)),
                  pl.BlockSpec((None, 1, D_MODEL), lambda l, j: (l, 0, j))],
        out_specs=pl.BlockSpec((None, SEQ_PAD, D_MODEL), lambda l, j: (l, 0, j)),
        out_shape=jax.ShapeDtypeStruct((depth, SEQ_PAD, N_MOD * D_MODEL), f32),
        compiler_params=_cparams(("arbitrary", "arbitrary")),
        name="ada",
    )(c_all, ada_w, ada_b.reshape(depth, 1, N_MOD * D_MODEL))


_IN_SPLITS = (POOL_WIDTH, ATTN_WIDTH, ATTN_WIDTH, ATTN_WIDTH, CONV_WIDTH, CONV_WIDTH, CONV_WIDTH,
              N_BRANCH * D_MODEL)


def _inproj_kernel(seq_ref, x_ref, ada_ref, g_ref, w_ref, *out_refs):
    s = seq_ref[pl.program_id(0)]
    mod = ada_ref[pl.ds(s, 1), :]
    h = _rms(x_ref[...], g_ref[...]) * (1.0 + mod[:, D_MODEL:2 * D_MODEL]) + mod[:, 0:D_MODEL]
    h = h.astype(bf16)
    col = 0
    for o_ref in out_refs:
        width = o_ref.shape[1]
        for c0 in range(0, width, 512):
            o_ref[:, c0:c0 + 512] = jnp.dot(h, w_ref[:, col + c0:col + c0 + 512],
                                            preferred_element_type=f32).astype(bf16)
        col += width


def _inproj(x, ada_l, g, w_in_bf, seq_lens):
    n = x.shape[0]
    seq, _, _ = _tile_meta(seq_lens, IN_TILE)
    row = lambda i, s: (i, 0)
    const = lambda i, s: (0, 0)
    grid_spec = pltpu.PrefetchScalarGridSpec(
        num_scalar_prefetch=1,
        grid=(n // IN_TILE,),
        in_specs=[pl.BlockSpec((IN_TILE, D_MODEL), row),
                  pl.BlockSpec((SEQ_PAD, N_MOD * D_MODEL), const),
                  pl.BlockSpec((1, D_MODEL), const),
                  pl.BlockSpec((D_MODEL, IN_COLS), const, pipeline_mode=pl.Buffered(1))],
        out_specs=[pl.BlockSpec((IN_TILE, w), row) for w in _IN_SPLITS],
    )
    return pl.pallas_call(
        _inproj_kernel,
        grid_spec=grid_spec,
        out_shape=[jax.ShapeDtypeStruct((n, w), bf16) for w in _IN_SPLITS],
        compiler_params=_cparams(("arbitrary",)),
        name="inproj",
    )(seq, x, ada_l, g, w_in_bf)


def _mixer_kernel(seq_ref, pos_ref, nt_ref,
                  x_ref, u_ref, up_ref, un_ref, q_ref, kp_ref, kc_ref, kn_ref, vp_ref, vc_ref, vn_ref,
                  b_ref, c_ref, cp_ref, cn_ref, xc_ref, xcp_ref, xcn_ref, gl_ref,
                  ada_ref, ng_ref, poolw_ref, pscale_ref, bias_ref, convw_ref, wb_ref, wout_ref,
                  rwh_ref, rwl_ref, rb_ref,
                  xo_ref, h2_ref, lg_ref,
                  uext, vext, kbuf, vbuf, pb_ref, sc_ref, e_ref):
    i = pl.program_id(0)
    s = seq_ref[i]
    pos = pos_ref[i]
    nt = nt_ref[i]
    first = pos == 0
    last = pos == nt - 1
    seq_len = nt * TILE

    uext[HALO:HALO + TILE, :] = u_ref[...].astype(f32)
    uext[0:HALO, :] = jnp.where(first, 0.0, up_ref[...].astype(f32)[GRID_W - HALO:GRID_W, :])
    uext[HALO + TILE:2 * HALO + TILE, :] = jnp.where(last, 0.0, un_ref[...].astype(f32)[0:HALO, :])

    vext[HALO:HALO + TILE, :] = c_ref[...].astype(f32) * xc_ref[...].astype(f32)
    vprev = cp_ref[...].astype(f32) * xcp_ref[...].astype(f32)
    vnext = cn_ref[...].astype(f32) * xcn_ref[...].astype(f32)
    vext[0:HALO, :] = jnp.where(first, 0.0, vprev[GRID_W - HALO:GRID_W, :])
    vext[HALO + TILE:2 * HALO + TILE, :] = jnp.where(last, 0.0, vnext[0:HALO, :])

    kbuf[0:TILE, :] = kp_ref[...]
    kbuf[TILE:2 * TILE, :] = kc_ref[...]
    kbuf[2 * TILE:3 * TILE, :] = kn_ref[...]
    vbuf[0:TILE, :] = vp_ref[...]
    vbuf[TILE:2 * TILE, :] = vc_ref[...]
    vbuf[2 * TILE:3 * TILE, :] = vn_ref[...]

    width = 2 * HEAD_DIM
    lane = lax.broadcasted_iota(i32, (1, width), 1)
    low = lane < HEAD_DIM
    rows = nt * ROWS_PER_TILE
    qscale = jnp.asarray(HEAD_DIM ** -0.5, bf16)
    eye = (lax.broadcasted_iota(i32, (width, width), 0)
           == lax.broadcasted_iota(i32, (width, width), 1)).astype(bf16)
    ones = jnp.ones((WIN_ROWS * GRID_W, width), bf16)

    def window(j):
        r = pos * ROWS_PER_TILE + j
        rs = jnp.clip(r - WIN_ROWS // 2, 0, rows - WIN_ROWS)
        koff = pl.multiple_of((rs - pos * ROWS_PER_TILE + ROWS_PER_TILE) * GRID_W, GRID_W)
        return r - rs, koff

    items = [(j, p) for j in range(ROWS_PER_TILE) for p in range(N_PAIRS)]
    for n, (j, p) in enumerate(items):
        si, koff = window(j)
        cols = slice(p * width, (p + 1) * width)
        qp = q_ref[j * GRID_W:(j + 1) * GRID_W, cols] * qscale
        zero = jnp.zeros_like(qp)
        qs = jnp.concatenate([jnp.where(low, qp, zero), jnp.where(low, zero, qp)], axis=0)
        lhs = jnp.concatenate([qs, eye], axis=1)
        rhs = jnp.concatenate([kbuf[pl.ds(koff, WIN_ROWS * GRID_W), cols], bias_ref[si, p]], axis=1)
        sc_ref[n] = lax.dot_general(lhs, rhs, (((1,), (1,)), ((), ())), preferred_element_type=f32)
    for n in range(len(items)):
        sc = sc_ref[n]
        e_ref[n] = jnp.exp(sc - jnp.max(sc, axis=-1, keepdims=True)).astype(bf16)
    for n, (j, p) in enumerate(items):
        _, koff = window(j)
        cols = slice(p * width, (p + 1) * width)
        rhs = jnp.concatenate([vbuf[pl.ds(koff, WIN_ROWS * GRID_W), cols], ones], axis=1)
        oe = jnp.dot(e_ref[n], rhs, preferred_element_type=f32)
        o = oe[:, 0:width] / oe[:, width:2 * width]
        pb_ref[j * GRID_W:(j + 1) * GRID_W, cols] = jnp.where(low, o[0:GRID_W], o[GRID_W:2 * GRID_W]).astype(bf16)

    t = pos * TILE + lax.broadcasted_iota(i32, (TILE, 1), 0)
    pa_parts = []
    for g, w in enumerate(POOL_WINDOWS):
        cols = slice(g * POOL_GROUP, (g + 1) * POOL_GROUP)
        half = w // 2
        acc = uext[HALO - half:HALO - half + TILE, cols]
        for d in range(-half + 1, half):
            acc = acc + uext[HALO + d:HALO + d + TILE, cols]
        cnt = (jnp.minimum(t + half, seq_len) - jnp.maximum(t - half, 0)).astype(f32)
        m = acc / cnt - uext[HALO:HALO + TILE, cols]
        z = jnp.dot(m.astype(bf16), poolw_ref[g], preferred_element_type=f32)
        pa_parts.append(z * pscale_ref[:, cols])
    pa = jnp.concatenate(pa_parts, axis=1).astype(bf16)

    y = (convw_ref[0:1, :] * vext[HALO - 1:HALO - 1 + TILE, :]
         + convw_ref[1:2, :] * vext[HALO:HALO + TILE, :]
         + convw_ref[2:3, :] * vext[HALO + 1:HALO + 1 + TILE, :])
    pc = (b_ref[...].astype(f32) * y).astype(bf16)

    branches = (pa, pb_ref[...], pc)
    merged = None
    for n in range(N_BRANCH):
        proj = jnp.dot(branches[n], wb_ref[n], preferred_element_type=f32)
        gate = jax.nn.sigmoid(gl_ref[:, n * D_MODEL:(n + 1) * D_MODEL].astype(f32))
        merged = gate * proj if merged is None else merged + gate * proj
    mix = jnp.dot(merged.astype(bf16), wout_ref[...], preferred_element_type=f32)

    mod = ada_ref[pl.ds(s, 1), :]
    gt1 = mod[:, 2 * D_MODEL:3 * D_MODEL]
    sh2 = mod[:, 3 * D_MODEL:4 * D_MODEL]
    sc2 = mod[:, 4 * D_MODEL:5 * D_MODEL]
    xn = x_ref[...] + gt1 * _rms(mix, ng_ref[1:2, :])
    xo_ref[...] = xn
    h2 = _rms(xn, ng_ref[2:3, :]) * (1.0 + sc2) + sh2
    _to_token_tiles(h2_ref, h2)
    h_hi = h2.astype(bf16)
    h_lo = (h2 - h_hi.astype(f32)).astype(bf16)
    logits = (jnp.dot(h_hi, rwh_ref[...], preferred_element_type=f32)
              + jnp.dot(h_lo, rwh_ref[...], preferred_element_type=f32)
              + jnp.dot(h_hi, rwl_ref[...], preferred_element_type=f32)
              + rb_ref[...])
    lg_ref[...] = logits.T[0:N_EXPERTS, :]


def _mixer(x, zs, ada_l, ng, poolw_bf, pscale, bias_tab, convw, wb_bf, wout_bf, rw_hi, rw_lo, rb,
           seq_lens):
    n = x.shape[0]
    zu, zq, zk, zv, zb, zc, zx, zg = zs
    seq, pos, cnt = _tile_meta(seq_lens, TILE)
    per = TILE // GRID_W

    cur = lambda i, sq, ps, ct: (i, 0)
    prev_t = lambda i, sq, ps, ct: (jnp.where(ps[i] > 0, i - 1, i), 0)
    next_t = lambda i, sq, ps, ct: (jnp.where(ps[i] < ct[i] - 1, i + 1, i), 0)
    prev_h = lambda i, sq, ps, ct: (jnp.where(ps[i] > 0, i * per - 1, i * per), 0)
    next_h = lambda i, sq, ps, ct: (jnp.where(ps[i] < ct[i] - 1, (i + 1) * per, i * per), 0)
    c2 = lambda i, sq, ps, ct: (0, 0)
    c3 = lambda i, sq, ps, ct: (0, 0, 0)
    c4 = lambda i, sq, ps, ct: (0, 0, 0, 0)

    tile512 = lambda im: pl.BlockSpec((TILE, 512), im)
    halo512 = lambda im: pl.BlockSpec((GRID_W, 512), im)
    once = pl.Buffered(1)
    in_specs = [
        pl.BlockSpec((TILE, D_MODEL), cur),
        tile512(cur), halo512(prev_h), halo512(next_h),
        tile512(cur),
        tile512(prev_t), tile512(cur), tile512(next_t),
        tile512(prev_t), tile512(cur), tile512(next_t),
        tile512(cur),
        tile512(cur), halo512(prev_h), halo512(next_h),
        tile512(cur), halo512(prev_h), halo512(next_h),
        pl.BlockSpec((TILE, N_BRANCH * D_MODEL), cur),
        pl.BlockSpec((SEQ_PAD, N_MOD * D_MODEL), c2),
        pl.BlockSpec((4, D_MODEL), c2),
        pl.BlockSpec((4, POOL_GROUP, POOL_GROUP), c3),
        pl.BlockSpec((1, POOL_WIDTH), c2),
        pl.BlockSpec((WIN_ROWS, N_PAIRS, WIN_ROWS * GRID_W, 2 * GRID_W), c4, pipeline_mode=once),
        pl.BlockSpec((3, CONV_WIDTH), c2),
        pl.BlockSpec((N_BRANCH, POOL_WIDTH, D_MODEL), c3, pipeline_mode=once),
        pl.BlockSpec((D_MODEL, D_MODEL), c2, pipeline_mode=once),
        pl.BlockSpec((D_MODEL, 128), c2),
        pl.BlockSpec((D_MODEL, 128), c2),
        pl.BlockSpec((1, 128), c2),
    ]
    out_specs = [
        pl.BlockSpec((TILE, D_MODEL), cur),
        pl.BlockSpec((TILE * 8, 128), cur),
        pl.BlockSpec((N_EXPERTS, TILE), lambda i, sq, ps, ct: (0, i)),
    ]
    grid_spec = pltpu.PrefetchScalarGridSpec(
        num_scalar_prefetch=3,
        grid=(n // TILE,),
        in_specs=in_specs,
        out_specs=out_specs,
        scratch_shapes=[pltpu.VMEM((TILE + 2 * HALO, POOL_WIDTH), f32),
                        pltpu.VMEM((TILE + 2 * HALO, CONV_WIDTH), f32),
                        pltpu.VMEM((3 * TILE, ATTN_WIDTH), bf16),
                        pltpu.VMEM((3 * TILE, ATTN_WIDTH), bf16),
                        pltpu.VMEM((TILE, ATTN_WIDTH), bf16),
                        pltpu.VMEM((ROWS_PER_TILE * N_PAIRS, 2 * GRID_W, WIN_ROWS * GRID_W), f32),
                        pltpu.VMEM((ROWS_PER_TILE * N_PAIRS, 2 * GRID_W, WIN_ROWS * GRID_W), bf16)],
    )
    return pl.pallas_call(
        _mixer_kernel,
        grid_spec=grid_spec,
        out_shape=[jax.ShapeDtypeStruct((n, D_MODEL), f32),
                   jax.ShapeDtypeStruct((n * 8, 128), f32),
                   jax.ShapeDtypeStruct((N_EXPERTS, n), f32)],
        compiler_params=_cparams(("arbitrary",)),
        name="mixer",
    )(seq, pos, cnt,
      x, zu, zu, zu, zq, zk, zk, zk, zv, zv, zv, zb, zc, zc, zc, zx, zx, zx, zg,
      ada_l, ng, poolw_bf, pscale, bias_tab, convw, wb_bf, wout_bf, rw_hi, rw_lo, rb)


def _bias_table(rpb_l):
    c = np.arange(GRID_W)[:, None]
    kc = np.arange(GRID_W)[None, :]
    cs = np.clip(c - WIN_COLS // 2, 0, GRID_W - WIN_COLS)
    valid = (kc >= cs) & (kc < cs + WIN_COLS)
    didx = np.clip(kc - c + WIN_COLS - 1, 0, 2 * WIN_COLS - 2)
    select = (didx[:, :, None] == np.arange(2 * WIN_COLS - 1)).astype(np.float32)
    dense = jnp.einsum('hrd,ckd->hrck', rpb_l, jnp.asarray(select), precision=lax.Precision.HIGHEST)
    dense = jnp.where(valid[None, None], dense, NEG_BIG)
    slabs = [dense[:, WIN_ROWS - 1 - s:2 * WIN_ROWS - 1 - s] for s in range(WIN_ROWS)]
    tab = jnp.stack(slabs, axis=0)
    tab = tab.transpose(0, 1, 2, 4, 3)
    tab = tab.reshape(WIN_ROWS, N_PAIRS, 2, WIN_ROWS * GRID_W, GRID_W).transpose(0, 1, 3, 2, 4)
    return tab.reshape(WIN_ROWS, N_PAIRS, WIN_ROWS * GRID_W, 2 * GRID_W).astype(bf16)


def _router_kernel(lg_ref, lp_ref, gt_ref, tc_ref):
    l = lg_ref[...]
    eio = lax.broadcasted_iota(i32, l.shape, 0)
    vals, hots = [], []
    for k in range(TOP_K):
        m = jnp.max(l, axis=0, keepdims=True)
        idx = jnp.min(jnp.where(l == m, eio, N_EXPERTS), axis=0, keepdims=True)
        hot = eio == idx
        l = jnp.where(hot, -jnp.inf, l)
        vals.append(m)
        hots.append(hot)
    ex = [jnp.exp(v - vals[0]) for v in vals]
    den = ex[0] + ex[1] + ex[2] + ex[3]
    for k in range(TOP_K):
        gt_ref[k:k + 1, :] = ex[k] / den

    hot_all = jnp.zeros(l.shape, f32)
    for hot in hots:
        hot_all = hot_all + hot.astype(f32)
    rr = lax.broadcasted_iota(i32, (SORT_TILE, SORT_TILE), 0)
    cc = lax.broadcasted_iota(i32, (SORT_TILE, SORT_TILE), 1)
    tri = (rr <= cc).astype(bf16)
    pref = jnp.dot(hot_all.astype(bf16), tri, preferred_element_type=f32)
    cnt = jnp.broadcast_to(pref[:, SORT_TILE - 1:SORT_TILE], (N_EXPERTS, 128))
    padded = jnp.floor((cnt + (CHUNK - 1)) * (1.0 / CHUNK)) * CHUNK
    er = lax.broadcasted_iota(i32, (N_EXPERTS, N_EXPERTS), 0)
    ec = lax.broadcasted_iota(i32, (N_EXPERTS, N_EXPERTS), 1)
    before = (ec < er).astype(f32)
    start = jnp.dot(before, padded, preferred_element_type=f32, precision=lax.Precision.HIGHEST)
    tot = start[:, 0:1] + pref - 1.0
    for k in range(TOP_K):
        lp_ref[k:k + 1, :] = jnp.sum(jnp.where(hots[k], tot, 0.0), axis=0, keepdims=True).astype(i32)
    tc_ref[...] = cnt.astype(i32)


def _router(logits_t):
    n = logits_t.shape[1]
    blk = lambda i: (0, i)
    return pl.pallas_call(
        _router_kernel,
        grid=(n // SORT_TILE,),
        in_specs=[pl.BlockSpec((N_EXPERTS, SORT_TILE), blk)],
        out_specs=[pl.BlockSpec((TOP_K, SORT_TILE), blk),
                   pl.BlockSpec((TOP_K, SORT_TILE), blk),
                   pl.BlockSpec((None, N_EXPERTS, 128), lambda i: (i, 0, 0))],
        out_shape=[jax.ShapeDtypeStruct((TOP_K, n), i32),
                   jax.ShapeDtypeStruct((TOP_K, n), f32),
                   jax.ShapeDtypeStruct((n // SORT_TILE, N_EXPERTS, 128), i32)],
        compiler_params=_cparams(("arbitrary",)),
        name="router",
    )(logits_t)


def _to_token_tiles(dst_ref, val):
    rows = val.shape[0]
    for s in range(8):
        dst_ref[pl.ds(s, rows, stride=8), :] = val[:, s * 128:(s + 1) * 128]


def _from_token_tiles(src_ref, rows):
    return jnp.concatenate([src_ref[pl.ds(s, rows, stride=8), :] for s in range(8)], axis=1)


def _sort_kernel(lp_ref, h_ref, o_ref):
    o_ref[...] = jnp.zeros_like(o_ref)

    def body(t, carry):
        v = h_ref[pl.ds(pl.multiple_of(t * 8, 8), 8), :]
        for k in range(TOP_K):
            o_ref[pl.ds(pl.multiple_of(lp_ref[t * TOP_K + k], 8), 8), :] = v
        return carry

    lax.fori_loop(0, SORT_TILE, body, 0, unroll=8)


def _local_sort(h2, lrow):
    nt = h2.shape[0] // (SORT_TILE * 8)
    return pl.pallas_call(
        _sort_kernel,
        grid=(nt,),
        in_specs=[pl.BlockSpec((TOP_K * SORT_TILE,), lambda i: (i,), memory_space=pltpu.SMEM),
                  pl.BlockSpec((SORT_TILE * 8, 128), lambda i: (i, 0))],
        out_specs=pl.BlockSpec((LOCAL_ROWS * 8, 128), lambda i: (i, 0)),
        out_shape=jax.ShapeDtypeStruct((nt * LOCAL_ROWS * 8, 128), f32),
        compiler_params=_cparams(("arbitrary",)),
        name="local_sort",
    )(lrow, h2)


def _expert_kernel(be_ref, nu_ref, src_ref, srcn_ref, dst_ref, xs_ref, wgu_ref, bgu_ref, wd_ref, bd_ref,
                   ys_ref, xbuf, ybuf, wgu_bf, wd_bf, act, sem_in, sem_out):
    b = pl.program_id(0)
    nb = pl.num_programs(0)
    nu = nu_ref[0]
    slot = b % 2
    e = be_ref[b]
    prev = be_ref[jnp.maximum(b - 1, 0)]
    chunk_rows = CHUNK * 8

    def gather_copy(idx_ref, c, s):
        src = pl.multiple_of(idx_ref[0, 0, c] * 8, chunk_rows)
        return pltpu.make_async_copy(xs_ref.at[pl.ds(src, chunk_rows), :],
                                     xbuf.at[s, pl.ds(c * chunk_rows, chunk_rows), :], sem_in.at[s])

    def scatter_copy(c, s):
        dst = pl.multiple_of(dst_ref[0, 0, c] * 8, chunk_rows)
        return pltpu.make_async_copy(ybuf.at[s, pl.ds(c * chunk_rows, chunk_rows), :],
                                     ys_ref.at[pl.ds(dst, chunk_rows), :], sem_out.at[s])

    def start_gather(idx_ref, s):
        for c in range(BLOCK_CHUNKS):
            gather_copy(idx_ref, c, s).start()

    def wait_gather(s):
        for c in range(BLOCK_CHUNKS):
            gather_copy(src_ref, c, s).wait()

    def wait_scatter(s):
        for c in range(BLOCK_CHUNKS):
            scatter_copy(c, s).wait()

    @pl.when(b == 0)
    def _():
        start_gather(src_ref, 0)

    @pl.when(b + 1 < nu)
    def _():
        start_gather(srcn_ref, 1 - slot)

    @pl.when((b >= 2) & (b - 2 < nu))
    def _():
        wait_scatter(slot)

    @pl.when((b == 0) | (e != prev))
    def _():
        wgu_bf[...] = wgu_ref[...].astype(bf16)
        wd_bf[...] = wd_ref[...].astype(bf16)

    @pl.when(b < nu)
    def _():
        wait_gather(slot)
        x = _from_token_tiles(xbuf.at[slot], SLOT_BLOCK).astype(bf16)
        for c0 in range(0, D_EXPERT, 512):
            gate = jnp.dot(x, wgu_bf[:, c0:c0 + 512], preferred_element_type=f32) + bgu_ref[:, c0:c0 + 512]
            up = (jnp.dot(x, wgu_bf[:, D_EXPERT + c0:D_EXPERT + c0 + 512], preferred_element_type=f32)
                  + bgu_ref[:, D_EXPERT + c0:D_EXPERT + c0 + 512])
            gate = jnp.minimum(gate, SWIGLU_LIMIT)
            up = jnp.clip(up, -SWIGLU_LIMIT, SWIGLU_LIMIT)
            act[:, c0:c0 + 512] = ((up + 1.0) * (gate * jax.nn.sigmoid(gate * SWIGLU_ALPHA))).astype(bf16)
        y = jnp.dot(act[...], wd_bf[...], preferred_element_type=f32) + bd_ref[...]
        _to_token_tiles(ybuf.at[slot], y)
        for c in range(BLOCK_CHUNKS):
            scatter_copy(c, slot).start()

    @pl.when(b == nb - 1)
    def _():
        @pl.when((b >= 1) & (b - 1 < nu))
        def _():
            wait_scatter(1 - slot)

        @pl.when(b < nu)
        def _():
            wait_scatter(slot)


def _experts(xs, chunk_src, chunk_dst, blk_expert, n_used, layer, w_gu, b_gu, w_down, b_down):
    nblk = chunk_src.shape[0]
    depth = w_gu.shape[0]
    wsel = lambda b, be, nu: (layer, be[b], 0, 0)
    cur = lambda b, be, nu: (b, 0, 0)
    nxt = lambda b, be, nu: (jnp.minimum(b + 1, nblk - 1), 0, 0)
    idx_spec = lambda im: pl.BlockSpec((1, 1, BLOCK_CHUNKS), im, memory_space=pltpu.SMEM)
    grid_spec = pltpu.PrefetchScalarGridSpec(
        num_scalar_prefetch=2,
        grid=(nblk,),
        in_specs=[idx_spec(cur), idx_spec(nxt), idx_spec(cur),
                  pl.BlockSpec(memory_space=pl.ANY),
                  pl.BlockSpec((None, None, D_MODEL, 2 * D_EXPERT), wsel),
                  pl.BlockSpec((None, None, 1, 2 * D_EXPERT), wsel),
                  pl.BlockSpec((None, None, D_EXPERT, D_MODEL), wsel),
                  pl.BlockSpec((None, None, 1, D_MODEL), wsel)],
        out_specs=pl.BlockSpec(memory_space=pl.ANY),
        scratch_shapes=[pltpu.VMEM((2, SLOT_BLOCK * 8, 128), f32),
                        pltpu.VMEM((2, SLOT_BLOCK * 8, 128), f32),
                        pltpu.VMEM((D_MODEL, 2 * D_EXPERT), bf16),
                        pltpu.VMEM((D_EXPERT, D_MODEL), bf16),
                        pltpu.VMEM((SLOT_BLOCK, D_EXPERT), bf16),
                        pltpu.SemaphoreType.DMA((2,)),
                        pltpu.SemaphoreType.DMA((2,))],
    )
    return pl.pallas_call(
        _expert_kernel,
        grid_spec=grid_spec,
        out_shape=jax.ShapeDtypeStruct((xs.shape[0] + 2 * SLOT_BLOCK * 8, 128), f32),
        compiler_params=_cparams(("arbitrary",)),
        name="experts",
    )(blk_expert, n_used, chunk_src, chunk_src, chunk_dst, xs, w_gu,
      b_gu.reshape(depth, N_EXPERTS, 1, 2 * D_EXPERT), w_down,
      b_down.reshape(depth, N_EXPERTS, 1, D_MODEL))


def _combine_kernel(first_b, seq_ref, lp_ref, gt_ref, x_ref, ys_ref, ada_ref, ng_ref, *refs):
    ff = refs[-1]
    def body(t, carry):
        acc = gt_ref[t * TOP_K] * ys_ref[pl.ds(pl.multiple_of(lp_ref[t * TOP_K], 8), 8), :]
        for k in range(1, TOP_K):
            acc = acc + (gt_ref[t * TOP_K + k]
                         * ys_ref[pl.ds(pl.multiple_of(lp_ref[t * TOP_K + k], 8), 8), :])
        ff[pl.ds(pl.multiple_of(t * 8, 8), 8), :] = acc
        return carry

    lax.fori_loop(0, SORT_TILE, body, 0, unroll=8)
    i = pl.program_id(0)
    s = seq_ref[i]
    gt2 = ada_ref[pl.ds(s, 1), 5 * D_MODEL:6 * D_MODEL]
    out = x_ref[...] + gt2 * _rms(_from_token_tiles(ff, SORT_TILE), ng_ref[3:4, :])
    if first_b is None:
        refs[0][...] = out
    else:
        @pl.when(i < first_b)
        def _():
            refs[0][...] = out

        @pl.when(i >= first_b)
        def _():
            refs[1][...] = out


def _combine(x, ys, lrow, gflat, ada_l, ng, seq_lens, split_rows=None):
    n = x.shape[0]
    if split_rows is None:
        first_b = None
        out_specs = pl.BlockSpec((SORT_TILE, D_MODEL), lambda i, sq: (i, 0))
        out_shape = jax.ShapeDtypeStruct((n, D_MODEL), f32)
    else:
        first_b = split_rows // SORT_TILE
        out_specs = [pl.BlockSpec((SORT_TILE, D_MODEL), lambda i, sq: (jnp.minimum(i, first_b - 1), 0)),
                     pl.BlockSpec((SORT_TILE, D_MODEL), lambda i, sq: (jnp.maximum(i - first_b, 0), 0))]
        out_shape = [jax.ShapeDtypeStruct((split_rows, D_MODEL), f32),
                     jax.ShapeDtypeStruct((n - split_rows, D_MODEL), f32)]
    seq, _, _ = _tile_meta(seq_lens, SORT_TILE)
    row = lambda i, sq: (i, 0)
    flat = lambda i, sq: (i,)
    const = lambda i, sq: (0, 0)
    grid_spec = pltpu.PrefetchScalarGridSpec(
        num_scalar_prefetch=1,
        grid=(n // SORT_TILE,),
        in_specs=[pl.BlockSpec((TOP_K * SORT_TILE,), flat, memory_space=pltpu.SMEM),
                  pl.BlockSpec((TOP_K * SORT_TILE,), flat, memory_space=pltpu.SMEM),
                  pl.BlockSpec((SORT_TILE, D_MODEL), row),
                  pl.BlockSpec((LOCAL_ROWS * 8, 128), row),
                  pl.BlockSpec((SEQ_PAD, N_MOD * D_MODEL), const),
                  pl.BlockSpec((4, D_MODEL), const)],
        out_specs=out_specs,
        scratch_shapes=[pltpu.VMEM((SORT_TILE * 8, 128), f32)],
    )
    return pl.pallas_call(
        functools.partial(_combine_kernel, first_b),
        grid_spec=grid_spec,
        out_shape=out_shape,
        compiler_params=_cparams(("arbitrary",)),
        name="combine",
    )(seq, lrow, gflat, x, ys, ada_l, ng)


def _chunk_plan(tile_counts):
    nt = tile_counts.shape[0]
    nch = (tile_counts + CHUNK - 1) // CHUNK
    rows = nch * CHUNK
    loff = jnp.cumsum(rows, axis=1) - rows
    cend = jnp.cumsum(nch, axis=0)
    cstart = cend - nch
    total = cend[-1]
    nblk_e = (total + BLOCK_CHUNKS - 1) // BLOCK_CHUNKS
    bend = jnp.cumsum(nblk_e)
    bstart = bend - nblk_e
    n_used = bend[-1].reshape(1)
    max_chunks = nt * (TOP_K * SORT_TILE // CHUNK + N_EXPERTS)
    nblk = max_chunks // BLOCK_CHUNKS + N_EXPERTS
    b = jnp.arange(nblk, dtype=i32)
    blk_expert = jnp.minimum(jnp.sum((b[:, None] >= bend[None, :]).astype(i32), axis=1), N_EXPERTS - 1)
    onehot = (blk_expert[:, None] == jnp.arange(N_EXPERTS, dtype=i32)).astype(f32)

    def of_block(per_expert):
        table = per_expert.astype(f32).reshape(-1, N_EXPERTS).T
        return jnp.dot(onehot, table, precision=lax.Precision.HIGHEST).astype(i32)

    cend_b, cstart_b, loff_b = of_block(cend), of_block(cstart), of_block(loff)
    total_b, bstart_b = of_block(total)[:, 0], of_block(bstart)[:, 0]
    q = (b - bstart_b)[:, None] * BLOCK_CHUNKS + jnp.arange(BLOCK_CHUNKS, dtype=i32)[None, :]
    valid = (q < total_b[:, None]) & (b < n_used[0])[:, None]
    tile = jnp.minimum(jnp.sum((cend_b[:, None, :] <= q[:, :, None]).astype(i32), axis=2), nt - 1)
    pick = tile[:, :, None] == jnp.arange(nt, dtype=i32)
    cs = jnp.sum(jnp.where(pick, cstart_b[:, None, :], 0), axis=2)
    lo = jnp.sum(jnp.where(pick, loff_b[:, None, :], 0), axis=2)
    row = tile * LOCAL_ROWS + lo + (q - cs) * CHUNK
    spare = (nt * LOCAL_ROWS + (b % 2)[:, None] * SLOT_BLOCK
             + jnp.arange(BLOCK_CHUNKS, dtype=i32)[None, :] * CHUNK)
    shape = (nblk, 1, BLOCK_CHUNKS)
    chunk_src = jnp.where(valid, row, 0).astype(i32).reshape(shape)
    chunk_dst = jnp.where(valid, row, spare).astype(i32).reshape(shape)
    return chunk_src, chunk_dst, blk_expert.astype(i32), n_used.astype(i32)


def _trunk(x, c_all, seq_lens, split_rows, ada_w, ada_b, norm_g, w_in, pool_w, pool_scale, rpb, conv_w,
           w_branch, w_out, router_w, router_b, expert_w_gu, expert_b_gu, expert_w_down, expert_b_down):
    depth = ada_w.shape[0]
    ada = _ada(c_all, ada_w, ada_b)
    for l in range(depth):
        ada_l = ada[l]
        ng = norm_g[l]
        zs = _inproj(x, ada_l, ng[0:1], w_in[l].astype(bf16), seq_lens)
        rw = jnp.pad(router_w[l], ((0, 0), (0, 128 - N_EXPERTS)))
        rw_hi = rw.astype(bf16)
        rw_lo = (rw - rw_hi.astype(f32)).astype(bf16)
        x, h2, logits_t = _mixer(
            x, zs, ada_l, ng, pool_w[l].astype(bf16), pool_scale[l].reshape(1, POOL_WIDTH),
            _bias_table(rpb[l]), conv_w[l], w_branch[l].astype(bf16), w_out[l].astype(bf16),
            rw_hi, rw_lo,
            jnp.pad(router_b[l], (0, 128 - N_EXPERTS)).reshape(1, 128), seq_lens)
        lpos, gates, tile_counts = _router(logits_t)
        chunk_src, chunk_dst, blk_expert, n_used = _chunk_plan(tile_counts[:, :, 0])
        lrow = (lpos * 8).T.reshape(-1)
        xs = _local_sort(h2, lrow)
        ys = _experts(xs, chunk_src, chunk_dst, blk_expert, n_used, l, expert_w_gu, expert_b_gu,
                      expert_w_down, expert_b_down)
        split = split_rows if l == depth - 1 else None
        x = _combine(x, ys, lrow, gates.T.reshape(-1), ada_l, ng, seq_lens, split)
    return x


def kernel(x_prompt, x_sample, c_prompt, c_sample, ada_w, ada_b, norm_g, w_in, pool_w, pool_scale, rpb,
           conv_w, w_branch, w_out, router_w, router_b, expert_w_gu, expert_b_gu, expert_w_down,
           expert_b_down):
    bp, lp, d = x_prompt.shape
    bs, ls, _ = x_sample.shape
    assert d == D_MODEL and bp + bs <= SEQ_PAD
    assert lp % IN_TILE == 0 and ls % IN_TILE == 0 and lp >= 2 * TILE and ls >= 2 * TILE
    x = jnp.concatenate([x_prompt.reshape(bp * lp, d), x_sample.reshape(bs * ls, d)], axis=0)
    c_all = jnp.concatenate([c_prompt, c_sample, jnp.zeros((SEQ_PAD - bp - bs, d), f32)], axis=0)
    seq_lens = ((bp, lp), (bs, ls))
    yp, ys = _trunk(x, c_all, seq_lens, bp * lp, ada_w, ada_b, norm_g, w_in, pool_w, pool_scale, rpb, conv_w,
                    w_branch, w_out, router_w, router_b, expert_w_gu, expert_b_gu, expert_w_down,
                    expert_b_down)
    return (yp.reshape(bp, lp, d), ys.reshape(bs, ls, d))
```

```python
import functools

import numpy as np
import jax
import jax.numpy as jnp
from jax import lax
from jax.experimental import pallas as pl
from jax.experimental.pallas import tpu as pltpu

f32 = jnp.float32
bf16 = jnp.bfloat16
i32 = jnp.int32

D_MODEL = 1024
GRID_W = 64
POOL_WINDOWS = (2, 4, 8, 16)
POOL_GROUP = 128
POOL_WIDTH = 512
N_HEADS = 8
HEAD_DIM = 64
N_PAIRS = N_HEADS // 2
ATTN_WIDTH = 512
WIN_ROWS = 8
WIN_COLS = 16
CONV_WIDTH = 512
N_BRANCH = 3
IN_COLS = POOL_WIDTH + 3 * ATTN_WIDTH + 3 * CONV_WIDTH + N_BRANCH * D_MODEL
N_EXPERTS = 32
TOP_K = 4
D_EXPERT = D_MODEL
SWIGLU_ALPHA = 1.702
SWIGLU_LIMIT = 7.0
RMS_EPS = 1e-6
N_MOD = 6
NEG_BIG = -1e30

SEQ_PAD = 16
ROWS_PER_TILE = 4
TILE = ROWS_PER_TILE * GRID_W
HALO = 8
IN_TILE = 512
SORT_TILE = 512
CHUNK = 8
SLOT_BLOCK = 512
BLOCK_CHUNKS = SLOT_BLOCK // CHUNK
LOCAL_ROWS = TOP_K * SORT_TILE + N_EXPERTS * CHUNK
VMEM_LIMIT = 56 * 1024 * 1024


def _cparams(sem):
    return pltpu.CompilerParams(dimension_semantics=sem, vmem_limit_bytes=VMEM_LIMIT)


def _tile_meta(seq_lens, tile):
    seq, pos, cnt = [], [], []
    sid = 0
    for nseq, length in seq_lens:
        nt = length // tile
        for _ in range(nseq):
            for p in range(nt):
                seq.append(sid)
                pos.append(p)
                cnt.append(nt)
            sid += 1
    return (jnp.asarray(np.array(seq, np.int32)), jnp.asarray(np.array(pos, np.int32)),
            jnp.asarray(np.array(cnt, np.int32)))


def _rms(x, g):
    ms = jnp.mean(x * x, axis=-1, keepdims=True)
    return x * lax.rsqrt(ms + RMS_EPS) * g


def _ada_kernel(c_ref, w_ref, b_ref, o_ref):
    c = c_ref[...]
    s = c * jax.nn.sigmoid(c)
    o_ref[...] = jnp.dot(s, w_ref[...], preferred_element_type=f32,
                         precision=lax.Precision.HIGHEST) + b_ref[...]


def _ada(c_all, ada_w, ada_b):
    depth = ada_w.shape[0]
    return pl.pallas_call(
        _ada_kernel,
        grid=(depth, N_MOD),
        in_specs=[pl.BlockSpec((SEQ_PAD, D_MODEL), lambda l, j: (0, 0)),
                  pl.BlockSpec((None, D_MODEL, D_MODEL), lambda l, j: (l, 0, j)),
                  pl.BlockSpec((None, 1, D_MODEL), lambda l, j: (l, 0, j))],
        out_specs=pl.BlockSpec((None, SEQ_PAD, D_MODEL), lambda l, j: (l, 0, j)),
        out_shape=jax.ShapeDtypeStruct((depth, SEQ_PAD, N_MOD * D_MODEL), f32),
        compiler_params=_cparams(("arbitrary", "arbitrary")),
        name="ada",
    )(c_all, ada_w, ada_b.reshape(depth, 1, N_MOD * D_MODEL))


_IN_SPLITS = (POOL_WIDTH, ATTN_WIDTH, ATTN_WIDTH, ATTN_WIDTH, CONV_WIDTH, CONV_WIDTH, CONV_WIDTH,
              N_BRANCH * D_MODEL)


def _pick_rows(x_refs, i, first_b):
    if len(x_refs) == 1:
        return x_refs[0][...]
    return jnp.where(i < first_b, x_refs[0][...], x_refs[1][...])


def _row_specs(x_parts, tile):
    if len(x_parts) == 1:
        return [pl.BlockSpec((tile, D_MODEL), lambda i, *_: (i, 0))], None
    first_b = x_parts[0].shape[0] // tile
    return [pl.BlockSpec((tile, D_MODEL), lambda i, *_: (jnp.minimum(i, first_b - 1), 0)),
            pl.BlockSpec((tile, D_MODEL), lambda i, *_: (jnp.maximum(i - first_b, 0), 0))], first_b


def _inproj_kernel(n_x, first_b, seq_ref, *refs):
    x_refs, (ada_ref, g_ref, w_ref), out_refs = refs[:n_x], refs[n_x:n_x + 3], refs[n_x + 3:]
    i = pl.program_id(0)
    s = seq_ref[i]
    mod = ada_ref[pl.ds(s, 1), :]
    x = _pick_rows(x_refs, i, first_b)
    h = _rms(x, g_ref[...]) * (1.0 + mod[:, D_MODEL:2 * D_MODEL]) + mod[:, 0:D_MODEL]
    h = h.astype(bf16)
    col = 0
    for o_ref in out_refs:
        width = o_ref.shape[1]
        for c0 in range(0, width, 512):
            o_ref[:, c0:c0 + 512] = jnp.dot(h, w_ref[:, col + c0:col + c0 + 512],
                                            preferred_element_type=f32).astype(bf16)
        col += width


def _inproj(x_parts, ada_l, g, w_in_bf, seq_lens):
    n = sum(p.shape[0] for p in x_parts)
    seq, _, _ = _tile_meta(seq_lens, IN_TILE)
    row = lambda i, s: (i, 0)
    const = lambda i, s: (0, 0)
    x_specs, first_b = _row_specs(x_parts, IN_TILE)
    grid_spec = pltpu.PrefetchScalarGridSpec(
        num_scalar_prefetch=1,
        grid=(n // IN_TILE,),
        in_specs=x_specs + [
                  pl.BlockSpec((SEQ_PAD, N_MOD * D_MODEL), const),
                  pl.BlockSpec((1, D_MODEL), const),
                  pl.BlockSpec((D_MODEL, IN_COLS), const, pipeline_mode=pl.Buffered(1))],
        out_specs=[pl.BlockSpec((IN_TILE, w), row) for w in _IN_SPLITS],
    )
    return pl.pallas_call(
        functools.partial(_inproj_kernel, len(x_parts), first_b),
        grid_spec=grid_spec,
        out_shape=[jax.ShapeDtypeStruct((n, w), bf16) for w in _IN_SPLITS],
        compiler_params=_cparams(("arbitrary",)),
        name="inproj",
    )(seq, *x_parts, ada_l, g, w_in_bf)


def _mixer_kernel(n_x, first_b, seq_ref, pos_ref, nt_ref, *refs):
    _mixer_body(_pick_rows(refs[:n_x], pl.program_id(0), first_b), seq_ref, pos_ref, nt_ref, *refs[n_x:])


def _mixer_body(x, seq_ref, pos_ref, nt_ref,
                  u_ref, up_ref, un_ref, q_ref, kp_ref, kc_ref, kn_ref, vp_ref, vc_ref, vn_ref,
                  b_ref, c_ref, cp_ref, cn_ref, xc_ref, xcp_ref, xcn_ref, gl_ref,
                  ada_ref, ng_ref, poolw_ref, pscale_ref, bias_ref, convw_ref, wb_ref, wout_ref,
                  rwh_ref, rwl_ref, rb_ref,
                  xo_ref, h2_ref, lg_ref,
                  uext, vext, kbuf, vbuf, pb_ref, sc_ref, e_ref):
    i = pl.program_id(0)
    s = seq_ref[i]
    pos = pos_ref[i]
    nt = nt_ref[i]
    first = pos == 0
    last = pos == nt - 1
    seq_len = nt * TILE

    uext[HALO:HALO + TILE, :] = u_ref[...].astype(f32)
    uext[0:HALO, :] = jnp.where(first, 0.0, up_ref[...].astype(f32)[GRID_W - HALO:GRID_W, :])
    uext[HALO + TILE:2 * HALO + TILE, :] = jnp.where(last, 0.0, un_ref[...].astype(f32)[0:HALO, :])

    vext[HALO:HALO + TILE, :] = c_ref[...].astype(f32) * xc_ref[...].astype(f32)
    vprev = cp_ref[...].astype(f32) * xcp_ref[...].astype(f32)
    vnext = cn_ref[...].astype(f32) * xcn_ref[...].astype(f32)
    vext[0:HALO, :] = jnp.where(first, 0.0, vprev[GRID_W - HALO:GRID_W, :])
    vext[HALO + TILE:2 * HALO + TILE, :] = jnp.where(last, 0.0, vnext[0:HALO, :])

    kbuf[0:TILE, :] = kp_ref[...]
    kbuf[TILE:2 * TILE, :] = kc_ref[...]
    kbuf[2 * TILE:3 * TILE, :] = kn_ref[...]
    vbuf[0:TILE, :] = vp_ref[...]
    vbuf[TILE:2 * TILE, :] = vc_ref[...]
    vbuf[2 * TILE:3 * TILE, :] = vn_ref[...]

    width = 2 * HEAD_DIM
    lane = lax.broadcasted_iota(i32, (1, width), 1)
    low = lane < HEAD_DIM
    rows = nt * ROWS_PER_TILE
    qscale = jnp.asarray(HEAD_DIM ** -0.5, bf16)
    eye = (lax.broadcasted_iota(i32, (width, width), 0)
           == lax.broadcasted_iota(i32, (width, width), 1)).astype(bf16)
    ones = jnp.ones((WIN_ROWS * GRID_W, width), bf16)

    def window(j):
        r = pos * ROWS_PER_TILE + j
        rs = jnp.clip(r - WIN_ROWS // 2, 0, rows - WIN_ROWS)
        koff = pl.multiple_of((rs - pos * ROWS_PER_TILE + ROWS_PER_TILE) * GRID_W, GRID_W)
        return r - rs, koff

    items = [(j, p) for j in range(ROWS_PER_TILE) for p in range(N_PAIRS)]

    def scores(n):
        j, p = items[n]
        si, koff = window(j)
        cols = slice(p * width, (p + 1) * width)
        qp = q_ref[j * GRID_W:(j + 1) * GRID_W, cols] * qscale
        zero = jnp.zeros_like(qp)
        qs = jnp.concatenate([jnp.where(low, qp, zero), jnp.where(low, zero, qp)], axis=0)
        lhs = jnp.concatenate([qs, eye], axis=1)
        rhs = jnp.concatenate([kbuf[pl.ds(koff, WIN_ROWS * GRID_W), cols], bias_ref[si, p]], axis=1)
        sc_ref[n] = lax.dot_general(lhs, rhs, (((1,), (1,)), ((), ())), preferred_element_type=f32)

    def exponentials(n):
        sc = sc_ref[n]
        e_ref[n] = jnp.exp(sc - jnp.max(sc, axis=-1, keepdims=True)).astype(bf16)

    def weighted_values(n):
        j, p = items[n]
        _, koff = window(j)
        cols = slice(p * width, (p + 1) * width)
        rhs = jnp.concatenate([vbuf[pl.ds(koff, WIN_ROWS * GRID_W), cols], ones], axis=1)
        oe = jnp.dot(e_ref[n], rhs, preferred_element_type=f32)
        o = oe[:, 0:width] / oe[:, width:2 * width]
        pb_ref[j * GRID_W:(j + 1) * GRID_W, cols] = jnp.where(low, o[0:GRID_W], o[GRID_W:2 * GRID_W]).astype(bf16)

    for stage in (scores, exponentials, weighted_values):
        for n in range(len(items)):
            stage(n)

    t = pos * TILE + lax.broadcasted_iota(i32, (TILE, 1), 0)
    pa_parts = []
    for g, w in enumerate(POOL_WINDOWS):
        cols = slice(g * POOL_GROUP, (g + 1) * POOL_GROUP)
        half = w // 2
        acc = uext[HALO - half:HALO - half + TILE, cols]
        for d in range(-half + 1, half):
            acc = acc + uext[HALO + d:HALO + d + TILE, cols]
        cnt = (jnp.minimum(t + half, seq_len) - jnp.maximum(t - half, 0)).astype(f32)
        m = acc / cnt - uext[HALO:HALO + TILE, cols]
        z = jnp.dot(m.astype(bf16), poolw_ref[g], preferred_element_type=f32)
        pa_parts.append(z * pscale_ref[:, cols])
    pa = jnp.concatenate(pa_parts, axis=1).astype(bf16)

    y = (convw_ref[0:1, :] * vext[HALO - 1:HALO - 1 + TILE, :]
         + convw_ref[1:2, :] * vext[HALO:HALO + TILE, :]
         + convw_ref[2:3, :] * vext[HALO + 1:HALO + 1 + TILE, :])
    pc = (b_ref[...].astype(f32) * y).astype(bf16)

    branches = (pa, pb_ref[...], pc)
    merged = None
    for n in range(N_BRANCH):
        proj = jnp.dot(branches[n], wb_ref[n], preferred_element_type=f32)
        gate = jax.nn.sigmoid(gl_ref[:, n * D_MODEL:(n + 1) * D_MODEL].astype(f32))
        merged = gate * proj if merged is None else merged + gate * proj
    mix = jnp.dot(merged.astype(bf16), wout_ref[...], preferred_element_type=f32)

    mod = ada_ref[pl.ds(s, 1), :]
    gt1 = mod[:, 2 * D_MODEL:3 * D_MODEL]
    sh2 = mod[:, 3 * D_MODEL:4 * D_MODEL]
    sc2 = mod[:, 4 * D_MODEL:5 * D_MODEL]
    xn = x + gt1 * _rms(mix, ng_ref[1:2, :])
    xo_ref[...] = xn
    h2 = _rms(xn, ng_ref[2:3, :]) * (1.0 + sc2) + sh2
    _to_token_tiles(h2_ref, h2)
    h_hi = h2.astype(bf16)
    h_lo = (h2 - h_hi.astype(f32)).astype(bf16)
    logits = (jnp.dot(h_hi, rwh_ref[...], preferred_element_type=f32)
              + jnp.dot(h_lo, rwh_ref[...], preferred_element_type=f32)
              + jnp.dot(h_hi, rwl_ref[...], preferred_element_type=f32)
              + rb_ref[...])
    lg_ref[...] = logits.T[0:N_EXPERTS, :]


def _mixer(x_parts, zs, ada_l, ng, poolw_bf, pscale, bias_tab, convw, wb_bf, wout_bf, rw_hi, rw_lo, rb,
           seq_lens):
    n = sum(p.shape[0] for p in x_parts)
    zu, zq, zk, zv, zb, zc, zx, zg = zs
    x_specs, first_b = _row_specs(x_parts, TILE)
    seq, pos, cnt = _tile_meta(seq_lens, TILE)
    per = TILE // GRID_W

    cur = lambda i, sq, ps, ct: (i, 0)
    prev_t = lambda i, sq, ps, ct: (jnp.where(ps[i] > 0, i - 1, i), 0)
    next_t = lambda i, sq, ps, ct: (jnp.where(ps[i] < ct[i] - 1, i + 1, i), 0)
    prev_h = lambda i, sq, ps, ct: (jnp.where(ps[i] > 0, i * per - 1, i * per), 0)
    next_h = lambda i, sq, ps, ct: (jnp.where(ps[i] < ct[i] - 1, (i + 1) * per, i * per), 0)
    c2 = lambda i, sq, ps, ct: (0, 0)
    c3 = lambda i, sq, ps, ct: (0, 0, 0)
    c4 = lambda i, sq, ps, ct: (0, 0, 0, 0)

    tile512 = lambda im: pl.BlockSpec((TILE, 512), im)
    halo512 = lambda im: pl.BlockSpec((GRID_W, 512), im)
    once = pl.Buffered(1)
    in_specs = x_specs + [
        tile512(cur), halo512(prev_h), halo512(next_h),
        tile512(cur),
        tile512(prev_t), tile512(cur), tile512(next_t),
        tile512(prev_t), tile512(cur), tile512(next_t),
        tile512(cur),
        tile512(cur), halo512(prev_h), halo512(next_h),
        tile512(cur), halo512(prev_h), halo512(next_h),
        pl.BlockSpec((TILE, N_BRANCH * D_MODEL), cur),
        pl.BlockSpec((SEQ_PAD, N_MOD * D_MODEL), c2),
        pl.BlockSpec((4, D_MODEL), c2),
        pl.BlockSpec((4, POOL_GROUP, POOL_GROUP), c3),
        pl.BlockSpec((1, POOL_WIDTH), c2),
        pl.BlockSpec((WIN_ROWS, N_PAIRS, WIN_ROWS * GRID_W, 2 * GRID_W), c4, pipeline_mode=once),
        pl.BlockSpec((3, CONV_WIDTH), c2),
        pl.BlockSpec((N_BRANCH, POOL_WIDTH, D_MODEL), c3, pipeline_mode=once),
        pl.BlockSpec((D_MODEL, D_MODEL), c2, pipeline_mode=once),
        pl.BlockSpec((D_MODEL, 128), c2),
        pl.BlockSpec((D_MODEL, 128), c2),
        pl.BlockSpec((1, 128), c2),
    ]
    out_specs = [
        pl.BlockSpec((TILE, D_MODEL), cur),
        pl.BlockSpec((TILE * 8, 128), cur),
        pl.BlockSpec((N_EXPERTS, TILE), lambda i, sq, ps, ct: (0, i)),
    ]
    grid_spec = pltpu.PrefetchScalarGridSpec(
        num_scalar_prefetch=3,
        grid=(n // TILE,),
        in_specs=in_specs,
        out_specs=out_specs,
        scratch_shapes=[pltpu.VMEM((TILE + 2 * HALO, POOL_WIDTH), f32),
                        pltpu.VMEM((TILE + 2 * HALO, CONV_WIDTH), f32),
                        pltpu.VMEM((3 * TILE, ATTN_WIDTH), bf16),
                        pltpu.VMEM((3 * TILE, ATTN_WIDTH), bf16),
                        pltpu.VMEM((TILE, ATTN_WIDTH), bf16),
                        pltpu.VMEM((ROWS_PER_TILE * N_PAIRS, 2 * GRID_W, WIN_ROWS * GRID_W), f32),
                        pltpu.VMEM((ROWS_PER_TILE * N_PAIRS, 2 * GRID_W, WIN_ROWS * GRID_W), bf16)],
    )
    return pl.pallas_call(
        functools.partial(_mixer_kernel, len(x_parts), first_b),
        grid_spec=grid_spec,
        out_shape=[jax.ShapeDtypeStruct((n, D_MODEL), f32),
                   jax.ShapeDtypeStruct((n * 8, 128), f32),
                   jax.ShapeDtypeStruct((N_EXPERTS, n), f32)],
        compiler_params=_cparams(("arbitrary",)),
        name="mixer",
    )(seq, pos, cnt,
      *x_parts, zu, zu, zu, zq, zk, zk, zk, zv, zv, zv, zb, zc, zc, zc, zx, zx, zx, zg,
      ada_l, ng, poolw_bf, pscale, bias_tab, convw, wb_bf, wout_bf, rw_hi, rw_lo, rb)


def _bias_table(rpb_l):
    c = np.arange(GRID_W)[:, None]
    kc = np.arange(GRID_W)[None, :]
    cs = np.clip(c - WIN_COLS // 2, 0, GRID_W - WIN_COLS)
    valid = (kc >= cs) & (kc < cs + WIN_COLS)
    didx = np.clip(kc - c + WIN_COLS - 1, 0, 2 * WIN_COLS - 2)
    select = (didx[:, :, None] == np.arange(2 * WIN_COLS - 1)).astype(np.float32)
    dense = jnp.einsum('hrd,ckd->hrck', rpb_l, jnp.asarray(select), precision=lax.Precision.HIGHEST)
    dense = jnp.where(valid[None, None], dense, NEG_BIG)
    slabs = [dense[:, WIN_ROWS - 1 - s:2 * WIN_ROWS - 1 - s] for s in range(WIN_ROWS)]
    tab = jnp.stack(slabs, axis=0)
    tab = tab.transpose(0, 1, 2, 4, 3)
    tab = tab.reshape(WIN_ROWS, N_PAIRS, 2, WIN_ROWS * GRID_W, GRID_W).transpose(0, 1, 3, 2, 4)
    return tab.reshape(WIN_ROWS, N_PAIRS, WIN_ROWS * GRID_W, 2 * GRID_W).astype(bf16)


def _router_kernel(lg_ref, lp_ref, gt_ref, tc_ref):
    l = lg_ref[...]
    eio = lax.broadcasted_iota(i32, l.shape, 0)
    vals, hots = [], []
    for k in range(TOP_K):
        m = jnp.max(l, axis=0, keepdims=True)
        idx = jnp.min(jnp.where(l == m, eio, N_EXPERTS), axis=0, keepdims=True)
        hot = eio == idx
        l = jnp.where(hot, -jnp.inf, l)
        vals.append(m)
        hots.append(hot)
    ex = [jnp.exp(v - vals[0]) for v in vals]
    den = ex[0] + ex[1] + ex[2] + ex[3]
    for k in range(TOP_K):
        gt_ref[k:k + 1, :] = ex[k] / den

    hot_all = jnp.zeros(l.shape, f32)
    for hot in hots:
        hot_all = hot_all + hot.astype(f32)
    rr = lax.broadcasted_iota(i32, (SORT_TILE, SORT_TILE), 0)
    cc = lax.broadcasted_iota(i32, (SORT_TILE, SORT_TILE), 1)
    tri = (rr <= cc).astype(bf16)
    pref = jnp.dot(hot_all.astype(bf16), tri, preferred_element_type=f32)
    cnt = jnp.broadcast_to(pref[:, SORT_TILE - 1:SORT_TILE], (N_EXPERTS, 128))
    padded = jnp.floor((cnt + (CHUNK - 1)) * (1.0 / CHUNK)) * CHUNK
    er = lax.broadcasted_iota(i32, (N_EXPERTS, N_EXPERTS), 0)
    ec = lax.broadcasted_iota(i32, (N_EXPERTS, N_EXPERTS), 1)
    before = (ec < er).astype(f32)
    start = jnp.dot(before, padded, preferred_element_type=f32, precision=lax.Precision.HIGHEST)
    tot = start[:, 0:1] + pref - 1.0
    for k in range(TOP_K):
        lp_ref[k:k + 1, :] = jnp.sum(jnp.where(hots[k], tot, 0.0), axis=0, keepdims=True).astype(i32)
    tc_ref[...] = cnt.astype(i32)


def _router(logits_t):
    n = logits_t.shape[1]
    blk = lambda i: (0, i)
    return pl.pallas_call(
        _router_kernel,
        grid=(n // SORT_TILE,),
        in_specs=[pl.BlockSpec((N_EXPERTS, SORT_TILE), blk)],
        out_specs=[pl.BlockSpec((TOP_K, SORT_TILE), blk),
                   pl.BlockSpec((TOP_K, SORT_TILE), blk),
                   pl.BlockSpec((None, N_EXPERTS, 128), lambda i: (i, 0, 0))],
        out_shape=[jax.ShapeDtypeStruct((TOP_K, n), i32),
                   jax.ShapeDtypeStruct((TOP_K, n), f32),
                   jax.ShapeDtypeStruct((n // SORT_TILE, N_EXPERTS, 128), i32)],
        compiler_params=_cparams(("arbitrary",)),
        name="router",
    )(logits_t)


def _to_token_tiles(dst_ref, val):
    rows = val.shape[0]
    for s in range(8):
        dst_ref[pl.ds(s, rows, stride=8), :] = val[:, s * 128:(s + 1) * 128]


def _from_token_tiles(src_ref, rows):
    return jnp.concatenate([src_ref[pl.ds(s, rows, stride=8), :] for s in range(8)], axis=1)


def _sort_kernel(lp_ref, h_ref, o_ref):
    o_ref[...] = jnp.zeros_like(o_ref)

    def body(t, carry):
        v = h_ref[pl.ds(pl.multiple_of(t * 8, 8), 8), :]
        for k in range(TOP_K):
            o_ref[pl.ds(pl.multiple_of(lp_ref[t * TOP_K + k], 8), 8), :] = v
        return carry

    lax.fori_loop(0, SORT_TILE, body, 0, unroll=8)


def _local_sort(h2, lrow):
    nt = h2.shape[0] // (SORT_TILE * 8)
    return pl.pallas_call(
        _sort_kernel,
        grid=(nt,),
        in_specs=[pl.BlockSpec((TOP_K * SORT_TILE,), lambda i: (i,), memory_space=pltpu.SMEM),
                  pl.BlockSpec((SORT_TILE * 8, 128), lambda i: (i, 0))],
        out_specs=pl.BlockSpec((LOCAL_ROWS * 8, 128), lambda i: (i, 0)),
        out_shape=jax.ShapeDtypeStruct((nt * LOCAL_ROWS * 8, 128), f32),
        compiler_params=_cparams(("arbitrary",)),
        name="local_sort",
    )(lrow, h2)


def _expert_kernel(be_ref, nu_ref, src_ref, srcn_ref, dst_ref, xs_ref, wgu_ref, bgu_ref, wd_ref, bd_ref,
                   ys_ref, xbuf, ybuf, wgu_bf, wd_bf, act, sem_in, sem_out):
    b = pl.program_id(0)
    nb = pl.num_programs(0)
    nu = nu_ref[0]
    slot = b % 2
    e = be_ref[b]
    prev = be_ref[jnp.maximum(b - 1, 0)]
    chunk_rows = CHUNK * 8

    def gather_copy(idx_ref, c, s):
        src = pl.multiple_of(idx_ref[0, 0, c] * 8, chunk_rows)
        return pltpu.make_async_copy(xs_ref.at[pl.ds(src, chunk_rows), :],
                                     xbuf.at[s, pl.ds(c * chunk_rows, chunk_rows), :], sem_in.at[s])

    def scatter_copy(c, s):
        dst = pl.multiple_of(dst_ref[0, 0, c] * 8, chunk_rows)
        return pltpu.make_async_copy(ybuf.at[s, pl.ds(c * chunk_rows, chunk_rows), :],
                                     ys_ref.at[pl.ds(dst, chunk_rows), :], sem_out.at[s])

    def start_gather(idx_ref, s):
        for c in range(BLOCK_CHUNKS):
            gather_copy(idx_ref, c, s).start()

    def wait_gather(s):
        for c in range(BLOCK_CHUNKS):
            gather_copy(src_ref, c, s).wait()

    def wait_scatter(s):
        for c in range(BLOCK_CHUNKS):
            scatter_copy(c, s).wait()

    @pl.when(b == 0)
    def _():
        start_gather(src_ref, 0)

    @pl.when(b + 1 < nu)
    def _():
        start_gather(srcn_ref, 1 - slot)

    @pl.when((b >= 2) & (b - 2 < nu))
    def _():
        wait_scatter(slot)

    @pl.when((b == 0) | (e != prev))
    def _():
        wgu_bf[...] = wgu_ref[...].astype(bf16)
        wd_bf[...] = wd_ref[...].astype(bf16)

    @pl.when(b < nu)
    def _():
        wait_gather(slot)
        x = _from_token_tiles(xbuf.at[slot], SLOT_BLOCK).astype(bf16)
        for c0 in range(0, D_EXPERT, 512):
            gate = jnp.dot(x, wgu_bf[:, c0:c0 + 512], preferred_element_type=f32) + bgu_ref[:, c0:c0 + 512]
            up = (jnp.dot(x, wgu_bf[:, D_EXPERT + c0:D_EXPERT + c0 + 512], preferred_element_type=f32)
                  + bgu_ref[:, D_EXPERT + c0:D_EXPERT + c0 + 512])
            gate = jnp.minimum(gate, SWIGLU_LIMIT)
            up = jnp.clip(up, -SWIGLU_LIMIT, SWIGLU_LIMIT)
            act[:, c0:c0 + 512] = ((up + 1.0) * (gate * jax.nn.sigmoid(gate * SWIGLU_ALPHA))).astype(bf16)
        y = jnp.dot(act[...], wd_bf[...], preferred_element_type=f32) + bd_ref[...]
        _to_token_tiles(ybuf.at[slot], y)
        for c in range(BLOCK_CHUNKS):
            scatter_copy(c, slot).start()

    @pl.when(b == nb - 1)
    def _():
        @pl.when((b >= 1) & (b - 1 < nu))
        def _():
            wait_scatter(1 - slot)

        @pl.when(b < nu)
        def _():
            wait_scatter(slot)


def _experts(xs, chunk_src, chunk_dst, blk_expert, n_used, layer, w_gu, b_gu, w_down, b_down):
    nblk = chunk_src.shape[0]
    depth = w_gu.shape[0]
    wsel = lambda b, be, nu: (layer, be[b], 0, 0)
    cur = lambda b, be, nu: (b, 0, 0)
    nxt = lambda b, be, nu: (jnp.minimum(b + 1, nblk - 1), 0, 0)
    idx_spec = lambda im: pl.BlockSpec((1, 1, BLOCK_CHUNKS), im, memory_space=pltpu.SMEM)
    grid_spec = pltpu.PrefetchScalarGridSpec(
        num_scalar_prefetch=2,
        grid=(nblk,),
        in_specs=[idx_spec(cur), idx_spec(nxt), idx_spec(cur),
                  pl.BlockSpec(memory_space=pl.ANY),
                  pl.BlockSpec((None, None, D_MODEL, 2 * D_EXPERT), wsel),
                  pl.BlockSpec((None, None, 1, 2 * D_EXPERT), wsel),
                  pl.BlockSpec((None, None, D_EXPERT, D_MODEL), wsel),
                  pl.BlockSpec((None, None, 1, D_MODEL), wsel)],
        out_specs=pl.BlockSpec(memory_space=pl.ANY),
        scratch_shapes=[pltpu.VMEM((2, SLOT_BLOCK * 8, 128), f32),
                        pltpu.VMEM((2, SLOT_BLOCK * 8, 128), f32),
                        pltpu.VMEM((D_MODEL, 2 * D_EXPERT), bf16),
                        pltpu.VMEM((D_EXPERT, D_MODEL), bf16),
                        pltpu.VMEM((SLOT_BLOCK, D_EXPERT), bf16),
                        pltpu.SemaphoreType.DMA((2,)),
                        pltpu.SemaphoreType.DMA((2,))],
    )
    return pl.pallas_call(
        _expert_kernel,
        grid_spec=grid_spec,
        out_shape=jax.ShapeDtypeStruct((xs.shape[0] + 2 * SLOT_BLOCK * 8, 128), f32),
        compiler_params=_cparams(("arbitrary",)),
        name="experts",
    )(blk_expert, n_used, chunk_src, chunk_src, chunk_dst, xs, w_gu,
      b_gu.reshape(depth, N_EXPERTS, 1, 2 * D_EXPERT), w_down,
      b_down.reshape(depth, N_EXPERTS, 1, D_MODEL))


def _combine_kernel(first_b, seq_ref, lp_ref, gt_ref, x_ref, ys_ref, ada_ref, ng_ref, *refs):
    ff = refs[-1]
    def body(t, carry):
        acc = gt_ref[t * TOP_K] * ys_ref[pl.ds(pl.multiple_of(lp_ref[t * TOP_K], 8), 8), :]
        for k in range(1, TOP_K):
            acc = acc + (gt_ref[t * TOP_K + k]
                         * ys_ref[pl.ds(pl.multiple_of(lp_ref[t * TOP_K + k], 8), 8), :])
        ff[pl.ds(pl.multiple_of(t * 8, 8), 8), :] = acc
        return carry

    lax.fori_loop(0, SORT_TILE, body, 0, unroll=8)
    i = pl.program_id(0)
    s = seq_ref[i]
    gt2 = ada_ref[pl.ds(s, 1), 5 * D_MODEL:6 * D_MODEL]
    out = x_ref[...] + gt2 * _rms(_from_token_tiles(ff, SORT_TILE), ng_ref[3:4, :])
    if first_b is None:
        refs[0][...] = out
    else:
        @pl.when(i < first_b)
        def _():
            refs[0][...] = out

        @pl.when(i >= first_b)
        def _():
            refs[1][...] = out


def _combine(x, ys, lrow, gflat, ada_l, ng, seq_lens, split_rows=None):
    n = x.shape[0]
    if split_rows is None:
        first_b = None
        out_specs = pl.BlockSpec((SORT_TILE, D_MODEL), lambda i, sq: (i, 0))
        out_shape = jax.ShapeDtypeStruct((n, D_MODEL), f32)
    else:
        first_b = split_rows // SORT_TILE
        out_specs = [pl.BlockSpec((SORT_TILE, D_MODEL), lambda i, sq: (jnp.minimum(i, first_b - 1), 0)),
                     pl.BlockSpec((SORT_TILE, D_MODEL), lambda i, sq: (jnp.maximum(i - first_b, 0), 0))]
        out_shape = [jax.ShapeDtypeStruct((split_rows, D_MODEL), f32),
                     jax.ShapeDtypeStruct((n - split_rows, D_MODEL), f32)]
    seq, _, _ = _tile_meta(seq_lens, SORT_TILE)
    row = lambda i, sq: (i, 0)
    flat = lambda i, sq: (i,)
    const = lambda i, sq: (0, 0)
    grid_spec = pltpu.PrefetchScalarGridSpec(
        num_scalar_prefetch=1,
        grid=(n // SORT_TILE,),
        in_specs=[pl.BlockSpec((TOP_K * SORT_TILE,), flat, memory_space=pltpu.SMEM),
                  pl.BlockSpec((TOP_K * SORT_TILE,), flat, memory_space=pltpu.SMEM),
                  pl.BlockSpec((SORT_TILE, D_MODEL), row),
                  pl.BlockSpec((LOCAL_ROWS * 8, 128), row),
                  pl.BlockSpec((SEQ_PAD, N_MOD * D_MODEL), const),
                  pl.BlockSpec((4, D_MODEL), const)],
        out_specs=out_specs,
        scratch_shapes=[pltpu.VMEM((SORT_TILE * 8, 128), f32)],
    )
    return pl.pallas_call(
        functools.partial(_combine_kernel, first_b),
        grid_spec=grid_spec,
        out_shape=out_shape,
        compiler_params=_cparams(("arbitrary",)),
        name="combine",
    )(seq, lrow, gflat, x, ys, ada_l, ng)


def _chunk_plan(tile_counts):
    nt = tile_counts.shape[0]
    nch = (tile_counts + CHUNK - 1) // CHUNK
    rows = nch * CHUNK
    loff = jnp.cumsum(rows, axis=1) - rows
    cend = jnp.cumsum(nch, axis=0)
    cstart = cend - nch
    total = cend[-1]
    nblk_e = (total + BLOCK_CHUNKS - 1) // BLOCK_CHUNKS
    bend = jnp.cumsum(nblk_e)
    bstart = bend - nblk_e
    n_used = bend[-1].reshape(1)
    max_chunks = nt * (TOP_K * SORT_TILE // CHUNK + N_EXPERTS)
    nblk = max_chunks // BLOCK_CHUNKS + N_EXPERTS
    b = jnp.arange(nblk, dtype=i32)
    blk_expert = jnp.minimum(jnp.sum((b[:, None] >= bend[None, :]).astype(i32), axis=1), N_EXPERTS - 1)
    onehot = (blk_expert[:, None] == jnp.arange(N_EXPERTS, dtype=i32)).astype(f32)

    def of_block(per_expert):
        table = per_expert.astype(f32).reshape(-1, N_EXPERTS).T
        return jnp.dot(onehot, table, precision=lax.Precision.HIGHEST).astype(i32)

    cend_b, cstart_b, loff_b = of_block(cend), of_block(cstart), of_block(loff)
    total_b, bstart_b = of_block(total)[:, 0], of_block(bstart)[:, 0]
    q = (b - bstart_b)[:, None] * BLOCK_CHUNKS + jnp.arange(BLOCK_CHUNKS, dtype=i32)[None, :]
    valid = (q < total_b[:, None]) & (b < n_used[0])[:, None]
    tile = jnp.minimum(jnp.sum((cend_b[:, None, :] <= q[:, :, None]).astype(i32), axis=2), nt - 1)
    pick = tile[:, :, None] == jnp.arange(nt, dtype=i32)
    cs = jnp.sum(jnp.where(pick, cstart_b[:, None, :], 0), axis=2)
    lo = jnp.sum(jnp.where(pick, loff_b[:, None, :], 0), axis=2)
    row = tile * LOCAL_ROWS + lo + (q - cs) * CHUNK
    spare = (nt * LOCAL_ROWS + (b % 2)[:, None] * SLOT_BLOCK
             + jnp.arange(BLOCK_CHUNKS, dtype=i32)[None, :] * CHUNK)
    shape = (nblk, 1, BLOCK_CHUNKS)
    chunk_src = jnp.where(valid, row, 0).astype(i32).reshape(shape)
    chunk_dst = jnp.where(valid, row, spare).astype(i32).reshape(shape)
    return chunk_src, chunk_dst, blk_expert.astype(i32), n_used.astype(i32)


def _trunk(x_parts, c_all, seq_lens, ada_w, ada_b, norm_g, w_in, pool_w, pool_scale, rpb, conv_w,
           w_branch, w_out, router_w, router_b, expert_w_gu, expert_b_gu, expert_w_down, expert_b_down):
    depth = ada_w.shape[0]
    split_rows = x_parts[0].shape[0]
    ada = _ada(c_all, ada_w, ada_b)
    x = tuple(x_parts)
    for l in range(depth):
        ada_l = ada[l]
        ng = norm_g[l]
        zs = _inproj(x, ada_l, ng[0:1], w_in[l].astype(bf16), seq_lens)
        rw = jnp.pad(router_w[l], ((0, 0), (0, 128 - N_EXPERTS)))
        rw_hi = rw.astype(bf16)
        rw_lo = (rw - rw_hi.astype(f32)).astype(bf16)
        x_mid, h2, logits_t = _mixer(
            x, zs, ada_l, ng, pool_w[l].astype(bf16), pool_scale[l].reshape(1, POOL_WIDTH),
            _bias_table(rpb[l]), conv_w[l], w_branch[l].astype(bf16), w_out[l].astype(bf16),
            rw_hi, rw_lo,
            jnp.pad(router_b[l], (0, 128 - N_EXPERTS)).reshape(1, 128), seq_lens)
        lpos, gates, tile_counts = _router(logits_t)
        chunk_src, chunk_dst, blk_expert, n_used = _chunk_plan(tile_counts[:, :, 0])
        lrow = (lpos * 8).T.reshape(-1)
        xs = _local_sort(h2, lrow)
        ys = _experts(xs, chunk_src, chunk_dst, blk_expert, n_used, l, expert_w_gu, expert_b_gu,
                      expert_w_down, expert_b_down)
        split = split_rows if l == depth - 1 else None
        out = _combine(x_mid, ys, lrow, gates.T.reshape(-1), ada_l, ng, seq_lens, split)
        x = (out,) if split is None else tuple(out)
    return x


def kernel(x_prompt, x_sample, c_prompt, c_sample, ada_w, ada_b, norm_g, w_in, pool_w, pool_scale, rpb,
           conv_w, w_branch, w_out, router_w, router_b, expert_w_gu, expert_b_gu, expert_w_down,
           expert_b_down):
    bp, lp, d = x_prompt.shape
    bs, ls, _ = x_sample.shape
    assert d == D_MODEL and bp + bs <= SEQ_PAD
    assert lp % IN_TILE == 0 and ls % IN_TILE == 0 and lp >= 2 * TILE and ls >= 2 * TILE
    x_parts = (x_prompt.reshape(bp * lp, d), x_sample.reshape(bs * ls, d))
    c_all = jnp.concatenate([c_prompt, c_sample, jnp.zeros((SEQ_PAD - bp - bs, d), f32)], axis=0)
    seq_lens = ((bp, lp), (bs, ls))
    yp, ys = _trunk(x_parts, c_all, seq_lens, ada_w, ada_b, norm_g, w_in, pool_w, pool_scale, rpb, conv_w,
                    w_branch, w_out, router_w, router_b, expert_w_gu, expert_b_gu, expert_w_down,
                    expert_b_down)
    return (yp.reshape(bp, lp, d), ys.reshape(bs, ls, d))
```

```python
import functools

import numpy as np
import jax
import jax.numpy as jnp
from jax import lax
from jax.experimental import pallas as pl
from jax.experimental.pallas import tpu as pltpu

f32 = jnp.float32
bf16 = jnp.bfloat16
i32 = jnp.int32

D_MODEL = 1024
GRID_W = 64
POOL_WINDOWS = (2, 4, 8, 16)
POOL_GROUP = 128
POOL_WIDTH = 512
N_HEADS = 8
HEAD_DIM = 64
N_PAIRS = N_HEADS // 2
ATTN_WIDTH = 512
WIN_ROWS = 8
WIN_COLS = 16
CONV_WIDTH = 512
N_BRANCH = 3
IN_COLS = POOL_WIDTH + 3 * ATTN_WIDTH + 3 * CONV_WIDTH + N_BRANCH * D_MODEL
N_EXPERTS = 32
TOP_K = 4
D_EXPERT = D_MODEL
SWIGLU_ALPHA = 1.702
SWIGLU_LIMIT = 7.0
RMS_EPS = 1e-6
N_MOD = 6
NEG_BIG = -1e30

SEQ_PAD = 16
ROWS_PER_TILE = 4
TILE = ROWS_PER_TILE * GRID_W
HALO = 8
IN_TILE = 512
SORT_TILE = 512
CHUNK = 8
SLOT_BLOCK = 512
BLOCK_CHUNKS = SLOT_BLOCK // CHUNK
LOCAL_ROWS = TOP_K * SORT_TILE + N_EXPERTS * CHUNK
VMEM_LIMIT = 56 * 1024 * 1024


def _cparams(sem):
    return pltpu.CompilerParams(dimension_semantics=sem, vmem_limit_bytes=VMEM_LIMIT)


def _tile_meta(seq_lens, tile):
    seq, pos, cnt = [], [], []
    sid = 0
    for nseq, length in seq_lens:
        nt = length // tile
        for _ in range(nseq):
            for p in range(nt):
                seq.append(sid)
                pos.append(p)
                cnt.append(nt)
            sid += 1
    return (jnp.asarray(np.array(seq, np.int32)), jnp.asarray(np.array(pos, np.int32)),
            jnp.asarray(np.array(cnt, np.int32)))


def _rms(x, g):
    ms = jnp.mean(x * x, axis=-1, keepdims=True)
    return x * lax.rsqrt(ms + RMS_EPS) * g


def _ada_kernel(c_ref, w_ref, b_ref, o_ref):
    c = c_ref[...]
    s = c * jax.nn.sigmoid(c)
    o_ref[...] = jnp.dot(s, w_ref[...], preferred_element_type=f32,
                         precision=lax.Precision.HIGHEST) + b_ref[...]


def _ada(c_all, ada_w, ada_b):
    depth = ada_w.shape[0]
    return pl.pallas_call(
        _ada_kernel,
        grid=(depth, N_MOD),
        in_specs=[pl.BlockSpec((SEQ_PAD, D_MODEL), lambda l, j: (0, 0)),
                  pl.BlockSpec((None, D_MODEL, D_MODEL), lambda l, j: (l, 0, j)),
                  pl.BlockSpec((None, 1, D_MODEL), lambda l, j: (l, 0, j))],
        out_specs=pl.BlockSpec((None, SEQ_PAD, D_MODEL), lambda l, j: (l, 0, j)),
        out_shape=jax.ShapeDtypeStruct((depth, SEQ_PAD, N_MOD * D_MODEL), f32),
        compiler_params=_cparams(("arbitrary", "arbitrary")),
        name="ada",
    )(c_all, ada_w, ada_b.reshape(depth, 1, N_MOD * D_MODEL))


_IN_SPLITS = (POOL_WIDTH, ATTN_WIDTH, ATTN_WIDTH, ATTN_WIDTH, CONV_WIDTH, CONV_WIDTH, CONV_WIDTH,
              N_BRANCH * D_MODEL)


def _pick_rows(x_refs, i, first_b):
    if len(x_refs) == 1:
        return x_refs[0][...]
    return jnp.where(i < first_b, x_refs[0][...], x_refs[1][...])


def _row_specs(x_parts, tile):
    if len(x_parts) == 1:
        return [pl.BlockSpec((tile, D_MODEL), lambda i, *_: (i, 0))], None
    first_b = x_parts[0].shape[0] // tile
    return [pl.BlockSpec((tile, D_MODEL), lambda i, *_: (jnp.minimum(i, first_b - 1), 0)),
            pl.BlockSpec((tile, D_MODEL), lambda i, *_: (jnp.maximum(i - first_b, 0), 0))], first_b


def _inproj_kernel(n_x, first_b, seq_ref, *refs):
    x_refs, (ada_ref, g_ref, w_ref), out_refs = refs[:n_x], refs[n_x:n_x + 3], refs[n_x + 3:]
    i = pl.program_id(0)
    s = seq_ref[i]
    mod = ada_ref[pl.ds(s, 1), :]
    x = _pick_rows(x_refs, i, first_b)
    h = _rms(x, g_ref[...]) * (1.0 + mod[:, D_MODEL:2 * D_MODEL]) + mod[:, 0:D_MODEL]
    h = h.astype(bf16)
    col = 0
    for o_ref in out_refs:
        width = o_ref.shape[1]
        for c0 in range(0, width, 512):
            o_ref[:, c0:c0 + 512] = jnp.dot(h, w_ref[:, col + c0:col + c0 + 512],
                                            preferred_element_type=f32).astype(bf16)
        col += width


def _inproj(x_parts, ada_l, g, w_in_bf, seq_lens):
    n = sum(p.shape[0] for p in x_parts)
    seq, _, _ = _tile_meta(seq_lens, IN_TILE)
    row = lambda i, s: (i, 0)
    const = lambda i, s: (0, 0)
    x_specs, first_b = _row_specs(x_parts, IN_TILE)
    grid_spec = pltpu.PrefetchScalarGridSpec(
        num_scalar_prefetch=1,
        grid=(n // IN_TILE,),
        in_specs=x_specs + [
                  pl.BlockSpec((SEQ_PAD, N_MOD * D_MODEL), const),
                  pl.BlockSpec((1, D_MODEL), const),
                  pl.BlockSpec((D_MODEL, IN_COLS), const, pipeline_mode=pl.Buffered(1))],
        out_specs=[pl.BlockSpec((IN_TILE, w), row) for w in _IN_SPLITS],
    )
    return pl.pallas_call(
        functools.partial(_inproj_kernel, len(x_parts), first_b),
        grid_spec=grid_spec,
        out_shape=[jax.ShapeDtypeStruct((n, w), bf16) for w in _IN_SPLITS],
        compiler_params=_cparams(("arbitrary",)),
        name="inproj",
    )(seq, *x_parts, ada_l, g, w_in_bf)


def _mixer_kernel(n_x, first_b, seq_ref, pos_ref, nt_ref, *refs):
    _mixer_body(_pick_rows(refs[:n_x], pl.program_id(0), first_b), seq_ref, pos_ref, nt_ref, *refs[n_x:])


def _mixer_body(x, seq_ref, pos_ref, nt_ref,
                  u_ref, up_ref, un_ref, q_ref, kp_ref, kc_ref, kn_ref, vp_ref, vc_ref, vn_ref,
                  b_ref, c_ref, cp_ref, cn_ref, xc_ref, xcp_ref, xcn_ref, gl_ref,
                  ada_ref, ng_ref, poolw_ref, pscale_ref, bias_ref, convw_ref, wb_ref, wout_ref,
                  rwh_ref, rwl_ref, rb_ref,
                  xo_ref, h2_ref, lg_ref,
                  uext, vext, kbuf, vbuf, pb_ref, sc_ref, e_ref):
    i = pl.program_id(0)
    s = seq_ref[i]
    pos = pos_ref[i]
    nt = nt_ref[i]
    first = pos == 0
    last = pos == nt - 1
    seq_len = nt * TILE

    uext[HALO:HALO + TILE, :] = u_ref[...].astype(f32)
    uext[0:HALO, :] = jnp.where(first, 0.0, up_ref[...].astype(f32)[GRID_W - HALO:GRID_W, :])
    uext[HALO + TILE:2 * HALO + TILE, :] = jnp.where(last, 0.0, un_ref[...].astype(f32)[0:HALO, :])

    vext[HALO:HALO + TILE, :] = c_ref[...].astype(f32) * xc_ref[...].astype(f32)
    vprev = cp_ref[...].astype(f32) * xcp_ref[...].astype(f32)
    vnext = cn_ref[...].astype(f32) * xcn_ref[...].astype(f32)
    vext[0:HALO, :] = jnp.where(first, 0.0, vprev[GRID_W - HALO:GRID_W, :])
    vext[HALO + TILE:2 * HALO + TILE, :] = jnp.where(last, 0.0, vnext[0:HALO, :])

    kbuf[0:TILE, :] = kp_ref[...]
    kbuf[TILE:2 * TILE, :] = kc_ref[...]
    kbuf[2 * TILE:3 * TILE, :] = kn_ref[...]
    vbuf[0:TILE, :] = vp_ref[...]
    vbuf[TILE:2 * TILE, :] = vc_ref[...]
    vbuf[2 * TILE:3 * TILE, :] = vn_ref[...]

    width = 2 * HEAD_DIM
    lane = lax.broadcasted_iota(i32, (1, width), 1)
    low = lane < HEAD_DIM
    rows = nt * ROWS_PER_TILE
    qscale = jnp.asarray(HEAD_DIM ** -0.5, bf16)
    eye = (lax.broadcasted_iota(i32, (width, width), 0)
           == lax.broadcasted_iota(i32, (width, width), 1)).astype(bf16)
    ones = jnp.ones((WIN_ROWS * GRID_W, width), bf16)

    def window(j):
        r = pos * ROWS_PER_TILE + j
        rs = jnp.clip(r - WIN_ROWS // 2, 0, rows - WIN_ROWS)
        koff = pl.multiple_of((rs - pos * ROWS_PER_TILE + ROWS_PER_TILE) * GRID_W, GRID_W)
        return r - rs, koff

    items = [(j, p) for j in range(ROWS_PER_TILE) for p in range(N_PAIRS)]

    def scores(n):
        j, p = items[n]
        si, koff = window(j)
        cols = slice(p * width, (p + 1) * width)
        qp = q_ref[j * GRID_W:(j + 1) * GRID_W, cols] * qscale
        zero = jnp.zeros_like(qp)
        qs = jnp.concatenate([jnp.where(low, qp, zero), jnp.where(low, zero, qp)], axis=0)
        lhs = jnp.concatenate([qs, eye], axis=1)
        rhs = jnp.concatenate([kbuf[pl.ds(koff, WIN_ROWS * GRID_W), cols], bias_ref[si, p]], axis=1)
        sc_ref[n] = lax.dot_general(lhs, rhs, (((1,), (1,)), ((), ())), preferred_element_type=f32)

    def exponentials(n):
        sc = sc_ref[n]
        e_ref[n] = jnp.exp(sc - jnp.max(sc, axis=-1, keepdims=True)).astype(bf16)

    def weighted_values(n):
        j, p = items[n]
        _, koff = window(j)
        cols = slice(p * width, (p + 1) * width)
        rhs = jnp.concatenate([vbuf[pl.ds(koff, WIN_ROWS * GRID_W), cols], ones], axis=1)
        oe = jnp.dot(e_ref[n], rhs, preferred_element_type=f32)
        o = oe[:, 0:width] / oe[:, width:2 * width]
        pb_ref[j * GRID_W:(j + 1) * GRID_W, cols] = jnp.where(low, o[0:GRID_W], o[GRID_W:2 * GRID_W]).astype(bf16)

    for stage in (scores, exponentials, weighted_values):
        for n in range(len(items)):
            stage(n)

    t = pos * TILE + lax.broadcasted_iota(i32, (TILE, 1), 0)
    pa_parts = []
    for g, w in enumerate(POOL_WINDOWS):
        cols = slice(g * POOL_GROUP, (g + 1) * POOL_GROUP)
        half = w // 2
        acc = uext[HALO - half:HALO - half + TILE, cols]
        for d in range(-half + 1, half):
            acc = acc + uext[HALO + d:HALO + d + TILE, cols]
        cnt = (jnp.minimum(t + half, seq_len) - jnp.maximum(t - half, 0)).astype(f32)
        m = acc / cnt - uext[HALO:HALO + TILE, cols]
        z = jnp.dot(m.astype(bf16), poolw_ref[g], preferred_element_type=f32)
        pa_parts.append(z * pscale_ref[:, cols])
    pa = jnp.concatenate(pa_parts, axis=1).astype(bf16)

    y = (convw_ref[0:1, :] * vext[HALO - 1:HALO - 1 + TILE, :]
         + convw_ref[1:2, :] * vext[HALO:HALO + TILE, :]
         + convw_ref[2:3, :] * vext[HALO + 1:HALO + 1 + TILE, :])
    pc = (b_ref[...].astype(f32) * y).astype(bf16)

    branches = (pa, pb_ref[...], pc)
    merged = None
    for n in range(N_BRANCH):
        proj = jnp.dot(branches[n], wb_ref[n], preferred_element_type=f32)
        gate = jax.nn.sigmoid(gl_ref[:, n * D_MODEL:(n + 1) * D_MODEL].astype(f32))
        merged = gate * proj if merged is None else merged + gate * proj
    mix = jnp.dot(merged.astype(bf16), wout_ref[...], preferred_element_type=f32)

    mod = ada_ref[pl.ds(s, 1), :]
    gt1 = mod[:, 2 * D_MODEL:3 * D_MODEL]
    sh2 = mod[:, 3 * D_MODEL:4 * D_MODEL]
    sc2 = mod[:, 4 * D_MODEL:5 * D_MODEL]
    xn = x + gt1 * _rms(mix, ng_ref[1:2, :])
    xo_ref[...] = xn
    h2 = _rms(xn, ng_ref[2:3, :]) * (1.0 + sc2) + sh2
    _to_token_tiles(h2_ref, h2)
    h_hi = h2.astype(bf16)
    h_lo = (h2 - h_hi.astype(f32)).astype(bf16)
    logits = (jnp.dot(h_hi, rwh_ref[...], preferred_element_type=f32)
              + jnp.dot(h_lo, rwh_ref[...], preferred_element_type=f32)
              + jnp.dot(h_hi, rwl_ref[...], preferred_element_type=f32)
              + rb_ref[...])
    lg_ref[...] = logits.T[0:N_EXPERTS, :]


def _mixer(x_parts, zs, ada_l, ng, poolw_bf, pscale, bias_tab, convw, wb_bf, wout_bf, rw_hi, rw_lo, rb,
           seq_lens):
    n = sum(p.shape[0] for p in x_parts)
    zu, zq, zk, zv, zb, zc, zx, zg = zs
    x_specs, first_b = _row_specs(x_parts, TILE)
    seq, pos, cnt = _tile_meta(seq_lens, TILE)
    per = TILE // GRID_W

    cur = lambda i, sq, ps, ct: (i, 0)
    prev_t = lambda i, sq, ps, ct: (jnp.where(ps[i] > 0, i - 1, i), 0)
    next_t = lambda i, sq, ps, ct: (jnp.where(ps[i] < ct[i] - 1, i + 1, i), 0)
    prev_h = lambda i, sq, ps, ct: (jnp.where(ps[i] > 0, i * per - 1, i * per), 0)
    next_h = lambda i, sq, ps, ct: (jnp.where(ps[i] < ct[i] - 1, (i + 1) * per, i * per), 0)
    c2 = lambda i, sq, ps, ct: (0, 0)
    c3 = lambda i, sq, ps, ct: (0, 0, 0)
    c4 = lambda i, sq, ps, ct: (0, 0, 0, 0)

    tile512 = lambda im: pl.BlockSpec((TILE, 512), im)
    halo512 = lambda im: pl.BlockSpec((GRID_W, 512), im)
    once = pl.Buffered(1)
    in_specs = x_specs + [
        tile512(cur), halo512(prev_h), halo512(next_h),
        tile512(cur),
        tile512(prev_t), tile512(cur), tile512(next_t),
        tile512(prev_t), tile512(cur), tile512(next_t),
        tile512(cur),
        tile512(cur), halo512(prev_h), halo512(next_h),
        tile512(cur), halo512(prev_h), halo512(next_h),
        pl.BlockSpec((TILE, N_BRANCH * D_MODEL), cur),
        pl.BlockSpec((SEQ_PAD, N_MOD * D_MODEL), c2),
        pl.BlockSpec((4, D_MODEL), c2),
        pl.BlockSpec((4, POOL_GROUP, POOL_GROUP), c3),
        pl.BlockSpec((1, POOL_WIDTH), c2),
        pl.BlockSpec((WIN_ROWS, N_PAIRS, WIN_ROWS * GRID_W, 2 * GRID_W), c4, pipeline_mode=once),
        pl.BlockSpec((3, CONV_WIDTH), c2),
        pl.BlockSpec((N_BRANCH, POOL_WIDTH, D_MODEL), c3, pipeline_mode=once),
        pl.BlockSpec((D_MODEL, D_MODEL), c2, pipeline_mode=once),
        pl.BlockSpec((D_MODEL, 128), c2),
        pl.BlockSpec((D_MODEL, 128), c2),
        pl.BlockSpec((1, 128), c2),
    ]
    out_specs = [
        pl.BlockSpec((TILE, D_MODEL), cur),
        pl.BlockSpec((TILE * 8, 128), cur),
        pl.BlockSpec((N_EXPERTS, TILE), lambda i, sq, ps, ct: (0, i)),
    ]
    grid_spec = pltpu.PrefetchScalarGridSpec(
        num_scalar_prefetch=3,
        grid=(n // TILE,),
        in_specs=in_specs,
        out_specs=out_specs,
        scratch_shapes=[pltpu.VMEM((TILE + 2 * HALO, POOL_WIDTH), f32),
                        pltpu.VMEM((TILE + 2 * HALO, CONV_WIDTH), f32),
                        pltpu.VMEM((3 * TILE, ATTN_WIDTH), bf16),
                        pltpu.VMEM((3 * TILE, ATTN_WIDTH), bf16),
                        pltpu.VMEM((TILE, ATTN_WIDTH), bf16),
                        pltpu.VMEM((ROWS_PER_TILE * N_PAIRS, 2 * GRID_W, WIN_ROWS * GRID_W), f32),
                        pltpu.VMEM((ROWS_PER_TILE * N_PAIRS, 2 * GRID_W, WIN_ROWS * GRID_W), bf16)],
    )
    return pl.pallas_call(
        functools.partial(_mixer_kernel, len(x_parts), first_b),
        grid_spec=grid_spec,
        out_shape=[jax.ShapeDtypeStruct((n, D_MODEL), f32),
                   jax.ShapeDtypeStruct((n * 8, 128), f32),
                   jax.ShapeDtypeStruct((N_EXPERTS, n), f32)],
        compiler_params=_cparams(("arbitrary",)),
        name="mixer",
    )(seq, pos, cnt,
      *x_parts, zu, zu, zu, zq, zk, zk, zk, zv, zv, zv, zb, zc, zc, zc, zx, zx, zx, zg,
      ada_l, ng, poolw_bf, pscale, bias_tab, convw, wb_bf, wout_bf, rw_hi, rw_lo, rb)


def _bias_table(rpb_l):
    c = np.arange(GRID_W)[:, None]
    kc = np.arange(GRID_W)[None, :]
    cs = np.clip(c - WIN_COLS // 2, 0, GRID_W - WIN_COLS)
    valid = (kc >= cs) & (kc < cs + WIN_COLS)
    didx = np.clip(kc - c + WIN_COLS - 1, 0, 2 * WIN_COLS - 2)
    select = (didx[:, :, None] == np.arange(2 * WIN_COLS - 1)).astype(np.float32)
    dense = jnp.einsum('hrd,ckd->hrck', rpb_l, jnp.asarray(select), precision=lax.Precision.HIGHEST)
    dense = jnp.where(valid[None, None], dense, NEG_BIG)
    slabs = [dense[:, WIN_ROWS - 1 - s:2 * WIN_ROWS - 1 - s] for s in range(WIN_ROWS)]
    tab = jnp.stack(slabs, axis=0)
    tab = tab.transpose(0, 1, 2, 4, 3)
    tab = tab.reshape(WIN_ROWS, N_PAIRS, 2, WIN_ROWS * GRID_W, GRID_W).transpose(0, 1, 3, 2, 4)
    return tab.reshape(WIN_ROWS, N_PAIRS, WIN_ROWS * GRID_W, 2 * GRID_W).astype(bf16)


def _router_kernel(lg_ref, lp_ref, gt_ref, tc_ref):
    l = lg_ref[...]
    eio = lax.broadcasted_iota(i32, l.shape, 0)
    vals, hots = [], []
    for k in range(TOP_K):
        m = jnp.max(l, axis=0, keepdims=True)
        idx = jnp.min(jnp.where(l == m, eio, N_EXPERTS), axis=0, keepdims=True)
        hot = eio == idx
        l = jnp.where(hot, -jnp.inf, l)
        vals.append(m)
        hots.append(hot)
    ex = [jnp.exp(v - vals[0]) for v in vals]
    den = ex[0] + ex[1] + ex[2] + ex[3]
    for k in range(TOP_K):
        gt_ref[k:k + 1, :] = ex[k] / den

    hot_all = jnp.zeros(l.shape, f32)
    for hot in hots:
        hot_all = hot_all + hot.astype(f32)
    rr = lax.broadcasted_iota(i32, (SORT_TILE, SORT_TILE), 0)
    cc = lax.broadcasted_iota(i32, (SORT_TILE, SORT_TILE), 1)
    tri = (rr <= cc).astype(bf16)
    pref = jnp.dot(hot_all.astype(bf16), tri, preferred_element_type=f32)
    cnt = jnp.broadcast_to(pref[:, SORT_TILE - 1:SORT_TILE], (N_EXPERTS, 128))
    padded = jnp.floor((cnt + (CHUNK - 1)) * (1.0 / CHUNK)) * CHUNK
    er = lax.broadcasted_iota(i32, (N_EXPERTS, N_EXPERTS), 0)
    ec = lax.broadcasted_iota(i32, (N_EXPERTS, N_EXPERTS), 1)
    before = (ec < er).astype(f32)
    start = jnp.dot(before, padded, preferred_element_type=f32, precision=lax.Precision.HIGHEST)
    tot = start[:, 0:1] + pref - 1.0
    for k in range(TOP_K):
        row = jnp.sum(jnp.where(hots[k], tot, 0.0), axis=0, keepdims=True).astype(i32)
        lp_ref[k:k + 1, :] = row * 8
    tc_ref[...] = cnt.astype(i32)


def _router(logits_t):
    n = logits_t.shape[1]
    blk = lambda i: (0, i)
    return pl.pallas_call(
        _router_kernel,
        grid=(n // SORT_TILE,),
        in_specs=[pl.BlockSpec((N_EXPERTS, SORT_TILE), blk)],
        out_specs=[pl.BlockSpec((None, TOP_K, SORT_TILE), lambda i: (i, 0, 0)),
                   pl.BlockSpec((None, TOP_K, SORT_TILE), lambda i: (i, 0, 0)),
                   pl.BlockSpec((None, N_EXPERTS, 128), lambda i: (i, 0, 0))],
        out_shape=[jax.ShapeDtypeStruct((n // SORT_TILE, TOP_K, SORT_TILE), i32),
                   jax.ShapeDtypeStruct((n // SORT_TILE, TOP_K, SORT_TILE), f32),
                   jax.ShapeDtypeStruct((n // SORT_TILE, N_EXPERTS, 128), i32)],
        compiler_params=_cparams(("arbitrary",)),
        name="router",
    )(logits_t)


def _to_token_tiles(dst_ref, val):
    rows = val.shape[0]
    for s in range(8):
        dst_ref[pl.ds(s, rows, stride=8), :] = val[:, s * 128:(s + 1) * 128]


def _from_token_tiles(src_ref, rows):
    return jnp.concatenate([src_ref[pl.ds(s, rows, stride=8), :] for s in range(8)], axis=1)


def _sort_kernel(lp_ref, h_ref, o_ref):
    o_ref[...] = jnp.zeros_like(o_ref)

    def body(t, carry):
        v = h_ref[pl.ds(pl.multiple_of(t * 8, 8), 8), :]
        for k in range(TOP_K):
            o_ref[pl.ds(pl.multiple_of(lp_ref[k * SORT_TILE + t], 8), 8), :] = v
        return carry

    lax.fori_loop(0, SORT_TILE, body, 0, unroll=8)


def _local_sort(h2, lrow):
    nt = h2.shape[0] // (SORT_TILE * 8)
    return pl.pallas_call(
        _sort_kernel,
        grid=(nt,),
        in_specs=[pl.BlockSpec((TOP_K * SORT_TILE,), lambda i: (i,), memory_space=pltpu.SMEM),
                  pl.BlockSpec((SORT_TILE * 8, 128), lambda i: (i, 0))],
        out_specs=pl.BlockSpec((LOCAL_ROWS * 8, 128), lambda i: (i, 0)),
        out_shape=jax.ShapeDtypeStruct((nt * LOCAL_ROWS * 8, 128), f32),
        compiler_params=_cparams(("arbitrary",)),
        name="local_sort",
    )(lrow, h2)


def _expert_kernel(be_ref, nu_ref, src_ref, srcn_ref, dst_ref, xs_ref, wgu_ref, bgu_ref, wd_ref, bd_ref,
                   ys_ref, xbuf, ybuf, wgu_bf, wd_bf, act, sem_in, sem_out):
    b = pl.program_id(0)
    nb = pl.num_programs(0)
    nu = nu_ref[0]
    slot = b % 2
    e = be_ref[b]
    prev = be_ref[jnp.maximum(b - 1, 0)]
    chunk_rows = CHUNK * 8

    def gather_copy(idx_ref, c, s):
        src = pl.multiple_of(idx_ref[0, 0, c] * 8, chunk_rows)
        return pltpu.make_async_copy(xs_ref.at[pl.ds(src, chunk_rows), :],
                                     xbuf.at[s, pl.ds(c * chunk_rows, chunk_rows), :], sem_in.at[s])

    def scatter_copy(c, s):
        dst = pl.multiple_of(dst_ref[0, 0, c] * 8, chunk_rows)
        return pltpu.make_async_copy(ybuf.at[s, pl.ds(c * chunk_rows, chunk_rows), :],
                                     ys_ref.at[pl.ds(dst, chunk_rows), :], sem_out.at[s])

    def start_gather(idx_ref, s):
        for c in range(BLOCK_CHUNKS):
            gather_copy(idx_ref, c, s).start()

    def wait_gather(s):
        for c in range(BLOCK_CHUNKS):
            gather_copy(src_ref, c, s).wait()

    def wait_scatter(s):
        for c in range(BLOCK_CHUNKS):
            scatter_copy(c, s).wait()

    @pl.when(b == 0)
    def _():
        start_gather(src_ref, 0)

    @pl.when(b + 1 < nu)
    def _():
        start_gather(srcn_ref, 1 - slot)

    @pl.when((b >= 2) & (b - 2 < nu))
    def _():
        wait_scatter(slot)

    @pl.when((b == 0) | (e != prev))
    def _():
        wgu_bf[...] = wgu_ref[...].astype(bf16)
        wd_bf[...] = wd_ref[...].astype(bf16)

    @pl.when(b < nu)
    def _():
        wait_gather(slot)
        x = _from_token_tiles(xbuf.at[slot], SLOT_BLOCK).astype(bf16)
        for c0 in range(0, D_EXPERT, 512):
            gate = jnp.dot(x, wgu_bf[:, c0:c0 + 512], preferred_element_type=f32) + bgu_ref[:, c0:c0 + 512]
            up = (jnp.dot(x, wgu_bf[:, D_EXPERT + c0:D_EXPERT + c0 + 512], preferred_element_type=f32)
                  + bgu_ref[:, D_EXPERT + c0:D_EXPERT + c0 + 512])
            gate = jnp.minimum(gate, SWIGLU_LIMIT)
            up = jnp.clip(up, -SWIGLU_LIMIT, SWIGLU_LIMIT)
            act[:, c0:c0 + 512] = ((up + 1.0) * (gate * jax.nn.sigmoid(gate * SWIGLU_ALPHA))).astype(bf16)
        y = jnp.dot(act[...], wd_bf[...], preferred_element_type=f32) + bd_ref[...]
        _to_token_tiles(ybuf.at[slot], y)
        for c in range(BLOCK_CHUNKS):
            scatter_copy(c, slot).start()

    @pl.when(b == nb - 1)
    def _():
        @pl.when((b >= 1) & (b - 1 < nu))
        def _():
            wait_scatter(1 - slot)

        @pl.when(b < nu)
        def _():
            wait_scatter(slot)


def _experts(xs, chunk_src, chunk_dst, blk_expert, n_used, layer, w_gu, b_gu, w_down, b_down):
    nblk = chunk_src.shape[0]
    depth = w_gu.shape[0]
    wsel = lambda b, be, nu: (layer, be[b], 0, 0)
    cur = lambda b, be, nu: (b, 0, 0)
    nxt = lambda b, be, nu: (jnp.minimum(b + 1, nblk - 1), 0, 0)
    idx_spec = lambda im: pl.BlockSpec((1, 1, BLOCK_CHUNKS), im, memory_space=pltpu.SMEM)
    grid_spec = pltpu.PrefetchScalarGridSpec(
        num_scalar_prefetch=2,
        grid=(nblk,),
        in_specs=[idx_spec(cur), idx_spec(nxt), idx_spec(cur),
                  pl.BlockSpec(memory_space=pl.ANY),
                  pl.BlockSpec((None, None, D_MODEL, 2 * D_EXPERT), wsel),
                  pl.BlockSpec((None, None, 1, 2 * D_EXPERT), wsel),
                  pl.BlockSpec((None, None, D_EXPERT, D_MODEL), wsel),
                  pl.BlockSpec((None, None, 1, D_MODEL), wsel)],
        out_specs=pl.BlockSpec(memory_space=pl.ANY),
        scratch_shapes=[pltpu.VMEM((2, SLOT_BLOCK * 8, 128), f32),
                        pltpu.VMEM((2, SLOT_BLOCK * 8, 128), f32),
                        pltpu.VMEM((D_MODEL, 2 * D_EXPERT), bf16),
                        pltpu.VMEM((D_EXPERT, D_MODEL), bf16),
                        pltpu.VMEM((SLOT_BLOCK, D_EXPERT), bf16),
                        pltpu.SemaphoreType.DMA((2,)),
                        pltpu.SemaphoreType.DMA((2,))],
    )
    return pl.pallas_call(
        _expert_kernel,
        grid_spec=grid_spec,
        out_shape=jax.ShapeDtypeStruct((xs.shape[0] + 2 * SLOT_BLOCK * 8, 128), f32),
        compiler_params=_cparams(("arbitrary",)),
        name="experts",
    )(blk_expert, n_used, chunk_src, chunk_src, chunk_dst, xs, w_gu,
      b_gu.reshape(depth, N_EXPERTS, 1, 2 * D_EXPERT), w_down,
      b_down.reshape(depth, N_EXPERTS, 1, D_MODEL))


def _combine_kernel(first_b, seq_ref, lp_ref, gt_ref, x_ref, ys_ref, ada_ref, ng_ref, *refs):
    ff = refs[-1]
    def body(t, carry):
        acc = gt_ref[t] * ys_ref[pl.ds(pl.multiple_of(lp_ref[t], 8), 8), :]
        for k in range(1, TOP_K):
            acc = acc + (gt_ref[k * SORT_TILE + t]
                         * ys_ref[pl.ds(pl.multiple_of(lp_ref[k * SORT_TILE + t], 8), 8), :])
        ff[pl.ds(pl.multiple_of(t * 8, 8), 8), :] = acc
        return carry

    lax.fori_loop(0, SORT_TILE, body, 0, unroll=8)
    i = pl.program_id(0)
    s = seq_ref[i]
    gt2 = ada_ref[pl.ds(s, 1), 5 * D_MODEL:6 * D_MODEL]
    out = x_ref[...] + gt2 * _rms(_from_token_tiles(ff, SORT_TILE), ng_ref[3:4, :])
    if first_b is None:
        refs[0][...] = out
    else:
        @pl.when(i < first_b)
        def _():
            refs[0][...] = out

        @pl.when(i >= first_b)
        def _():
            refs[1][...] = out


def _combine(x, ys, lrow, gflat, ada_l, ng, seq_lens, split_rows=None):
    n = x.shape[0]
    if split_rows is None:
        first_b = None
        out_specs = pl.BlockSpec((SORT_TILE, D_MODEL), lambda i, sq: (i, 0))
        out_shape = jax.ShapeDtypeStruct((n, D_MODEL), f32)
    else:
        first_b = split_rows // SORT_TILE
        out_specs = [pl.BlockSpec((SORT_TILE, D_MODEL), lambda i, sq: (jnp.minimum(i, first_b - 1), 0)),
                     pl.BlockSpec((SORT_TILE, D_MODEL), lambda i, sq: (jnp.maximum(i - first_b, 0), 0))]
        out_shape = [jax.ShapeDtypeStruct((split_rows, D_MODEL), f32),
                     jax.ShapeDtypeStruct((n - split_rows, D_MODEL), f32)]
    seq, _, _ = _tile_meta(seq_lens, SORT_TILE)
    row = lambda i, sq: (i, 0)
    flat = lambda i, sq: (i,)
    const = lambda i, sq: (0, 0)
    grid_spec = pltpu.PrefetchScalarGridSpec(
        num_scalar_prefetch=1,
        grid=(n // SORT_TILE,),
        in_specs=[pl.BlockSpec((TOP_K * SORT_TILE,), flat, memory_space=pltpu.SMEM),
                  pl.BlockSpec((TOP_K * SORT_TILE,), flat, memory_space=pltpu.SMEM),
                  pl.BlockSpec((SORT_TILE, D_MODEL), row),
                  pl.BlockSpec((LOCAL_ROWS * 8, 128), row),
                  pl.BlockSpec((SEQ_PAD, N_MOD * D_MODEL), const),
                  pl.BlockSpec((4, D_MODEL), const)],
        out_specs=out_specs,
        scratch_shapes=[pltpu.VMEM((SORT_TILE * 8, 128), f32)],
    )
    return pl.pallas_call(
        functools.partial(_combine_kernel, first_b),
        grid_spec=grid_spec,
        out_shape=out_shape,
        compiler_params=_cparams(("arbitrary",)),
        name="combine",
    )(seq, lrow, gflat, x, ys, ada_l, ng)


def _chunk_plan(tile_counts):
    nt = tile_counts.shape[0]
    nch = (tile_counts + CHUNK - 1) // CHUNK
    rows = nch * CHUNK
    loff = jnp.cumsum(rows, axis=1) - rows
    cend = jnp.cumsum(nch, axis=0)
    cstart = cend - nch
    total = cend[-1]
    nblk_e = (total + BLOCK_CHUNKS - 1) // BLOCK_CHUNKS
    bend = jnp.cumsum(nblk_e)
    bstart = bend - nblk_e
    n_used = bend[-1].reshape(1)
    max_chunks = nt * (TOP_K * SORT_TILE // CHUNK + N_EXPERTS)
    nblk = max_chunks // BLOCK_CHUNKS + N_EXPERTS
    b = jnp.arange(nblk, dtype=i32)
    blk_expert = jnp.minimum(jnp.sum((b[:, None] >= bend[None, :]).astype(i32), axis=1), N_EXPERTS - 1)
    onehot = (blk_expert[:, None] == jnp.arange(N_EXPERTS, dtype=i32)).astype(f32)

    def of_block(per_expert):
        table = per_expert.astype(f32).reshape(-1, N_EXPERTS).T
        return jnp.dot(onehot, table, precision=lax.Precision.HIGHEST).astype(i32)

    cend_b, cstart_b, loff_b = of_block(cend), of_block(cstart), of_block(loff)
    total_b, bstart_b = of_block(total)[:, 0], of_block(bstart)[:, 0]
    q = (b - bstart_b)[:, None] * BLOCK_CHUNKS + jnp.arange(BLOCK_CHUNKS, dtype=i32)[None, :]
    valid = (q < total_b[:, None]) & (b < n_used[0])[:, None]
    tile = jnp.minimum(jnp.sum((cend_b[:, None, :] <= q[:, :, None]).astype(i32), axis=2), nt - 1)
    pick = tile[:, :, None] == jnp.arange(nt, dtype=i32)
    cs = jnp.sum(jnp.where(pick, cstart_b[:, None, :], 0), axis=2)
    lo = jnp.sum(jnp.where(pick, loff_b[:, None, :], 0), axis=2)
    row = tile * LOCAL_ROWS + lo + (q - cs) * CHUNK
    spare = (nt * LOCAL_ROWS + (b % 2)[:, None] * SLOT_BLOCK
             + jnp.arange(BLOCK_CHUNKS, dtype=i32)[None, :] * CHUNK)
    shape = (nblk, 1, BLOCK_CHUNKS)
    chunk_src = jnp.where(valid, row, 0).astype(i32).reshape(shape)
    chunk_dst = jnp.where(valid, row, spare).astype(i32).reshape(shape)
    return chunk_src, chunk_dst, blk_expert.astype(i32), n_used.astype(i32)


def _trunk(x_parts, c_all, seq_lens, ada_w, ada_b, norm_g, w_in, pool_w, pool_scale, rpb, conv_w,
           w_branch, w_out, router_w, router_b, expert_w_gu, expert_b_gu, expert_w_down, expert_b_down):
    depth = ada_w.shape[0]
    split_rows = x_parts[0].shape[0]
    ada = _ada(c_all, ada_w, ada_b)
    x = tuple(x_parts)
    for l in range(depth):
        ada_l = ada[l]
        ng = norm_g[l]
        zs = _inproj(x, ada_l, ng[0:1], w_in[l].astype(bf16), seq_lens)
        rw = jnp.pad(router_w[l], ((0, 0), (0, 128 - N_EXPERTS)))
        rw_hi = rw.astype(bf16)
        rw_lo = (rw - rw_hi.astype(f32)).astype(bf16)
        x_mid, h2, logits_t = _mixer(
            x, zs, ada_l, ng, pool_w[l].astype(bf16), pool_scale[l].reshape(1, POOL_WIDTH),
            _bias_table(rpb[l]), conv_w[l], w_branch[l].astype(bf16), w_out[l].astype(bf16),
            rw_hi, rw_lo,
            jnp.pad(router_b[l], (0, 128 - N_EXPERTS)).reshape(1, 128), seq_lens)
        lpos, gates, tile_counts = _router(logits_t)
        chunk_src, chunk_dst, blk_expert, n_used = _chunk_plan(tile_counts[:, :, 0])
        lrow = lpos.reshape(-1)
        xs = _local_sort(h2, lrow)
        ys = _experts(xs, chunk_src, chunk_dst, blk_expert, n_used, l, expert_w_gu, expert_b_gu,
                      expert_w_down, expert_b_down)
        split = split_rows if l == depth - 1 else None
        out = _combine(x_mid, ys, lrow, gates.reshape(-1), ada_l, ng, seq_lens, split)
        x = (out,) if split is None else tuple(out)
    return x


def kernel(x_prompt, x_sample, c_prompt, c_sample, ada_w, ada_b, norm_g, w_in, pool_w, pool_scale, rpb,
           conv_w, w_branch, w_out, router_w, router_b, expert_w_gu, expert_b_gu, expert_w_down,
           expert_b_down):
    bp, lp, d = x_prompt.shape
    bs, ls, _ = x_sample.shape
    assert d == D_MODEL and bp + bs <= SEQ_PAD
    assert lp % IN_TILE == 0 and ls % IN_TILE == 0 and lp >= 2 * TILE and ls >= 2 * TILE
    x_parts = (x_prompt.reshape(bp * lp, d), x_sample.reshape(bs * ls, d))
    c_all = jnp.concatenate([c_prompt, c_sample, jnp.zeros((SEQ_PAD - bp - bs, d), f32)], axis=0)
    seq_lens = ((bp, lp), (bs, ls))
    yp, ys = _trunk(x_parts, c_all, seq_lens, ada_w, ada_b, norm_g, w_in, pool_w, pool_scale, rpb, conv_w,
                    w_branch, w_out, router_w, router_b, expert_w_gu, expert_b_gu, expert_w_down,
                    expert_b_down)
    return (yp.reshape(bp, lp, d), ys.reshape(bs, ls, d))
```

```python
import functools

import numpy as np
import jax
import jax.numpy as jnp
from jax import lax
from jax.experimental import pallas as pl
from jax.experimental.pallas import tpu as pltpu

f32 = jnp.float32
bf16 = jnp.bfloat16
i32 = jnp.int32

D_MODEL = 1024
GRID_W = 64
POOL_WINDOWS = (2, 4, 8, 16)
POOL_GROUP = 128
POOL_WIDTH = 512
N_HEADS = 8
HEAD_DIM = 64
N_PAIRS = N_HEADS // 2
ATTN_WIDTH = 512
WIN_ROWS = 8
WIN_COLS = 16
CONV_WIDTH = 512
N_BRANCH = 3
IN_COLS = POOL_WIDTH + 3 * ATTN_WIDTH + 3 * CONV_WIDTH + N_BRANCH * D_MODEL
N_EXPERTS = 32
TOP_K = 4
D_EXPERT = D_MODEL
SWIGLU_ALPHA = 1.702
SWIGLU_LIMIT = 7.0
RMS_EPS = 1e-6
N_MOD = 6
NEG_BIG = -1e30

LANES = 128
SUBLANES = 8
COL_CHUNK = 512
SEQ_PAD = 16
ROWS_PER_TILE = 4
TILE = ROWS_PER_TILE * GRID_W
HALO = 8
IN_TILE = 512
SORT_TILE = 512
CHUNK = 8
SLOT_BLOCK = 512
BLOCK_CHUNKS = SLOT_BLOCK // CHUNK
LOCAL_ROWS = TOP_K * SORT_TILE + N_EXPERTS * CHUNK
VMEM_LIMIT = 56 * 1024 * 1024

assert SUBLANES * LANES == D_MODEL
assert SORT_TILE % TILE == 0 and SLOT_BLOCK % CHUNK == 0 and CHUNK % SUBLANES == 0


def _cparams(sem):
    return pltpu.CompilerParams(dimension_semantics=sem, vmem_limit_bytes=VMEM_LIMIT)


def _tile_meta(seq_lens, tile):
    seq, pos, cnt = [], [], []
    sid = 0
    for nseq, length in seq_lens:
        nt = length // tile
        for _ in range(nseq):
            for p in range(nt):
                seq.append(sid)
                pos.append(p)
                cnt.append(nt)
            sid += 1
    return (jnp.asarray(np.array(seq, np.int32)), jnp.asarray(np.array(pos, np.int32)),
            jnp.asarray(np.array(cnt, np.int32)))


def _rms(x, g):
    ms = jnp.mean(x * x, axis=-1, keepdims=True)
    return x * lax.rsqrt(ms + RMS_EPS) * g


def _ada_kernel(c_ref, w_ref, b_ref, o_ref):
    c = c_ref[...]
    s = c * jax.nn.sigmoid(c)
    o_ref[...] = jnp.dot(s, w_ref[...], preferred_element_type=f32,
                         precision=lax.Precision.HIGHEST) + b_ref[...]


def _ada(c_all, ada_w, ada_b):
    depth = ada_w.shape[0]
    return pl.pallas_call(
        _ada_kernel,
        grid=(depth, N_MOD),
        in_specs=[pl.BlockSpec((SEQ_PAD, D_MODEL), lambda l, j: (0, 0)),
                  pl.BlockSpec((None, D_MODEL, D_MODEL), lambda l, j: (l, 0, j)),
                  pl.BlockSpec((None, 1, D_MODEL), lambda l, j: (l, 0, j))],
        out_specs=pl.BlockSpec((None, SEQ_PAD, D_MODEL), lambda l, j: (l, 0, j)),
        out_shape=jax.ShapeDtypeStruct((depth, SEQ_PAD, N_MOD * D_MODEL), f32),
        compiler_params=_cparams(("arbitrary", "arbitrary")),
        name="ada",
    )(c_all, ada_w, ada_b.reshape(depth, 1, N_MOD * D_MODEL))


_IN_SPLITS = (POOL_WIDTH, ATTN_WIDTH, ATTN_WIDTH, ATTN_WIDTH, CONV_WIDTH, CONV_WIDTH, CONV_WIDTH,
              N_BRANCH * D_MODEL)


def _pick_rows(x_refs, i, first_b):
    if len(x_refs) == 1:
        return x_refs[0][...]
    return jnp.where(i < first_b, x_refs[0][...], x_refs[1][...])


def _row_specs(x_parts, tile):
    if len(x_parts) == 1:
        return [pl.BlockSpec((tile, D_MODEL), lambda i, *_: (i, 0))], None
    first_b = x_parts[0].shape[0] // tile
    return [pl.BlockSpec((tile, D_MODEL), lambda i, *_: (jnp.minimum(i, first_b - 1), 0)),
            pl.BlockSpec((tile, D_MODEL), lambda i, *_: (jnp.maximum(i - first_b, 0), 0))], first_b


def _inproj_kernel(n_x, first_b, seq_ref, *refs):
    x_refs, (ada_ref, g_ref, w_ref), out_refs = refs[:n_x], refs[n_x:n_x + 3], refs[n_x + 3:]
    i = pl.program_id(0)
    s = seq_ref[i]
    mod = ada_ref[pl.ds(s, 1), :]
    x = _pick_rows(x_refs, i, first_b)
    h = _rms(x, g_ref[...]) * (1.0 + mod[:, D_MODEL:2 * D_MODEL]) + mod[:, 0:D_MODEL]
    h = h.astype(bf16)
    col = 0
    for o_ref in out_refs:
        width = o_ref.shape[1]
        for c0 in range(0, width, COL_CHUNK):
            o_ref[:, c0:c0 + COL_CHUNK] = jnp.dot(h, w_ref[:, col + c0:col + c0 + COL_CHUNK],
                                            preferred_element_type=f32).astype(bf16)
        col += width


def _inproj(x_parts, ada_l, g, w_in_bf, seq_lens):
    n = sum(p.shape[0] for p in x_parts)
    seq, _, _ = _tile_meta(seq_lens, IN_TILE)
    row = lambda i, s: (i, 0)
    const = lambda i, s: (0, 0)
    x_specs, first_b = _row_specs(x_parts, IN_TILE)
    grid_spec = pltpu.PrefetchScalarGridSpec(
        num_scalar_prefetch=1,
        grid=(n // IN_TILE,),
        in_specs=x_specs + [
                  pl.BlockSpec((SEQ_PAD, N_MOD * D_MODEL), const),
                  pl.BlockSpec((1, D_MODEL), const),
                  pl.BlockSpec((D_MODEL, IN_COLS), const, pipeline_mode=pl.Buffered(1))],
        out_specs=[pl.BlockSpec((IN_TILE, w), row) for w in _IN_SPLITS],
    )
    return pl.pallas_call(
        functools.partial(_inproj_kernel, len(x_parts), first_b),
        grid_spec=grid_spec,
        out_shape=[jax.ShapeDtypeStruct((n, w), bf16) for w in _IN_SPLITS],
        compiler_params=_cparams(("arbitrary",)),
        name="inproj",
    )(seq, *x_parts, ada_l, g, w_in_bf)


def _mixer_kernel(n_x, first_b, seq_ref, pos_ref, nt_ref, *refs):
    _mixer_body(_pick_rows(refs[:n_x], pl.program_id(0), first_b), seq_ref, pos_ref, nt_ref, *refs[n_x:])


def _mixer_body(x, seq_ref, pos_ref, nt_ref,
                u_ref, up_ref, un_ref, q_ref, kp_ref, kc_ref, kn_ref, vp_ref, vc_ref, vn_ref,
                b_ref, c_ref, cp_ref, cn_ref, xc_ref, xcp_ref, xcn_ref, gl_ref,
                ada_ref, ng_ref, poolw_ref, pscale_ref, bias_ref, convw_ref, wb_ref, wout_ref,
                rwh_ref, rwl_ref, rb_ref,
                xo_ref, h2_ref, lg_ref,
                uext, vext, kbuf, vbuf, pb_ref, sc_ref, e_ref):
    i = pl.program_id(0)
    s = seq_ref[i]
    pos = pos_ref[i]
    nt = nt_ref[i]
    first = pos == 0
    last = pos == nt - 1
    seq_len = nt * TILE

    uext[HALO:HALO + TILE, :] = u_ref[...].astype(f32)
    uext[0:HALO, :] = jnp.where(first, 0.0, up_ref[...].astype(f32)[GRID_W - HALO:GRID_W, :])
    uext[HALO + TILE:2 * HALO + TILE, :] = jnp.where(last, 0.0, un_ref[...].astype(f32)[0:HALO, :])

    vext[HALO:HALO + TILE, :] = c_ref[...].astype(f32) * xc_ref[...].astype(f32)
    vprev = cp_ref[...].astype(f32) * xcp_ref[...].astype(f32)
    vnext = cn_ref[...].astype(f32) * xcn_ref[...].astype(f32)
    vext[0:HALO, :] = jnp.where(first, 0.0, vprev[GRID_W - HALO:GRID_W, :])
    vext[HALO + TILE:2 * HALO + TILE, :] = jnp.where(last, 0.0, vnext[0:HALO, :])

    kbuf[0:TILE, :] = kp_ref[...]
    kbuf[TILE:2 * TILE, :] = kc_ref[...]
    kbuf[2 * TILE:3 * TILE, :] = kn_ref[...]
    vbuf[0:TILE, :] = vp_ref[...]
    vbuf[TILE:2 * TILE, :] = vc_ref[...]
    vbuf[2 * TILE:3 * TILE, :] = vn_ref[...]

    width = 2 * HEAD_DIM
    lane = lax.broadcasted_iota(i32, (1, width), 1)
    low = lane < HEAD_DIM
    rows = nt * ROWS_PER_TILE
    qscale = jnp.asarray(HEAD_DIM ** -0.5, bf16)
    eye = (lax.broadcasted_iota(i32, (width, width), 0)
           == lax.broadcasted_iota(i32, (width, width), 1)).astype(bf16)
    ones = jnp.ones((WIN_ROWS * GRID_W, width), bf16)

    def window(j):
        r = pos * ROWS_PER_TILE + j
        rs = jnp.clip(r - WIN_ROWS // 2, 0, rows - WIN_ROWS)
        koff = pl.multiple_of((rs - pos * ROWS_PER_TILE + ROWS_PER_TILE) * GRID_W, GRID_W)
        return r - rs, koff

    items = [(j, p) for j in range(ROWS_PER_TILE) for p in range(N_PAIRS)]

    def scores(n):
        j, p = items[n]
        si, koff = window(j)
        cols = slice(p * width, (p + 1) * width)
        qp = q_ref[j * GRID_W:(j + 1) * GRID_W, cols] * qscale
        zero = jnp.zeros_like(qp)
        qs = jnp.concatenate([jnp.where(low, qp, zero), jnp.where(low, zero, qp)], axis=0)
        lhs = jnp.concatenate([qs, eye], axis=1)
        rhs = jnp.concatenate([kbuf[pl.ds(koff, WIN_ROWS * GRID_W), cols], bias_ref[si, p]], axis=1)
        sc_ref[n] = lax.dot_general(lhs, rhs, (((1,), (1,)), ((), ())), preferred_element_type=f32)

    def exponentials(n):
        sc = sc_ref[n]
        e_ref[n] = jnp.exp(sc - jnp.max(sc, axis=-1, keepdims=True)).astype(bf16)

    def weighted_values(n):
        j, p = items[n]
        _, koff = window(j)
        cols = slice(p * width, (p + 1) * width)
        rhs = jnp.concatenate([vbuf[pl.ds(koff, WIN_ROWS * GRID_W), cols], ones], axis=1)
        oe = jnp.dot(e_ref[n], rhs, preferred_element_type=f32)
        o = oe[:, 0:width] / oe[:, width:2 * width]
        pb_ref[j * GRID_W:(j + 1) * GRID_W, cols] = jnp.where(low, o[0:GRID_W], o[GRID_W:2 * GRID_W]).astype(bf16)

    for stage in (scores, exponentials, weighted_values):
        for n in range(len(items)):
            stage(n)

    t = pos * TILE + lax.broadcasted_iota(i32, (TILE, 1), 0)
    pa_parts = []
    for g, w in enumerate(POOL_WINDOWS):
        cols = slice(g * POOL_GROUP, (g + 1) * POOL_GROUP)
        half = w // 2
        acc = uext[HALO - half:HALO - half + TILE, cols]
        for d in range(-half + 1, half):
            acc = acc + uext[HALO + d:HALO + d + TILE, cols]
        cnt = (jnp.minimum(t + half, seq_len) - jnp.maximum(t - half, 0)).astype(f32)
        m = acc / cnt - uext[HALO:HALO + TILE, cols]
        z = jnp.dot(m.astype(bf16), poolw_ref[g], preferred_element_type=f32)
        pa_parts.append(z * pscale_ref[:, cols])
    pa = jnp.concatenate(pa_parts, axis=1).astype(bf16)

    y = (convw_ref[0:1, :] * vext[HALO - 1:HALO - 1 + TILE, :]
         + convw_ref[1:2, :] * vext[HALO:HALO + TILE, :]
         + convw_ref[2:3, :] * vext[HALO + 1:HALO + 1 + TILE, :])
    pc = (b_ref[...].astype(f32) * y).astype(bf16)

    branches = (pa, pb_ref[...], pc)
    merged = None
    for n in range(N_BRANCH):
        proj = jnp.dot(branches[n], wb_ref[n], preferred_element_type=f32)
        gate = jax.nn.sigmoid(gl_ref[:, n * D_MODEL:(n + 1) * D_MODEL].astype(f32))
        merged = gate * proj if merged is None else merged + gate * proj
    mix = jnp.dot(merged.astype(bf16), wout_ref[...], preferred_element_type=f32)

    mod = ada_ref[pl.ds(s, 1), :]
    gt1 = mod[:, 2 * D_MODEL:3 * D_MODEL]
    sh2 = mod[:, 3 * D_MODEL:4 * D_MODEL]
    sc2 = mod[:, 4 * D_MODEL:5 * D_MODEL]
    xn = x + gt1 * _rms(mix, ng_ref[1:2, :])
    xo_ref[...] = xn
    h2 = _rms(xn, ng_ref[2:3, :]) * (1.0 + sc2) + sh2
    _to_token_tiles(h2_ref, h2)
    h_hi = h2.astype(bf16)
    h_lo = (h2 - h_hi.astype(f32)).astype(bf16)
    logits = (jnp.dot(h_hi, rwh_ref[...], preferred_element_type=f32)
              + jnp.dot(h_lo, rwh_ref[...], preferred_element_type=f32)
              + jnp.dot(h_hi, rwl_ref[...], preferred_element_type=f32)
              + rb_ref[...])
    lg_ref[...] = logits.T[0:N_EXPERTS, :]


def _mixer(x_parts, zs, ada_l, ng, poolw_bf, pscale, bias_tab, convw, wb_bf, wout_bf, rw_hi, rw_lo, rb,
           seq_lens):
    n = sum(p.shape[0] for p in x_parts)
    zu, zq, zk, zv, zb, zc, zx, zg = zs
    x_specs, first_b = _row_specs(x_parts, TILE)
    seq, pos, cnt = _tile_meta(seq_lens, TILE)
    per = TILE // GRID_W

    cur = lambda i, sq, ps, ct: (i, 0)
    prev_t = lambda i, sq, ps, ct: (jnp.where(ps[i] > 0, i - 1, i), 0)
    next_t = lambda i, sq, ps, ct: (jnp.where(ps[i] < ct[i] - 1, i + 1, i), 0)
    prev_h = lambda i, sq, ps, ct: (jnp.where(ps[i] > 0, i * per - 1, i * per), 0)
    next_h = lambda i, sq, ps, ct: (jnp.where(ps[i] < ct[i] - 1, (i + 1) * per, i * per), 0)
    c2 = lambda i, sq, ps, ct: (0, 0)
    c3 = lambda i, sq, ps, ct: (0, 0, 0)
    c4 = lambda i, sq, ps, ct: (0, 0, 0, 0)

    tile512 = lambda im: pl.BlockSpec((TILE, ATTN_WIDTH), im)
    halo512 = lambda im: pl.BlockSpec((GRID_W, ATTN_WIDTH), im)
    once = pl.Buffered(1)
    in_specs = x_specs + [
        tile512(cur), halo512(prev_h), halo512(next_h),
        tile512(cur),
        tile512(prev_t), tile512(cur), tile512(next_t),
        tile512(prev_t), tile512(cur), tile512(next_t),
        tile512(cur),
        tile512(cur), halo512(prev_h), halo512(next_h),
        tile512(cur), halo512(prev_h), halo512(next_h),
        pl.BlockSpec((TILE, N_BRANCH * D_MODEL), cur),
        pl.BlockSpec((SEQ_PAD, N_MOD * D_MODEL), c2),
        pl.BlockSpec((4, D_MODEL), c2),
        pl.BlockSpec((4, POOL_GROUP, POOL_GROUP), c3),
        pl.BlockSpec((1, POOL_WIDTH), c2),
        pl.BlockSpec((WIN_ROWS, N_PAIRS, WIN_ROWS * GRID_W, 2 * GRID_W), c4, pipeline_mode=once),
        pl.BlockSpec((3, CONV_WIDTH), c2),
        pl.BlockSpec((N_BRANCH, POOL_WIDTH, D_MODEL), c3, pipeline_mode=once),
        pl.BlockSpec((D_MODEL, D_MODEL), c2, pipeline_mode=once),
        pl.BlockSpec((D_MODEL, LANES), c2),
        pl.BlockSpec((D_MODEL, LANES), c2),
        pl.BlockSpec((1, LANES), c2),
    ]
    out_specs = [
        pl.BlockSpec((TILE, D_MODEL), cur),
        pl.BlockSpec((TILE * SUBLANES, LANES), cur),
        pl.BlockSpec((N_EXPERTS, TILE), lambda i, sq, ps, ct: (0, i)),
    ]
    grid_spec = pltpu.PrefetchScalarGridSpec(
        num_scalar_prefetch=3,
        grid=(n // TILE,),
        in_specs=in_specs,
        out_specs=out_specs,
        scratch_shapes=[pltpu.VMEM((TILE + 2 * HALO, POOL_WIDTH), f32),
                        pltpu.VMEM((TILE + 2 * HALO, CONV_WIDTH), f32),
                        pltpu.VMEM((3 * TILE, ATTN_WIDTH), bf16),
                        pltpu.VMEM((3 * TILE, ATTN_WIDTH), bf16),
                        pltpu.VMEM((TILE, ATTN_WIDTH), bf16),
                        pltpu.VMEM((ROWS_PER_TILE * N_PAIRS, 2 * GRID_W, WIN_ROWS * GRID_W), f32),
                        pltpu.VMEM((ROWS_PER_TILE * N_PAIRS, 2 * GRID_W, WIN_ROWS * GRID_W), bf16)],
    )
    return pl.pallas_call(
        functools.partial(_mixer_kernel, len(x_parts), first_b),
        grid_spec=grid_spec,
        out_shape=[jax.ShapeDtypeStruct((n, D_MODEL), f32),
                   jax.ShapeDtypeStruct((n * SUBLANES, LANES), f32),
                   jax.ShapeDtypeStruct((N_EXPERTS, n), f32)],
        compiler_params=_cparams(("arbitrary",)),
        name="mixer",
    )(seq, pos, cnt,
      *x_parts, zu, zu, zu, zq, zk, zk, zk, zv, zv, zv, zb, zc, zc, zc, zx, zx, zx, zg,
      ada_l, ng, poolw_bf, pscale, bias_tab, convw, wb_bf, wout_bf, rw_hi, rw_lo, rb)


def _bias_table(rpb_l):
    c = np.arange(GRID_W)[:, None]
    kc = np.arange(GRID_W)[None, :]
    cs = np.clip(c - WIN_COLS // 2, 0, GRID_W - WIN_COLS)
    valid = (kc >= cs) & (kc < cs + WIN_COLS)
    didx = np.clip(kc - c + WIN_COLS - 1, 0, 2 * WIN_COLS - 2)
    select = (didx[:, :, None] == np.arange(2 * WIN_COLS - 1)).astype(np.float32)
    dense = jnp.einsum('hrd,ckd->hrck', rpb_l, jnp.asarray(select), precision=lax.Precision.HIGHEST)
    dense = jnp.where(valid[None, None], dense, NEG_BIG)
    slabs = [dense[:, WIN_ROWS - 1 - s:2 * WIN_ROWS - 1 - s] for s in range(WIN_ROWS)]
    tab = jnp.stack(slabs, axis=0)
    tab = tab.transpose(0, 1, 2, 4, 3)
    tab = tab.reshape(WIN_ROWS, N_PAIRS, 2, WIN_ROWS * GRID_W, GRID_W).transpose(0, 1, 3, 2, 4)
    return tab.reshape(WIN_ROWS, N_PAIRS, WIN_ROWS * GRID_W, 2 * GRID_W).astype(bf16)


def _router_kernel(lg_ref, lp_ref, gt_ref, tc_ref):
    l = lg_ref[...]
    eio = lax.broadcasted_iota(i32, l.shape, 0)
    vals, hots = [], []
    for k in range(TOP_K):
        m = jnp.max(l, axis=0, keepdims=True)
        idx = jnp.min(jnp.where(l == m, eio, N_EXPERTS), axis=0, keepdims=True)
        hot = eio == idx
        l = jnp.where(hot, -jnp.inf, l)
        vals.append(m)
        hots.append(hot)
    ex = [jnp.exp(v - vals[0]) for v in vals]
    den = ex[0] + ex[1] + ex[2] + ex[3]
    for k in range(TOP_K):
        gt_ref[k:k + 1, :] = ex[k] / den

    hot_all = jnp.zeros(l.shape, f32)
    for hot in hots:
        hot_all = hot_all + hot.astype(f32)
    rr = lax.broadcasted_iota(i32, (SORT_TILE, SORT_TILE), 0)
    cc = lax.broadcasted_iota(i32, (SORT_TILE, SORT_TILE), 1)
    tri = (rr <= cc).astype(bf16)
    pref = jnp.dot(hot_all.astype(bf16), tri, preferred_element_type=f32)
    cnt = jnp.broadcast_to(pref[:, SORT_TILE - 1:SORT_TILE], (N_EXPERTS, LANES))
    padded = jnp.floor((cnt + (CHUNK - 1)) * (1.0 / CHUNK)) * CHUNK
    er = lax.broadcasted_iota(i32, (N_EXPERTS, N_EXPERTS), 0)
    ec = lax.broadcasted_iota(i32, (N_EXPERTS, N_EXPERTS), 1)
    before = (ec < er).astype(f32)
    start = jnp.dot(before, padded, preferred_element_type=f32, precision=lax.Precision.HIGHEST)
    tot = start[:, 0:1] + pref - 1.0
    for k in range(TOP_K):
        row = jnp.sum(jnp.where(hots[k], tot, 0.0), axis=0, keepdims=True).astype(i32)
        lp_ref[k:k + 1, :] = row * SUBLANES
    tc_ref[...] = cnt.astype(i32)


def _router(logits_t):
    n = logits_t.shape[1]
    blk = lambda i: (0, i)
    return pl.pallas_call(
        _router_kernel,
        grid=(n // SORT_TILE,),
        in_specs=[pl.BlockSpec((N_EXPERTS, SORT_TILE), blk)],
        out_specs=[pl.BlockSpec((None, TOP_K, SORT_TILE), lambda i: (i, 0, 0)),
                   pl.BlockSpec((None, TOP_K, SORT_TILE), lambda i: (i, 0, 0)),
                   pl.BlockSpec((None, N_EXPERTS, LANES), lambda i: (i, 0, 0))],
        out_shape=[jax.ShapeDtypeStruct((n // SORT_TILE, TOP_K, SORT_TILE), i32),
                   jax.ShapeDtypeStruct((n // SORT_TILE, TOP_K, SORT_TILE), f32),
                   jax.ShapeDtypeStruct((n // SORT_TILE, N_EXPERTS, LANES), i32)],
        compiler_params=_cparams(("arbitrary",)),
        name="router",
    )(logits_t)


def _to_token_tiles(dst_ref, val):
    rows = val.shape[0]
    for s in range(SUBLANES):
        dst_ref[pl.ds(s, rows, stride=SUBLANES), :] = val[:, s * LANES:(s + 1) * LANES]


def _from_token_tiles(src_ref, rows):
    return jnp.concatenate([src_ref[pl.ds(s, rows, stride=SUBLANES), :] for s in range(SUBLANES)], axis=1)


def _sort_kernel(lp_ref, h_ref, o_ref):
    o_ref[...] = jnp.zeros_like(o_ref)

    def body(t, carry):
        v = h_ref[pl.ds(pl.multiple_of(t * SUBLANES, SUBLANES), SUBLANES), :]
        for k in range(TOP_K):
            o_ref[pl.ds(pl.multiple_of(lp_ref[k * SORT_TILE + t], SUBLANES), SUBLANES), :] = v
        return carry

    lax.fori_loop(0, SORT_TILE, body, 0, unroll=8)


def _local_sort(h2, lrow):
    nt = h2.shape[0] // (SORT_TILE * SUBLANES)
    return pl.pallas_call(
        _sort_kernel,
        grid=(nt,),
        in_specs=[pl.BlockSpec((TOP_K * SORT_TILE,), lambda i: (i,), memory_space=pltpu.SMEM),
                  pl.BlockSpec((SORT_TILE * SUBLANES, LANES), lambda i: (i, 0))],
        out_specs=pl.BlockSpec((LOCAL_ROWS * SUBLANES, LANES), lambda i: (i, 0)),
        out_shape=jax.ShapeDtypeStruct((nt * LOCAL_ROWS * SUBLANES, LANES), f32),
        compiler_params=_cparams(("arbitrary",)),
        name="local_sort",
    )(lrow, h2)


def _expert_kernel(be_ref, nu_ref, src_ref, srcn_ref, dst_ref, xs_ref, wgu_ref, bgu_ref, wd_ref, bd_ref,
                   ys_ref, xbuf, ybuf, wgu_bf, wd_bf, act, sem_in, sem_out):
    b = pl.program_id(0)
    nb = pl.num_programs(0)
    nu = nu_ref[0]
    slot = b % 2
    e = be_ref[b]
    prev = be_ref[jnp.maximum(b - 1, 0)]
    chunk_rows = CHUNK * SUBLANES

    def gather_copy(idx_ref, c, s):
        src = pl.multiple_of(idx_ref[0, 0, c] * SUBLANES, chunk_rows)
        return pltpu.make_async_copy(xs_ref.at[pl.ds(src, chunk_rows), :],
                                     xbuf.at[s, pl.ds(c * chunk_rows, chunk_rows), :], sem_in.at[s])

    def scatter_copy(c, s):
        dst = pl.multiple_of(dst_ref[0, 0, c] * SUBLANES, chunk_rows)
        return pltpu.make_async_copy(ybuf.at[s, pl.ds(c * chunk_rows, chunk_rows), :],
                                     ys_ref.at[pl.ds(dst, chunk_rows), :], sem_out.at[s])

    def start_gather(idx_ref, s):
        for c in range(BLOCK_CHUNKS):
            gather_copy(idx_ref, c, s).start()

    def wait_gather(s):
        for c in range(BLOCK_CHUNKS):
            gather_copy(src_ref, c, s).wait()

    def wait_scatter(s):
        for c in range(BLOCK_CHUNKS):
            scatter_copy(c, s).wait()

    @pl.when(b == 0)
    def _():
        start_gather(src_ref, 0)

    @pl.when(b + 1 < nu)
    def _():
        start_gather(srcn_ref, 1 - slot)

    @pl.when((b >= 2) & (b - 2 < nu))
    def _():
        wait_scatter(slot)

    @pl.when((b == 0) | (e != prev))
    def _():
        wgu_bf[...] = wgu_ref[...].astype(bf16)
        wd_bf[...] = wd_ref[...].astype(bf16)

    @pl.when(b < nu)
    def _():
        wait_gather(slot)
        x = _from_token_tiles(xbuf.at[slot], SLOT_BLOCK).astype(bf16)
        for c0 in range(0, D_EXPERT, COL_CHUNK):
            c1 = c0 + COL_CHUNK
            gate = jnp.dot(x, wgu_bf[:, c0:c1], preferred_element_type=f32) + bgu_ref[:, c0:c1]
            up = (jnp.dot(x, wgu_bf[:, D_EXPERT + c0:D_EXPERT + c1], preferred_element_type=f32)
                  + bgu_ref[:, D_EXPERT + c0:D_EXPERT + c1])
            gate = jnp.minimum(gate, SWIGLU_LIMIT)
            up = jnp.clip(up, -SWIGLU_LIMIT, SWIGLU_LIMIT)
            act[:, c0:c1] = ((up + 1.0) * (gate * jax.nn.sigmoid(gate * SWIGLU_ALPHA))).astype(bf16)
        y = jnp.dot(act[...], wd_bf[...], preferred_element_type=f32) + bd_ref[...]
        _to_token_tiles(ybuf.at[slot], y)
        for c in range(BLOCK_CHUNKS):
            scatter_copy(c, slot).start()

    @pl.when(b == nb - 1)
    def _():
        @pl.when((b >= 1) & (b - 1 < nu))
        def _():
            wait_scatter(1 - slot)

        @pl.when(b < nu)
        def _():
            wait_scatter(slot)


def _experts(xs, chunk_src, chunk_dst, blk_expert, n_used, layer, w_gu, b_gu, w_down, b_down):
    nblk = chunk_src.shape[0]
    depth = w_gu.shape[0]
    wsel = lambda b, be, nu: (layer, be[b], 0, 0)
    cur = lambda b, be, nu: (b, 0, 0)
    nxt = lambda b, be, nu: (jnp.minimum(b + 1, nblk - 1), 0, 0)
    idx_spec = lambda im: pl.BlockSpec((1, 1, BLOCK_CHUNKS), im, memory_space=pltpu.SMEM)
    grid_spec = pltpu.PrefetchScalarGridSpec(
        num_scalar_prefetch=2,
        grid=(nblk,),
        in_specs=[idx_spec(cur), idx_spec(nxt), idx_spec(cur),
                  pl.BlockSpec(memory_space=pl.ANY),
                  pl.BlockSpec((None, None, D_MODEL, 2 * D_EXPERT), wsel),
                  pl.BlockSpec((None, None, 1, 2 * D_EXPERT), wsel),
                  pl.BlockSpec((None, None, D_EXPERT, D_MODEL), wsel),
                  pl.BlockSpec((None, None, 1, D_MODEL), wsel)],
        out_specs=pl.BlockSpec(memory_space=pl.ANY),
        scratch_shapes=[pltpu.VMEM((2, SLOT_BLOCK * SUBLANES, LANES), f32),
                        pltpu.VMEM((2, SLOT_BLOCK * SUBLANES, LANES), f32),
                        pltpu.VMEM((D_MODEL, 2 * D_EXPERT), bf16),
                        pltpu.VMEM((D_EXPERT, D_MODEL), bf16),
                        pltpu.VMEM((SLOT_BLOCK, D_EXPERT), bf16),
                        pltpu.SemaphoreType.DMA((2,)),
                        pltpu.SemaphoreType.DMA((2,))],
    )
    return pl.pallas_call(
        _expert_kernel,
        grid_spec=grid_spec,
        out_shape=jax.ShapeDtypeStruct((xs.shape[0] + 2 * SLOT_BLOCK * SUBLANES, LANES), f32),
        compiler_params=_cparams(("arbitrary",)),
        name="experts",
    )(blk_expert, n_used, chunk_src, chunk_src, chunk_dst, xs, w_gu,
      b_gu.reshape(depth, N_EXPERTS, 1, 2 * D_EXPERT), w_down,
      b_down.reshape(depth, N_EXPERTS, 1, D_MODEL))


def _combine_kernel(first_b, seq_ref, lp_ref, gt_ref, x_ref, ys_ref, ada_ref, ng_ref, *refs):
    ff = refs[-1]

    def body(t, carry):
        acc = gt_ref[t] * ys_ref[pl.ds(pl.multiple_of(lp_ref[t], SUBLANES), SUBLANES), :]
        for k in range(1, TOP_K):
            acc = acc + (gt_ref[k * SORT_TILE + t]
                         * ys_ref[pl.ds(pl.multiple_of(lp_ref[k * SORT_TILE + t], SUBLANES), SUBLANES), :])
        ff[pl.ds(pl.multiple_of(t * SUBLANES, SUBLANES), SUBLANES), :] = acc
        return carry

    lax.fori_loop(0, SORT_TILE, body, 0, unroll=8)
    i = pl.program_id(0)
    s = seq_ref[i]
    gt2 = ada_ref[pl.ds(s, 1), 5 * D_MODEL:6 * D_MODEL]
    out = x_ref[...] + gt2 * _rms(_from_token_tiles(ff, SORT_TILE), ng_ref[3:4, :])
    if first_b is None:
        refs[0][...] = out
    else:
        @pl.when(i < first_b)
        def _():
            refs[0][...] = out

        @pl.when(i >= first_b)
        def _():
            refs[1][...] = out


def _combine(x, ys, lrow, gflat, ada_l, ng, seq_lens, split_rows=None):
    n = x.shape[0]
    if split_rows is None:
        first_b = None
        out_specs = pl.BlockSpec((SORT_TILE, D_MODEL), lambda i, sq: (i, 0))
        out_shape = jax.ShapeDtypeStruct((n, D_MODEL), f32)
    else:
        first_b = split_rows // SORT_TILE
        out_specs = [pl.BlockSpec((SORT_TILE, D_MODEL), lambda i, sq: (jnp.minimum(i, first_b - 1), 0)),
                     pl.BlockSpec((SORT_TILE, D_MODEL), lambda i, sq: (jnp.maximum(i - first_b, 0), 0))]
        out_shape = [jax.ShapeDtypeStruct((split_rows, D_MODEL), f32),
                     jax.ShapeDtypeStruct((n - split_rows, D_MODEL), f32)]
    seq, _, _ = _tile_meta(seq_lens, SORT_TILE)
    row = lambda i, sq: (i, 0)
    flat = lambda i, sq: (i,)
    const = lambda i, sq: (0, 0)
    grid_spec = pltpu.PrefetchScalarGridSpec(
        num_scalar_prefetch=1,
        grid=(n // SORT_TILE,),
        in_specs=[pl.BlockSpec((TOP_K * SORT_TILE,), flat, memory_space=pltpu.SMEM),
                  pl.BlockSpec((TOP_K * SORT_TILE,), flat, memory_space=pltpu.SMEM),
                  pl.BlockSpec((SORT_TILE, D_MODEL), row),
                  pl.BlockSpec((LOCAL_ROWS * SUBLANES, LANES), row),
                  pl.BlockSpec((SEQ_PAD, N_MOD * D_MODEL), const),
                  pl.BlockSpec((4, D_MODEL), const)],
        out_specs=out_specs,
        scratch_shapes=[pltpu.VMEM((SORT_TILE * SUBLANES, LANES), f32)],
    )
    return pl.pallas_call(
        functools.partial(_combine_kernel, first_b),
        grid_spec=grid_spec,
        out_shape=out_shape,
        compiler_params=_cparams(("arbitrary",)),
        name="combine",
    )(seq, lrow, gflat, x, ys, ada_l, ng)


def _chunk_plan(tile_counts):
    nt = tile_counts.shape[0]
    nch = (tile_counts + CHUNK - 1) // CHUNK
    rows = nch * CHUNK
    loff = jnp.cumsum(rows, axis=1) - rows
    cend = jnp.cumsum(nch, axis=0)
    cstart = cend - nch
    total = cend[-1]
    nblk_e = (total + BLOCK_CHUNKS - 1) // BLOCK_CHUNKS
    bend = jnp.cumsum(nblk_e)
    bstart = bend - nblk_e
    n_used = bend[-1].reshape(1)
    max_chunks = nt * (TOP_K * SORT_TILE // CHUNK + N_EXPERTS)
    nblk = max_chunks // BLOCK_CHUNKS + N_EXPERTS
    b = jnp.arange(nblk, dtype=i32)
    blk_expert = jnp.minimum(jnp.sum((b[:, None] >= bend[None, :]).astype(i32), axis=1), N_EXPERTS - 1)
    onehot = (blk_expert[:, None] == jnp.arange(N_EXPERTS, dtype=i32)).astype(f32)

    def of_block(per_expert):
        table = per_expert.astype(f32).reshape(-1, N_EXPERTS).T
        return jnp.dot(onehot, table, precision=lax.Precision.HIGHEST).astype(i32)

    cend_b, cstart_b, loff_b = of_block(cend), of_block(cstart), of_block(loff)
    total_b, bstart_b = of_block(total)[:, 0], of_block(bstart)[:, 0]
    q = (b - bstart_b)[:, None] * BLOCK_CHUNKS + jnp.arange(BLOCK_CHUNKS, dtype=i32)[None, :]
    valid = (q < total_b[:, None]) & (b < n_used[0])[:, None]
    tile = jnp.minimum(jnp.sum((cend_b[:, None, :] <= q[:, :, None]).astype(i32), axis=2), nt - 1)
    pick = tile[:, :, None] == jnp.arange(nt, dtype=i32)
    cs = jnp.sum(jnp.where(pick, cstart_b[:, None, :], 0), axis=2)
    lo = jnp.sum(jnp.where(pick, loff_b[:, None, :], 0), axis=2)
    row = tile * LOCAL_ROWS + lo + (q - cs) * CHUNK
    spare = (nt * LOCAL_ROWS + (b % 2)[:, None] * SLOT_BLOCK
             + jnp.arange(BLOCK_CHUNKS, dtype=i32)[None, :] * CHUNK)
    shape = (nblk, 1, BLOCK_CHUNKS)
    chunk_src = jnp.where(valid, row, 0).astype(i32).reshape(shape)
    chunk_dst = jnp.where(valid, row, spare).astype(i32).reshape(shape)
    return chunk_src, chunk_dst, blk_expert.astype(i32), n_used.astype(i32)


def _trunk(x_parts, c_all, seq_lens, ada_w, ada_b, norm_g, w_in, pool_w, pool_scale, rpb, conv_w,
           w_branch, w_out, router_w, router_b, expert_w_gu, expert_b_gu, expert_w_down, expert_b_down):
    depth = ada_w.shape[0]
    split_rows = x_parts[0].shape[0]
    ada = _ada(c_all, ada_w, ada_b)
    x = tuple(x_parts)
    for l in range(depth):
        ada_l = ada[l]
        ng = norm_g[l]
        zs = _inproj(x, ada_l, ng[0:1], w_in[l].astype(bf16), seq_lens)
        rw = jnp.pad(router_w[l], ((0, 0), (0, LANES - N_EXPERTS)))
        rw_hi = rw.astype(bf16)
        rw_lo = (rw - rw_hi.astype(f32)).astype(bf16)
        x_mid, h2, logits_t = _mixer(
            x, zs, ada_l, ng, pool_w[l].astype(bf16), pool_scale[l].reshape(1, POOL_WIDTH),
            _bias_table(rpb[l]), conv_w[l], w_branch[l].astype(bf16), w_out[l].astype(bf16),
            rw_hi, rw_lo,
            jnp.pad(router_b[l], (0, LANES - N_EXPERTS)).reshape(1, LANES), seq_lens)
        lpos, gates, tile_counts = _router(logits_t)
        chunk_src, chunk_dst, blk_expert, n_used = _chunk_plan(tile_counts[:, :, 0])
        lrow = lpos.reshape(-1)
        xs = _local_sort(h2, lrow)
        ys = _experts(xs, chunk_src, chunk_dst, blk_expert, n_used, l, expert_w_gu, expert_b_gu,
                      expert_w_down, expert_b_down)
        split = split_rows if l == depth - 1 else None
        out = _combine(x_mid, ys, lrow, gates.reshape(-1), ada_l, ng, seq_lens, split)
        x = (out,) if split is None else tuple(out)
    return x


def kernel(x_prompt, x_sample, c_prompt, c_sample, ada_w, ada_b, norm_g, w_in, pool_w, pool_scale, rpb,
           conv_w, w_branch, w_out, router_w, router_b, expert_w_gu, expert_b_gu, expert_w_down,
           expert_b_down):
    bp, lp, d = x_prompt.shape
    bs, ls, _ = x_sample.shape
    assert d == D_MODEL and bp + bs <= SEQ_PAD
    assert lp % IN_TILE == 0 and ls % IN_TILE == 0 and lp >= 2 * TILE and ls >= 2 * TILE
    x_parts = (x_prompt.reshape(bp * lp, d), x_sample.reshape(bs * ls, d))
    c_all = jnp.concatenate([c_prompt, c_sample, jnp.zeros((SEQ_PAD - bp - bs, d), f32)], axis=0)
    seq_lens = ((bp, lp), (bs, ls))
    yp, ys = _trunk(x_parts, c_all, seq_lens, ada_w, ada_b, norm_g, w_in, pool_w, pool_scale, rpb, conv_w,
                    w_branch, w_out, router_w, router_b, expert_w_gu, expert_b_gu, expert_w_down,
                    expert_b_down)
    return (yp.reshape(bp, lp, d), ys.reshape(bs, ls, d))
```

```python
import functools

import numpy as np
import jax
import jax.numpy as jnp
from jax import lax
from jax.experimental import pallas as pl
from jax.experimental.pallas import tpu as pltpu

f32 = jnp.float32
bf16 = jnp.bfloat16
i32 = jnp.int32

D_MODEL = 1024
GRID_W = 64
POOL_WINDOWS = (2, 4, 8, 16)
POOL_GROUP = 128
POOL_WIDTH = 512
N_HEADS = 8
HEAD_DIM = 64
N_PAIRS = N_HEADS // 2
ATTN_WIDTH = 512
WIN_ROWS = 8
WIN_COLS = 16
CONV_WIDTH = 512
N_BRANCH = 3
IN_COLS = POOL_WIDTH + 3 * ATTN_WIDTH + 3 * CONV_WIDTH + N_BRANCH * D_MODEL
N_EXPERTS = 32
TOP_K = 4
D_EXPERT = D_MODEL
SWIGLU_ALPHA = 1.702
SWIGLU_LIMIT = 7.0
RMS_EPS = 1e-6
N_MOD = 6
NEG_BIG = -1e30

LANES = 128
SUBLANES = 8
COL_CHUNK = 512
SEQ_PAD = 16
ROWS_PER_TILE = 4
TILE = ROWS_PER_TILE * GRID_W
HALO = 8
IN_TILE = 512
SORT_TILE = 512
CHUNK = 8
SLOT_BLOCK = 768
BLOCK_CHUNKS = SLOT_BLOCK // CHUNK
LOCAL_ROWS = TOP_K * SORT_TILE + N_EXPERTS * CHUNK
VMEM_LIMIT = 56 * 1024 * 1024

assert SUBLANES * LANES == D_MODEL
assert SORT_TILE % TILE == 0 and SLOT_BLOCK % CHUNK == 0 and CHUNK % SUBLANES == 0


def _cparams(sem):
    return pltpu.CompilerParams(dimension_semantics=sem, vmem_limit_bytes=VMEM_LIMIT)


def _tile_meta(seq_lens, tile):
    seq, pos, cnt = [], [], []
    sid = 0
    for nseq, length in seq_lens:
        nt = length // tile
        for _ in range(nseq):
            for p in range(nt):
                seq.append(sid)
                pos.append(p)
                cnt.append(nt)
            sid += 1
    return (jnp.asarray(np.array(seq, np.int32)), jnp.asarray(np.array(pos, np.int32)),
            jnp.asarray(np.array(cnt, np.int32)))


def _rms(x, g):
    ms = jnp.mean(x * x, axis=-1, keepdims=True)
    return x * lax.rsqrt(ms + RMS_EPS) * g


def _ada_kernel(c_ref, w_ref, b_ref, o_ref):
    c = c_ref[...]
    s = c * jax.nn.sigmoid(c)
    o_ref[...] = jnp.dot(s, w_ref[...], preferred_element_type=f32,
                         precision=lax.Precision.HIGHEST) + b_ref[...]


def _ada(c_all, ada_w, ada_b):
    depth = ada_w.shape[0]
    return pl.pallas_call(
        _ada_kernel,
        grid=(depth, N_MOD),
        in_specs=[pl.BlockSpec((SEQ_PAD, D_MODEL), lambda l, j: (0, 0)),
                  pl.BlockSpec((None, D_MODEL, D_MODEL), lambda l, j: (l, 0, j)),
                  pl.BlockSpec((None, 1, D_MODEL), lambda l, j: (l, 0, j))],
        out_specs=pl.BlockSpec((None, SEQ_PAD, D_MODEL), lambda l, j: (l, 0, j)),
        out_shape=jax.ShapeDtypeStruct((depth, SEQ_PAD, N_MOD * D_MODEL), f32),
        compiler_params=_cparams(("arbitrary", "arbitrary")),
        name="ada",
    )(c_all, ada_w, ada_b.reshape(depth, 1, N_MOD * D_MODEL))


_IN_SPLITS = (POOL_WIDTH, ATTN_WIDTH, ATTN_WIDTH, ATTN_WIDTH, CONV_WIDTH, CONV_WIDTH, CONV_WIDTH,
              N_BRANCH * D_MODEL)


def _pick_rows(x_refs, i, first_b):
    if len(x_refs) == 1:
        return x_refs[0][...]
    return jnp.where(i < first_b, x_refs[0][...], x_refs[1][...])


def _row_specs(x_parts, tile):
    if len(x_parts) == 1:
        return [pl.BlockSpec((tile, D_MODEL), lambda i, *_: (i, 0))], None
    first_b = x_parts[0].shape[0] // tile
    return [pl.BlockSpec((tile, D_MODEL), lambda i, *_: (jnp.minimum(i, first_b - 1), 0)),
            pl.BlockSpec((tile, D_MODEL), lambda i, *_: (jnp.maximum(i - first_b, 0), 0))], first_b


def _inproj_kernel(n_x, first_b, seq_ref, *refs):
    x_refs, (ada_ref, g_ref, w_ref), out_refs = refs[:n_x], refs[n_x:n_x + 3], refs[n_x + 3:]
    i = pl.program_id(0)
    s = seq_ref[i]
    mod = ada_ref[pl.ds(s, 1), :]
    x = _pick_rows(x_refs, i, first_b)
    h = _rms(x, g_ref[...]) * (1.0 + mod[:, D_MODEL:2 * D_MODEL]) + mod[:, 0:D_MODEL]
    h = h.astype(bf16)
    col = 0
    for o_ref in out_refs:
        width = o_ref.shape[1]
        for c0 in range(0, width, COL_CHUNK):
            o_ref[:, c0:c0 + COL_CHUNK] = jnp.dot(h, w_ref[:, col + c0:col + c0 + COL_CHUNK],
                                            preferred_element_type=f32).astype(bf16)
        col += width


def _inproj(x_parts, ada_l, g, w_in_bf, seq_lens):
    n = sum(p.shape[0] for p in x_parts)
    seq, _, _ = _tile_meta(seq_lens, IN_TILE)
    row = lambda i, s: (i, 0)
    const = lambda i, s: (0, 0)
    x_specs, first_b = _row_specs(x_parts, IN_TILE)
    grid_spec = pltpu.PrefetchScalarGridSpec(
        num_scalar_prefetch=1,
        grid=(n // IN_TILE,),
        in_specs=x_specs + [
                  pl.BlockSpec((SEQ_PAD, N_MOD * D_MODEL), const),
                  pl.BlockSpec((1, D_MODEL), const),
                  pl.BlockSpec((D_MODEL, IN_COLS), const, pipeline_mode=pl.Buffered(1))],
        out_specs=[pl.BlockSpec((IN_TILE, w), row) for w in _IN_SPLITS],
    )
    return pl.pallas_call(
        functools.partial(_inproj_kernel, len(x_parts), first_b),
        grid_spec=grid_spec,
        out_shape=[jax.ShapeDtypeStruct((n, w), bf16) for w in _IN_SPLITS],
        compiler_params=_cparams(("arbitrary",)),
        name="inproj",
    )(seq, *x_parts, ada_l, g, w_in_bf)


def _mixer_kernel(n_x, first_b, seq_ref, pos_ref, nt_ref, *refs):
    _mixer_body(_pick_rows(refs[:n_x], pl.program_id(0), first_b), seq_ref, pos_ref, nt_ref, *refs[n_x:])


def _mixer_body(x, seq_ref, pos_ref, nt_ref,
                u_ref, up_ref, un_ref, q_ref, kp_ref, kc_ref, kn_ref, vp_ref, vc_ref, vn_ref,
                b_ref, c_ref, cp_ref, cn_ref, xc_ref, xcp_ref, xcn_ref, gl_ref,
                ada_ref, ng_ref, poolw_ref, pscale_ref, bias_ref, convw_ref, wb_ref, wout_ref,
                rwh_ref, rwl_ref, rb_ref,
                xo_ref, h2_ref, lg_ref,
                uext, vext, kbuf, vbuf, pb_ref, sc_ref, e_ref):
    i = pl.program_id(0)
    s = seq_ref[i]
    pos = pos_ref[i]
    nt = nt_ref[i]
    first = pos == 0
    last = pos == nt - 1
    seq_len = nt * TILE

    uext[HALO:HALO + TILE, :] = u_ref[...].astype(f32)
    uext[0:HALO, :] = jnp.where(first, 0.0, up_ref[...].astype(f32)[GRID_W - HALO:GRID_W, :])
    uext[HALO + TILE:2 * HALO + TILE, :] = jnp.where(last, 0.0, un_ref[...].astype(f32)[0:HALO, :])

    vext[HALO:HALO + TILE, :] = c_ref[...].astype(f32) * xc_ref[...].astype(f32)
    vprev = cp_ref[...].astype(f32) * xcp_ref[...].astype(f32)
    vnext = cn_ref[...].astype(f32) * xcn_ref[...].astype(f32)
    vext[0:HALO, :] = jnp.where(first, 0.0, vprev[GRID_W - HALO:GRID_W, :])
    vext[HALO + TILE:2 * HALO + TILE, :] = jnp.where(last, 0.0, vnext[0:HALO, :])

    kbuf[0:TILE, :] = kp_ref[...]
    kbuf[TILE:2 * TILE, :] = kc_ref[...]
    kbuf[2 * TILE:3 * TILE, :] = kn_ref[...]
    vbuf[0:TILE, :] = vp_ref[...]
    vbuf[TILE:2 * TILE, :] = vc_ref[...]
    vbuf[2 * TILE:3 * TILE, :] = vn_ref[...]

    width = 2 * HEAD_DIM
    lane = lax.broadcasted_iota(i32, (1, width), 1)
    low = lane < HEAD_DIM
    rows = nt * ROWS_PER_TILE
    qscale = jnp.asarray(HEAD_DIM ** -0.5, bf16)
    eye = (lax.broadcasted_iota(i32, (width, width), 0)
           == lax.broadcasted_iota(i32, (width, width), 1)).astype(bf16)
    ones = jnp.ones((WIN_ROWS * GRID_W, width), bf16)

    def window(j):
        r = pos * ROWS_PER_TILE + j
        rs = jnp.clip(r - WIN_ROWS // 2, 0, rows - WIN_ROWS)
        koff = pl.multiple_of((rs - pos * ROWS_PER_TILE + ROWS_PER_TILE) * GRID_W, GRID_W)
        return r - rs, koff

    items = [(j, p) for j in range(ROWS_PER_TILE) for p in range(N_PAIRS)]

    def scores(n):
        j, p = items[n]
        si, koff = window(j)
        cols = slice(p * width, (p + 1) * width)
        qp = q_ref[j * GRID_W:(j + 1) * GRID_W, cols] * qscale
        zero = jnp.zeros_like(qp)
        qs = jnp.concatenate([jnp.where(low, qp, zero), jnp.where(low, zero, qp)], axis=0)
        lhs = jnp.concatenate([qs, eye], axis=1)
        rhs = jnp.concatenate([kbuf[pl.ds(koff, WIN_ROWS * GRID_W), cols], bias_ref[si, p]], axis=1)
        sc_ref[n] = lax.dot_general(lhs, rhs, (((1,), (1,)), ((), ())), preferred_element_type=f32)

    def exponentials(n):
        sc = sc_ref[n]
        e_ref[n] = jnp.exp(sc - jnp.max(sc, axis=-1, keepdims=True)).astype(bf16)

    def weighted_values(n):
        j, p = items[n]
        _, koff = window(j)
        cols = slice(p * width, (p + 1) * width)
        rhs = jnp.concatenate([vbuf[pl.ds(koff, WIN_ROWS * GRID_W), cols], ones], axis=1)
        oe = jnp.dot(e_ref[n], rhs, preferred_element_type=f32)
        o = oe[:, 0:width] / oe[:, width:2 * width]
        pb_ref[j * GRID_W:(j + 1) * GRID_W, cols] = jnp.where(low, o[0:GRID_W], o[GRID_W:2 * GRID_W]).astype(bf16)

    for stage in (scores, exponentials, weighted_values):
        for n in range(len(items)):
            stage(n)

    t = pos * TILE + lax.broadcasted_iota(i32, (TILE, 1), 0)
    pa_parts = []
    for g, w in enumerate(POOL_WINDOWS):
        cols = slice(g * POOL_GROUP, (g + 1) * POOL_GROUP)
        half = w // 2
        acc = uext[HALO - half:HALO - half + TILE, cols]
        for d in range(-half + 1, half):
            acc = acc + uext[HALO + d:HALO + d + TILE, cols]
        cnt = (jnp.minimum(t + half, seq_len) - jnp.maximum(t - half, 0)).astype(f32)
        m = acc / cnt - uext[HALO:HALO + TILE, cols]
        z = jnp.dot(m.astype(bf16), poolw_ref[g], preferred_element_type=f32)
        pa_parts.append(z * pscale_ref[:, cols])
    pa = jnp.concatenate(pa_parts, axis=1).astype(bf16)

    y = (convw_ref[0:1, :] * vext[HALO - 1:HALO - 1 + TILE, :]
         + convw_ref[1:2, :] * vext[HALO:HALO + TILE, :]
         + convw_ref[2:3, :] * vext[HALO + 1:HALO + 1 + TILE, :])
    pc = (b_ref[...].astype(f32) * y).astype(bf16)

    branches = (pa, pb_ref[...], pc)
    merged = None
    for n in range(N_BRANCH):
        proj = jnp.dot(branches[n], wb_ref[n], preferred_element_type=f32)
        gate = jax.nn.sigmoid(gl_ref[:, n * D_MODEL:(n + 1) * D_MODEL].astype(f32))
        merged = gate * proj if merged is None else merged + gate * proj
    mix = jnp.dot(merged.astype(bf16), wout_ref[...], preferred_element_type=f32)

    mod = ada_ref[pl.ds(s, 1), :]
    gt1 = mod[:, 2 * D_MODEL:3 * D_MODEL]
    sh2 = mod[:, 3 * D_MODEL:4 * D_MODEL]
    sc2 = mod[:, 4 * D_MODEL:5 * D_MODEL]
    xn = x + gt1 * _rms(mix, ng_ref[1:2, :])
    xo_ref[...] = xn
    h2 = _rms(xn, ng_ref[2:3, :]) * (1.0 + sc2) + sh2
    _to_token_tiles(h2_ref, h2)
    h_hi = h2.astype(bf16)
    h_lo = (h2 - h_hi.astype(f32)).astype(bf16)
    logits = (jnp.dot(h_hi, rwh_ref[...], preferred_element_type=f32)
              + jnp.dot(h_lo, rwh_ref[...], preferred_element_type=f32)
              + jnp.dot(h_hi, rwl_ref[...], preferred_element_type=f32)
              + rb_ref[...])
    lg_ref[...] = logits.T[0:N_EXPERTS, :]


def _mixer(x_parts, zs, ada_l, ng, poolw_bf, pscale, bias_tab, convw, wb_bf, wout_bf, rw_hi, rw_lo, rb,
           seq_lens):
    n = sum(p.shape[0] for p in x_parts)
    zu, zq, zk, zv, zb, zc, zx, zg = zs
    x_specs, first_b = _row_specs(x_parts, TILE)
    seq, pos, cnt = _tile_meta(seq_lens, TILE)
    per = TILE // GRID_W

    cur = lambda i, sq, ps, ct: (i, 0)
    prev_t = lambda i, sq, ps, ct: (jnp.where(ps[i] > 0, i - 1, i), 0)
    next_t = lambda i, sq, ps, ct: (jnp.where(ps[i] < ct[i] - 1, i + 1, i), 0)
    prev_h = lambda i, sq, ps, ct: (jnp.where(ps[i] > 0, i * per - 1, i * per), 0)
    next_h = lambda i, sq, ps, ct: (jnp.where(ps[i] < ct[i] - 1, (i + 1) * per, i * per), 0)
    c2 = lambda i, sq, ps, ct: (0, 0)
    c3 = lambda i, sq, ps, ct: (0, 0, 0)
    c4 = lambda i, sq, ps, ct: (0, 0, 0, 0)

    tile512 = lambda im: pl.BlockSpec((TILE, ATTN_WIDTH), im)
    halo512 = lambda im: pl.BlockSpec((GRID_W, ATTN_WIDTH), im)
    once = pl.Buffered(1)
    in_specs = x_specs + [
        tile512(cur), halo512(prev_h), halo512(next_h),
        tile512(cur),
        tile512(prev_t), tile512(cur), tile512(next_t),
        tile512(prev_t), tile512(cur), tile512(next_t),
        tile512(cur),
        tile512(cur), halo512(prev_h), halo512(next_h),
        tile512(cur), halo512(prev_h), halo512(next_h),
        pl.BlockSpec((TILE, N_BRANCH * D_MODEL), cur),
        pl.BlockSpec((SEQ_PAD, N_MOD * D_MODEL), c2),
        pl.BlockSpec((4, D_MODEL), c2),
        pl.BlockSpec((4, POOL_GROUP, POOL_GROUP), c3),
        pl.BlockSpec((1, POOL_WIDTH), c2),
        pl.BlockSpec((WIN_ROWS, N_PAIRS, WIN_ROWS * GRID_W, 2 * GRID_W), c4, pipeline_mode=once),
        pl.BlockSpec((3, CONV_WIDTH), c2),
        pl.BlockSpec((N_BRANCH, POOL_WIDTH, D_MODEL), c3, pipeline_mode=once),
        pl.BlockSpec((D_MODEL, D_MODEL), c2, pipeline_mode=once),
        pl.BlockSpec((D_MODEL, LANES), c2),
        pl.BlockSpec((D_MODEL, LANES), c2),
        pl.BlockSpec((1, LANES), c2),
    ]
    out_specs = [
        pl.BlockSpec((TILE, D_MODEL), cur),
        pl.BlockSpec((TILE * SUBLANES, LANES), cur),
        pl.BlockSpec((N_EXPERTS, TILE), lambda i, sq, ps, ct: (0, i)),
    ]
    grid_spec = pltpu.PrefetchScalarGridSpec(
        num_scalar_prefetch=3,
        grid=(n // TILE,),
        in_specs=in_specs,
        out_specs=out_specs,
        scratch_shapes=[pltpu.VMEM((TILE + 2 * HALO, POOL_WIDTH), f32),
                        pltpu.VMEM((TILE + 2 * HALO, CONV_WIDTH), f32),
                        pltpu.VMEM((3 * TILE, ATTN_WIDTH), bf16),
                        pltpu.VMEM((3 * TILE, ATTN_WIDTH), bf16),
                        pltpu.VMEM((TILE, ATTN_WIDTH), bf16),
                        pltpu.VMEM((ROWS_PER_TILE * N_PAIRS, 2 * GRID_W, WIN_ROWS * GRID_W), f32),
                        pltpu.VMEM((ROWS_PER_TILE * N_PAIRS, 2 * GRID_W, WIN_ROWS * GRID_W), bf16)],
    )
    return pl.pallas_call(
        functools.partial(_mixer_kernel, len(x_parts), first_b),
        grid_spec=grid_spec,
        out_shape=[jax.ShapeDtypeStruct((n, D_MODEL), f32),
                   jax.ShapeDtypeStruct((n * SUBLANES, LANES), f32),
                   jax.ShapeDtypeStruct((N_EXPERTS, n), f32)],
        compiler_params=_cparams(("arbitrary",)),
        name="mixer",
    )(seq, pos, cnt,
      *x_parts, zu, zu, zu, zq, zk, zk, zk, zv, zv, zv, zb, zc, zc, zc, zx, zx, zx, zg,
      ada_l, ng, poolw_bf, pscale, bias_tab, convw, wb_bf, wout_bf, rw_hi, rw_lo, rb)


def _bias_table(rpb_l):
    c = np.arange(GRID_W)[:, None]
    kc = np.arange(GRID_W)[None, :]
    cs = np.clip(c - WIN_COLS // 2, 0, GRID_W - WIN_COLS)
    valid = (kc >= cs) & (kc < cs + WIN_COLS)
    didx = np.clip(kc - c + WIN_COLS - 1, 0, 2 * WIN_COLS - 2)
    select = (didx[:, :, None] == np.arange(2 * WIN_COLS - 1)).astype(np.float32)
    dense = jnp.einsum('hrd,ckd->hrck', rpb_l, jnp.asarray(select), precision=lax.Precision.HIGHEST)
    dense = jnp.where(valid[None, None], dense, NEG_BIG)
    slabs = [dense[:, WIN_ROWS - 1 - s:2 * WIN_ROWS - 1 - s] for s in range(WIN_ROWS)]
    tab = jnp.stack(slabs, axis=0)
    tab = tab.transpose(0, 1, 2, 4, 3)
    tab = tab.reshape(WIN_ROWS, N_PAIRS, 2, WIN_ROWS * GRID_W, GRID_W).transpose(0, 1, 3, 2, 4)
    return tab.reshape(WIN_ROWS, N_PAIRS, WIN_ROWS * GRID_W, 2 * GRID_W).astype(bf16)


def _router_kernel(lg_ref, lp_ref, gt_ref, tc_ref):
    l = lg_ref[...]
    eio = lax.broadcasted_iota(i32, l.shape, 0)
    vals, hots = [], []
    for k in range(TOP_K):
        m = jnp.max(l, axis=0, keepdims=True)
        idx = jnp.min(jnp.where(l == m, eio, N_EXPERTS), axis=0, keepdims=True)
        hot = eio == idx
        l = jnp.where(hot, -jnp.inf, l)
        vals.append(m)
        hots.append(hot)
    ex = [jnp.exp(v - vals[0]) for v in vals]
    den = ex[0] + ex[1] + ex[2] + ex[3]
    for k in range(TOP_K):
        gt_ref[k:k + 1, :] = ex[k] / den

    hot_all = jnp.zeros(l.shape, f32)
    for hot in hots:
        hot_all = hot_all + hot.astype(f32)
    rr = lax.broadcasted_iota(i32, (SORT_TILE, SORT_TILE), 0)
    cc = lax.broadcasted_iota(i32, (SORT_TILE, SORT_TILE), 1)
    tri = (rr <= cc).astype(bf16)
    pref = jnp.dot(hot_all.astype(bf16), tri, preferred_element_type=f32)
    cnt = jnp.broadcast_to(pref[:, SORT_TILE - 1:SORT_TILE], (N_EXPERTS, LANES))
    padded = jnp.floor((cnt + (CHUNK - 1)) * (1.0 / CHUNK)) * CHUNK
    er = lax.broadcasted_iota(i32, (N_EXPERTS, N_EXPERTS), 0)
    ec = lax.broadcasted_iota(i32, (N_EXPERTS, N_EXPERTS), 1)
    before = (ec < er).astype(f32)
    start = jnp.dot(before, padded, preferred_element_type=f32, precision=lax.Precision.HIGHEST)
    tot = start[:, 0:1] + pref - 1.0
    for k in range(TOP_K):
        row = jnp.sum(jnp.where(hots[k], tot, 0.0), axis=0, keepdims=True).astype(i32)
        lp_ref[k:k + 1, :] = row * SUBLANES
    tc_ref[...] = cnt.astype(i32)


def _router(logits_t):
    n = logits_t.shape[1]
    blk = lambda i: (0, i)
    return pl.pallas_call(
        _router_kernel,
        grid=(n // SORT_TILE,),
        in_specs=[pl.BlockSpec((N_EXPERTS, SORT_TILE), blk)],
        out_specs=[pl.BlockSpec((None, TOP_K, SORT_TILE), lambda i: (i, 0, 0)),
                   pl.BlockSpec((None, TOP_K, SORT_TILE), lambda i: (i, 0, 0)),
                   pl.BlockSpec((None, N_EXPERTS, LANES), lambda i: (i, 0, 0))],
        out_shape=[jax.ShapeDtypeStruct((n // SORT_TILE, TOP_K, SORT_TILE), i32),
                   jax.ShapeDtypeStruct((n // SORT_TILE, TOP_K, SORT_TILE), f32),
                   jax.ShapeDtypeStruct((n // SORT_TILE, N_EXPERTS, LANES), i32)],
        compiler_params=_cparams(("arbitrary",)),
        name="router",
    )(logits_t)


def _to_token_tiles(dst_ref, val):
    rows = val.shape[0]
    for s in range(SUBLANES):
        dst_ref[pl.ds(s, rows, stride=SUBLANES), :] = val[:, s * LANES:(s + 1) * LANES]


def _from_token_tiles(src_ref, rows):
    return jnp.concatenate([src_ref[pl.ds(s, rows, stride=SUBLANES), :] for s in range(SUBLANES)], axis=1)


def _sort_kernel(lp_ref, h_ref, o_ref):
    o_ref[...] = jnp.zeros_like(o_ref)

    def body(t, carry):
        v = h_ref[pl.ds(pl.multiple_of(t * SUBLANES, SUBLANES), SUBLANES), :]
        for k in range(TOP_K):
            o_ref[pl.ds(pl.multiple_of(lp_ref[k * SORT_TILE + t], SUBLANES), SUBLANES), :] = v
        return carry

    lax.fori_loop(0, SORT_TILE, body, 0, unroll=8)


def _local_sort(h2, lrow):
    nt = h2.shape[0] // (SORT_TILE * SUBLANES)
    return pl.pallas_call(
        _sort_kernel,
        grid=(nt,),
        in_specs=[pl.BlockSpec((TOP_K * SORT_TILE,), lambda i: (i,), memory_space=pltpu.SMEM),
                  pl.BlockSpec((SORT_TILE * SUBLANES, LANES), lambda i: (i, 0))],
        out_specs=pl.BlockSpec((LOCAL_ROWS * SUBLANES, LANES), lambda i: (i, 0)),
        out_shape=jax.ShapeDtypeStruct((nt * LOCAL_ROWS * SUBLANES, LANES), f32),
        compiler_params=_cparams(("arbitrary",)),
        name="local_sort",
    )(lrow, h2)


def _expert_kernel(be_ref, nu_ref, src_ref, srcn_ref, dst_ref, xs_ref, wgu_ref, bgu_ref, wd_ref, bd_ref,
                   ys_ref, xbuf, ybuf, wgu_bf, wd_bf, act, sem_in, sem_out):
    b = pl.program_id(0)
    nb = pl.num_programs(0)
    nu = nu_ref[0]
    slot = b % 2
    e = be_ref[b]
    prev = be_ref[jnp.maximum(b - 1, 0)]
    chunk_rows = CHUNK * SUBLANES

    def gather_copy(idx_ref, c, s):
        src = pl.multiple_of(idx_ref[0, 0, c] * SUBLANES, chunk_rows)
        return pltpu.make_async_copy(xs_ref.at[pl.ds(src, chunk_rows), :],
                                     xbuf.at[s, pl.ds(c * chunk_rows, chunk_rows), :], sem_in.at[s])

    def scatter_copy(c, s):
        dst = pl.multiple_of(dst_ref[0, 0, c] * SUBLANES, chunk_rows)
        return pltpu.make_async_copy(ybuf.at[s, pl.ds(c * chunk_rows, chunk_rows), :],
                                     ys_ref.at[pl.ds(dst, chunk_rows), :], sem_out.at[s])

    def start_gather(idx_ref, s):
        for c in range(BLOCK_CHUNKS):
            gather_copy(idx_ref, c, s).start()

    def wait_gather(s):
        for c in range(BLOCK_CHUNKS):
            gather_copy(src_ref, c, s).wait()

    def wait_scatter(s):
        for c in range(BLOCK_CHUNKS):
            scatter_copy(c, s).wait()

    @pl.when(b == 0)
    def _():
        start_gather(src_ref, 0)

    @pl.when(b + 1 < nu)
    def _():
        start_gather(srcn_ref, 1 - slot)

    @pl.when((b >= 2) & (b - 2 < nu))
    def _():
        wait_scatter(slot)

    @pl.when((b == 0) | (e != prev))
    def _():
        wgu_bf[...] = wgu_ref[...].astype(bf16)
        wd_bf[...] = wd_ref[...].astype(bf16)

    @pl.when(b < nu)
    def _():
        wait_gather(slot)
        x = _from_token_tiles(xbuf.at[slot], SLOT_BLOCK).astype(bf16)
        for c0 in range(0, D_EXPERT, COL_CHUNK):
            c1 = c0 + COL_CHUNK
            gate = jnp.dot(x, wgu_bf[:, c0:c1], preferred_element_type=f32) + bgu_ref[:, c0:c1]
            up = (jnp.dot(x, wgu_bf[:, D_EXPERT + c0:D_EXPERT + c1], preferred_element_type=f32)
                  + bgu_ref[:, D_EXPERT + c0:D_EXPERT + c1])
            gate = jnp.minimum(gate, SWIGLU_LIMIT)
            up = jnp.clip(up, -SWIGLU_LIMIT, SWIGLU_LIMIT)
            act[:, c0:c1] = ((up + 1.0) * (gate * jax.nn.sigmoid(gate * SWIGLU_ALPHA))).astype(bf16)
        y = jnp.dot(act[...], wd_bf[...], preferred_element_type=f32) + bd_ref[...]
        _to_token_tiles(ybuf.at[slot], y)
        for c in range(BLOCK_CHUNKS):
            scatter_copy(c, slot).start()

    @pl.when(b == nb - 1)
    def _():
        @pl.when((b >= 1) & (b - 1 < nu))
        def _():
            wait_scatter(1 - slot)

        @pl.when(b < nu)
        def _():
            wait_scatter(slot)


def _experts(xs, chunk_src, chunk_dst, blk_expert, n_used, layer, w_gu, b_gu, w_down, b_down):
    nblk = chunk_src.shape[0]
    depth = w_gu.shape[0]
    wsel = lambda b, be, nu: (layer, be[b], 0, 0)
    cur = lambda b, be, nu: (b, 0, 0)
    nxt = lambda b, be, nu: (jnp.minimum(b + 1, nblk - 1), 0, 0)
    idx_spec = lambda im: pl.BlockSpec((1, 1, BLOCK_CHUNKS), im, memory_space=pltpu.SMEM)
    grid_spec = pltpu.PrefetchScalarGridSpec(
        num_scalar_prefetch=2,
        grid=(nblk,),
        in_specs=[idx_spec(cur), idx_spec(nxt), idx_spec(cur),
                  pl.BlockSpec(memory_space=pl.ANY),
                  pl.BlockSpec((None, None, D_MODEL, 2 * D_EXPERT), wsel),
                  pl.BlockSpec((None, None, 1, 2 * D_EXPERT), wsel),
                  pl.BlockSpec((None, None, D_EXPERT, D_MODEL), wsel),
                  pl.BlockSpec((None, None, 1, D_MODEL), wsel)],
        out_specs=pl.BlockSpec(memory_space=pl.ANY),
        scratch_shapes=[pltpu.VMEM((2, SLOT_BLOCK * SUBLANES, LANES), f32),
                        pltpu.VMEM((2, SLOT_BLOCK * SUBLANES, LANES), f32),
                        pltpu.VMEM((D_MODEL, 2 * D_EXPERT), bf16),
                        pltpu.VMEM((D_EXPERT, D_MODEL), bf16),
                        pltpu.VMEM((SLOT_BLOCK, D_EXPERT), bf16),
                        pltpu.SemaphoreType.DMA((2,)),
                        pltpu.SemaphoreType.DMA((2,))],
    )
    return pl.pallas_call(
        _expert_kernel,
        grid_spec=grid_spec,
        out_shape=jax.ShapeDtypeStruct((xs.shape[0] + 2 * SLOT_BLOCK * SUBLANES, LANES), f32),
        compiler_params=_cparams(("arbitrary",)),
        name="experts",
    )(blk_expert, n_used, chunk_src, chunk_src, chunk_dst, xs, w_gu,
      b_gu.reshape(depth, N_EXPERTS, 1, 2 * D_EXPERT), w_down,
      b_down.reshape(depth, N_EXPERTS, 1, D_MODEL))


def _combine_kernel(first_b, seq_ref, lp_ref, gt_ref, x_ref, ys_ref, ada_ref, ng_ref, *refs):
    ff = refs[-1]

    def body(t, carry):
        acc = gt_ref[t] * ys_ref[pl.ds(pl.multiple_of(lp_ref[t], SUBLANES), SUBLANES), :]
        for k in range(1, TOP_K):
            acc = acc + (gt_ref[k * SORT_TILE + t]
                         * ys_ref[pl.ds(pl.multiple_of(lp_ref[k * SORT_TILE + t], SUBLANES), SUBLANES), :])
        ff[pl.ds(pl.multiple_of(t * SUBLANES, SUBLANES), SUBLANES), :] = acc
        return carry

    lax.fori_loop(0, SORT_TILE, body, 0, unroll=8)
    i = pl.program_id(0)
    s = seq_ref[i]
    gt2 = ada_ref[pl.ds(s, 1), 5 * D_MODEL:6 * D_MODEL]
    out = x_ref[...] + gt2 * _rms(_from_token_tiles(ff, SORT_TILE), ng_ref[3:4, :])
    if first_b is None:
        refs[0][...] = out
    else:
        @pl.when(i < first_b)
        def _():
            refs[0][...] = out

        @pl.when(i >= first_b)
        def _():
            refs[1][...] = out


def _combine(x, ys, lrow, gflat, ada_l, ng, seq_lens, split_rows=None):
    n = x.shape[0]
    if split_rows is None:
        first_b = None
        out_specs = pl.BlockSpec((SORT_TILE, D_MODEL), lambda i, sq: (i, 0))
        out_shape = jax.ShapeDtypeStruct((n, D_MODEL), f32)
    else:
        first_b = split_rows // SORT_TILE
        out_specs = [pl.BlockSpec((SORT_TILE, D_MODEL), lambda i, sq: (jnp.minimum(i, first_b - 1), 0)),
                     pl.BlockSpec((SORT_TILE, D_MODEL), lambda i, sq: (jnp.maximum(i - first_b, 0), 0))]
        out_shape = [jax.ShapeDtypeStruct((split_rows, D_MODEL), f32),
                     jax.ShapeDtypeStruct((n - split_rows, D_MODEL), f32)]
    seq, _, _ = _tile_meta(seq_lens, SORT_TILE)
    row = lambda i, sq: (i, 0)
    flat = lambda i, sq: (i,)
    const = lambda i, sq: (0, 0)
    grid_spec = pltpu.PrefetchScalarGridSpec(
        num_scalar_prefetch=1,
        grid=(n // SORT_TILE,),
        in_specs=[pl.BlockSpec((TOP_K * SORT_TILE,), flat, memory_space=pltpu.SMEM),
                  pl.BlockSpec((TOP_K * SORT_TILE,), flat, memory_space=pltpu.SMEM),
                  pl.BlockSpec((SORT_TILE, D_MODEL), row),
                  pl.BlockSpec((LOCAL_ROWS * SUBLANES, LANES), row),
                  pl.BlockSpec((SEQ_PAD, N_MOD * D_MODEL), const),
                  pl.BlockSpec((4, D_MODEL), const)],
        out_specs=out_specs,
        scratch_shapes=[pltpu.VMEM((SORT_TILE * SUBLANES, LANES), f32)],
    )
    return pl.pallas_call(
        functools.partial(_combine_kernel, first_b),
        grid_spec=grid_spec,
        out_shape=out_shape,
        compiler_params=_cparams(("arbitrary",)),
        name="combine",
    )(seq, lrow, gflat, x, ys, ada_l, ng)


def _chunk_plan(tile_counts):
    nt = tile_counts.shape[0]
    nch = (tile_counts + CHUNK - 1) // CHUNK
    rows = nch * CHUNK
    loff = jnp.cumsum(rows, axis=1) - rows
    cend = jnp.cumsum(nch, axis=0)
    cstart = cend - nch
    total = cend[-1]
    nblk_e = (total + BLOCK_CHUNKS - 1) // BLOCK_CHUNKS
    bend = jnp.cumsum(nblk_e)
    bstart = bend - nblk_e
    n_used = bend[-1].reshape(1)
    max_chunks = nt * (TOP_K * SORT_TILE // CHUNK + N_EXPERTS)
    nblk = max_chunks // BLOCK_CHUNKS + N_EXPERTS
    b = jnp.arange(nblk, dtype=i32)
    blk_expert = jnp.minimum(jnp.sum((b[:, None] >= bend[None, :]).astype(i32), axis=1), N_EXPERTS - 1)
    onehot = (blk_expert[:, None] == jnp.arange(N_EXPERTS, dtype=i32)).astype(f32)

    def of_block(per_expert):
        table = per_expert.astype(f32).reshape(-1, N_EXPERTS).T
        return jnp.dot(onehot, table, precision=lax.Precision.HIGHEST).astype(i32)

    cend_b, cstart_b, loff_b = of_block(cend), of_block(cstart), of_block(loff)
    total_b, bstart_b = of_block(total)[:, 0], of_block(bstart)[:, 0]
    q = (b - bstart_b)[:, None] * BLOCK_CHUNKS + jnp.arange(BLOCK_CHUNKS, dtype=i32)[None, :]
    valid = (q < total_b[:, None]) & (b < n_used[0])[:, None]
    tile = jnp.minimum(jnp.sum((cend_b[:, None, :] <= q[:, :, None]).astype(i32), axis=2), nt - 1)
    pick = tile[:, :, None] == jnp.arange(nt, dtype=i32)
    cs = jnp.sum(jnp.where(pick, cstart_b[:, None, :], 0), axis=2)
    lo = jnp.sum(jnp.where(pick, loff_b[:, None, :], 0), axis=2)
    row = tile * LOCAL_ROWS + lo + (q - cs) * CHUNK
    spare = (nt * LOCAL_ROWS + (b % 2)[:, None] * SLOT_BLOCK
             + jnp.arange(BLOCK_CHUNKS, dtype=i32)[None, :] * CHUNK)
    shape = (nblk, 1, BLOCK_CHUNKS)
    chunk_src = jnp.where(valid, row, 0).astype(i32).reshape(shape)
    chunk_dst = jnp.where(valid, row, spare).astype(i32).reshape(shape)
    return chunk_src, chunk_dst, blk_expert.astype(i32), n_used.astype(i32)


def _trunk(x_parts, c_all, seq_lens, ada_w, ada_b, norm_g, w_in, pool_w, pool_scale, rpb, conv_w,
           w_branch, w_out, router_w, router_b, expert_w_gu, expert_b_gu, expert_w_down, expert_b_down):
    depth = ada_w.shape[0]
    split_rows = x_parts[0].shape[0]
    ada = _ada(c_all, ada_w, ada_b)
    x = tuple(x_parts)
    for l in range(depth):
        ada_l = ada[l]
        ng = norm_g[l]
        zs = _inproj(x, ada_l, ng[0:1], w_in[l].astype(bf16), seq_lens)
        rw = jnp.pad(router_w[l], ((0, 0), (0, LANES - N_EXPERTS)))
        rw_hi = rw.astype(bf16)
        rw_lo = (rw - rw_hi.astype(f32)).astype(bf16)
        x_mid, h2, logits_t = _mixer(
            x, zs, ada_l, ng, pool_w[l].astype(bf16), pool_scale[l].reshape(1, POOL_WIDTH),
            _bias_table(rpb[l]), conv_w[l], w_branch[l].astype(bf16), w_out[l].astype(bf16),
            rw_hi, rw_lo,
            jnp.pad(router_b[l], (0, LANES - N_EXPERTS)).reshape(1, LANES), seq_lens)
        lpos, gates, tile_counts = _router(logits_t)
        chunk_src, chunk_dst, blk_expert, n_used = _chunk_plan(tile_counts[:, :, 0])
        lrow = lpos.reshape(-1)
        xs = _local_sort(h2, lrow)
        ys = _experts(xs, chunk_src, chunk_dst, blk_expert, n_used, l, expert_w_gu, expert_b_gu,
                      expert_w_down, expert_b_down)
        split = split_rows if l == depth - 1 else None
        out = _combine(x_mid, ys, lrow, gates.reshape(-1), ada_l, ng, seq_lens, split)
        x = (out,) if split is None else tuple(out)
    return x


def kernel(x_prompt, x_sample, c_prompt, c_sample, ada_w, ada_b, norm_g, w_in, pool_w, pool_scale, rpb,
           conv_w, w_branch, w_out, router_w, router_b, expert_w_gu, expert_b_gu, expert_w_down,
           expert_b_down):
    bp, lp, d = x_prompt.shape
    bs, ls, _ = x_sample.shape
    assert d == D_MODEL and bp + bs <= SEQ_PAD
    assert lp % IN_TILE == 0 and ls % IN_TILE == 0 and lp >= 2 * TILE and ls >= 2 * TILE
    x_parts = (x_prompt.reshape(bp * lp, d), x_sample.reshape(bs * ls, d))
    c_all = jnp.concatenate([c_prompt, c_sample, jnp.zeros((SEQ_PAD - bp - bs, d), f32)], axis=0)
    seq_lens = ((bp, lp), (bs, ls))
    yp, ys = _trunk(x_parts, c_all, seq_lens, ada_w, ada_b, norm_g, w_in, pool_w, pool_scale, rpb, conv_w,
                    w_branch, w_out, router_w, router_b, expert_w_gu, expert_b_gu, expert_w_down,
                    expert_b_down)
    return (yp.reshape(bp, lp, d), ys.reshape(bs, ls, d))
```

```python
import functools

import numpy as np
import jax
import jax.numpy as jnp
from jax import lax
from jax.experimental import pallas as pl
from jax.experimental.pallas import tpu as pltpu

f32 = jnp.float32
bf16 = jnp.bfloat16
i32 = jnp.int32

D_MODEL = 1024
GRID_W = 64
POOL_WINDOWS = (2, 4, 8, 16)
POOL_GROUP = 128
POOL_WIDTH = 512
N_HEADS = 8
HEAD_DIM = 64
N_PAIRS = N_HEADS // 2
ATTN_WIDTH = 512
WIN_ROWS = 8
WIN_COLS = 16
CONV_WIDTH = 512
N_BRANCH = 3
IN_COLS = POOL_WIDTH + 3 * ATTN_WIDTH + 3 * CONV_WIDTH + N_BRANCH * D_MODEL
N_EXPERTS = 32
TOP_K = 4
D_EXPERT = D_MODEL
SWIGLU_ALPHA = 1.702
SWIGLU_LIMIT = 7.0
RMS_EPS = 1e-6
N_MOD = 6
NEG_BIG = -1e30

LANES = 128
SUBLANES = 8
COL_CHUNK = 512
SEQ_PAD = 16
ROWS_PER_TILE = 4
TILE = ROWS_PER_TILE * GRID_W
HALO = 8
IN_TILE = 512
SORT_TILE = 512
CHUNK = 8
SLOT_BLOCK = 1024
BLOCK_CHUNKS = SLOT_BLOCK // CHUNK
LOCAL_ROWS = TOP_K * SORT_TILE + N_EXPERTS * CHUNK
VMEM_LIMIT = 56 * 1024 * 1024

assert SUBLANES * LANES == D_MODEL
assert SORT_TILE % TILE == 0 and SLOT_BLOCK % CHUNK == 0 and CHUNK % SUBLANES == 0


def _cparams(sem):
    return pltpu.CompilerParams(dimension_semantics=sem, vmem_limit_bytes=VMEM_LIMIT)


def _tile_meta(seq_lens, tile):
    seq, pos, cnt = [], [], []
    sid = 0
    for nseq, length in seq_lens:
        nt = length // tile
        for _ in range(nseq):
            for p in range(nt):
                seq.append(sid)
                pos.append(p)
                cnt.append(nt)
            sid += 1
    return (jnp.asarray(np.array(seq, np.int32)), jnp.asarray(np.array(pos, np.int32)),
            jnp.asarray(np.array(cnt, np.int32)))


def _rms(x, g):
    ms = jnp.mean(x * x, axis=-1, keepdims=True)
    return x * lax.rsqrt(ms + RMS_EPS) * g


def _ada_kernel(c_ref, w_ref, b_ref, o_ref):
    c = c_ref[...]
    s = c * jax.nn.sigmoid(c)
    o_ref[...] = jnp.dot(s, w_ref[...], preferred_element_type=f32,
                         precision=lax.Precision.HIGHEST) + b_ref[...]


def _ada(c_all, ada_w, ada_b):
    depth = ada_w.shape[0]
    return pl.pallas_call(
        _ada_kernel,
        grid=(depth, N_MOD),
        in_specs=[pl.BlockSpec((SEQ_PAD, D_MODEL), lambda l, j: (0, 0)),
                  pl.BlockSpec((None, D_MODEL, D_MODEL), lambda l, j: (l, 0, j)),
                  pl.BlockSpec((None, 1, D_MODEL), lambda l, j: (l, 0, j))],
        out_specs=pl.BlockSpec((None, SEQ_PAD, D_MODEL), lambda l, j: (l, 0, j)),
        out_shape=jax.ShapeDtypeStruct((depth, SEQ_PAD, N_MOD * D_MODEL), f32),
        compiler_params=_cparams(("arbitrary", "arbitrary")),
        name="ada",
    )(c_all, ada_w, ada_b.reshape(depth, 1, N_MOD * D_MODEL))


_IN_SPLITS = (POOL_WIDTH, ATTN_WIDTH, ATTN_WIDTH, ATTN_WIDTH, CONV_WIDTH, CONV_WIDTH, CONV_WIDTH,
              N_BRANCH * D_MODEL)


def _pick_rows(x_refs, i, first_b):
    if len(x_refs) == 1:
        return x_refs[0][...]
    return jnp.where(i < first_b, x_refs[0][...], x_refs[1][...])


def _row_specs(x_parts, tile):
    if len(x_parts) == 1:
        return [pl.BlockSpec((tile, D_MODEL), lambda i, *_: (i, 0))], None
    first_b = x_parts[0].shape[0] // tile
    return [pl.BlockSpec((tile, D_MODEL), lambda i, *_: (jnp.minimum(i, first_b - 1), 0)),
            pl.BlockSpec((tile, D_MODEL), lambda i, *_: (jnp.maximum(i - first_b, 0), 0))], first_b


def _inproj_kernel(n_x, first_b, seq_ref, *refs):
    x_refs, (ada_ref, g_ref, w_ref), out_refs = refs[:n_x], refs[n_x:n_x + 3], refs[n_x + 3:]
    i = pl.program_id(0)
    s = seq_ref[i]
    mod = ada_ref[pl.ds(s, 1), :]
    x = _pick_rows(x_refs, i, first_b)
    h = _rms(x, g_ref[...]) * (1.0 + mod[:, D_MODEL:2 * D_MODEL]) + mod[:, 0:D_MODEL]
    h = h.astype(bf16)
    col = 0
    for o_ref in out_refs:
        width = o_ref.shape[1]
        for c0 in range(0, width, COL_CHUNK):
            o_ref[:, c0:c0 + COL_CHUNK] = jnp.dot(h, w_ref[:, col + c0:col + c0 + COL_CHUNK],
                                            preferred_element_type=f32).astype(bf16)
        col += width


def _inproj(x_parts, ada_l, g, w_in_bf, seq_lens):
    n = sum(p.shape[0] for p in x_parts)
    seq, _, _ = _tile_meta(seq_lens, IN_TILE)
    row = lambda i, s: (i, 0)
    const = lambda i, s: (0, 0)
    x_specs, first_b = _row_specs(x_parts, IN_TILE)
    grid_spec = pltpu.PrefetchScalarGridSpec(
        num_scalar_prefetch=1,
        grid=(n // IN_TILE,),
        in_specs=x_specs + [
                  pl.BlockSpec((SEQ_PAD, N_MOD * D_MODEL), const),
                  pl.BlockSpec((1, D_MODEL), const),
                  pl.BlockSpec((D_MODEL, IN_COLS), const, pipeline_mode=pl.Buffered(1))],
        out_specs=[pl.BlockSpec((IN_TILE, w), row) for w in _IN_SPLITS],
    )
    return pl.pallas_call(
        functools.partial(_inproj_kernel, len(x_parts), first_b),
        grid_spec=grid_spec,
        out_shape=[jax.ShapeDtypeStruct((n, w), bf16) for w in _IN_SPLITS],
        compiler_params=_cparams(("arbitrary",)),
        name="inproj",
    )(seq, *x_parts, ada_l, g, w_in_bf)


def _mixer_kernel(n_x, first_b, seq_ref, pos_ref, nt_ref, *refs):
    _mixer_body(_pick_rows(refs[:n_x], pl.program_id(0), first_b), seq_ref, pos_ref, nt_ref, *refs[n_x:])


def _mixer_body(x, seq_ref, pos_ref, nt_ref,
                u_ref, up_ref, un_ref, q_ref, kp_ref, kc_ref, kn_ref, vp_ref, vc_ref, vn_ref,
                b_ref, c_ref, cp_ref, cn_ref, xc_ref, xcp_ref, xcn_ref, gl_ref,
                ada_ref, ng_ref, poolw_ref, pscale_ref, bias_ref, convw_ref, wb_ref, wout_ref,
                rwh_ref, rwl_ref, rb_ref,
                xo_ref, h2_ref, lg_ref,
                uext, vext, kbuf, vbuf, pb_ref, sc_ref, e_ref):
    i = pl.program_id(0)
    s = seq_ref[i]
    pos = pos_ref[i]
    nt = nt_ref[i]
    first = pos == 0
    last = pos == nt - 1
    seq_len = nt * TILE

    uext[HALO:HALO + TILE, :] = u_ref[...].astype(f32)
    uext[0:HALO, :] = jnp.where(first, 0.0, up_ref[...].astype(f32)[GRID_W - HALO:GRID_W, :])
    uext[HALO + TILE:2 * HALO + TILE, :] = jnp.where(last, 0.0, un_ref[...].astype(f32)[0:HALO, :])

    vext[HALO:HALO + TILE, :] = c_ref[...].astype(f32) * xc_ref[...].astype(f32)
    vprev = cp_ref[...].astype(f32) * xcp_ref[...].astype(f32)
    vnext = cn_ref[...].astype(f32) * xcn_ref[...].astype(f32)
    vext[0:HALO, :] = jnp.where(first, 0.0, vprev[GRID_W - HALO:GRID_W, :])
    vext[HALO + TILE:2 * HALO + TILE, :] = jnp.where(last, 0.0, vnext[0:HALO, :])

    kbuf[0:TILE, :] = kp_ref[...]
    kbuf[TILE:2 * TILE, :] = kc_ref[...]
    kbuf[2 * TILE:3 * TILE, :] = kn_ref[...]
    vbuf[0:TILE, :] = vp_ref[...]
    vbuf[TILE:2 * TILE, :] = vc_ref[...]
    vbuf[2 * TILE:3 * TILE, :] = vn_ref[...]

    width = 2 * HEAD_DIM
    lane = lax.broadcasted_iota(i32, (1, width), 1)
    low = lane < HEAD_DIM
    rows = nt * ROWS_PER_TILE
    qscale = jnp.asarray(HEAD_DIM ** -0.5, bf16)
    eye = (lax.broadcasted_iota(i32, (width, width), 0)
           == lax.broadcasted_iota(i32, (width, width), 1)).astype(bf16)
    ones = jnp.ones((WIN_ROWS * GRID_W, width), bf16)

    def window(j):
        r = pos * ROWS_PER_TILE + j
        rs = jnp.clip(r - WIN_ROWS // 2, 0, rows - WIN_ROWS)
        koff = pl.multiple_of((rs - pos * ROWS_PER_TILE + ROWS_PER_TILE) * GRID_W, GRID_W)
        return r - rs, koff

    items = [(j, p) for j in range(ROWS_PER_TILE) for p in range(N_PAIRS)]

    def scores(n):
        j, p = items[n]
        si, koff = window(j)
        cols = slice(p * width, (p + 1) * width)
        qp = q_ref[j * GRID_W:(j + 1) * GRID_W, cols] * qscale
        zero = jnp.zeros_like(qp)
        qs = jnp.concatenate([jnp.where(low, qp, zero), jnp.where(low, zero, qp)], axis=0)
        lhs = jnp.concatenate([qs, eye], axis=1)
        rhs = jnp.concatenate([kbuf[pl.ds(koff, WIN_ROWS * GRID_W), cols], bias_ref[si, p]], axis=1)
        sc_ref[n] = lax.dot_general(lhs, rhs, (((1,), (1,)), ((), ())), preferred_element_type=f32)

    def exponentials(n):
        sc = sc_ref[n]
        e_ref[n] = jnp.exp(sc - jnp.max(sc, axis=-1, keepdims=True)).astype(bf16)

    def weighted_values(n):
        j, p = items[n]
        _, koff = window(j)
        cols = slice(p * width, (p + 1) * width)
        rhs = jnp.concatenate([vbuf[pl.ds(koff, WIN_ROWS * GRID_W), cols], ones], axis=1)
        oe = jnp.dot(e_ref[n], rhs, preferred_element_type=f32)
        o = oe[:, 0:width] / oe[:, width:2 * width]
        pb_ref[j * GRID_W:(j + 1) * GRID_W, cols] = jnp.where(low, o[0:GRID_W], o[GRID_W:2 * GRID_W]).astype(bf16)

    for stage in (scores, exponentials, weighted_values):
        for n in range(len(items)):
            stage(n)

    t = pos * TILE + lax.broadcasted_iota(i32, (TILE, 1), 0)
    pa_parts = []
    for g, w in enumerate(POOL_WINDOWS):
        cols = slice(g * POOL_GROUP, (g + 1) * POOL_GROUP)
        half = w // 2
        acc = uext[HALO - half:HALO - half + TILE, cols]
        for d in range(-half + 1, half):
            acc = acc + uext[HALO + d:HALO + d + TILE, cols]
        cnt = (jnp.minimum(t + half, seq_len) - jnp.maximum(t - half, 0)).astype(f32)
        m = acc / cnt - uext[HALO:HALO + TILE, cols]
        z = jnp.dot(m.astype(bf16), poolw_ref[g], preferred_element_type=f32)
        pa_parts.append(z * pscale_ref[:, cols])
    pa = jnp.concatenate(pa_parts, axis=1).astype(bf16)

    y = (convw_ref[0:1, :] * vext[HALO - 1:HALO - 1 + TILE, :]
         + convw_ref[1:2, :] * vext[HALO:HALO + TILE, :]
         + convw_ref[2:3, :] * vext[HALO + 1:HALO + 1 + TILE, :])
    pc = (b_ref[...].astype(f32) * y).astype(bf16)

    branches = (pa, pb_ref[...], pc)
    merged = None
    for n in range(N_BRANCH):
        proj = jnp.dot(branches[n], wb_ref[n], preferred_element_type=f32)
        gate = jax.nn.sigmoid(gl_ref[:, n * D_MODEL:(n + 1) * D_MODEL].astype(f32))
        merged = gate * proj if merged is None else merged + gate * proj
    mix = jnp.dot(merged.astype(bf16), wout_ref[...], preferred_element_type=f32)

    mod = ada_ref[pl.ds(s, 1), :]
    gt1 = mod[:, 2 * D_MODEL:3 * D_MODEL]
    sh2 = mod[:, 3 * D_MODEL:4 * D_MODEL]
    sc2 = mod[:, 4 * D_MODEL:5 * D_MODEL]
    xn = x + gt1 * _rms(mix, ng_ref[1:2, :])
    xo_ref[...] = xn
    h2 = _rms(xn, ng_ref[2:3, :]) * (1.0 + sc2) + sh2
    _to_token_tiles(h2_ref, h2)
    h_hi = h2.astype(bf16)
    h_lo = (h2 - h_hi.astype(f32)).astype(bf16)
    logits = (jnp.dot(h_hi, rwh_ref[...], preferred_element_type=f32)
              + jnp.dot(h_lo, rwh_ref[...], preferred_element_type=f32)
              + jnp.dot(h_hi, rwl_ref[...], preferred_element_type=f32)
              + rb_ref[...])
    lg_ref[...] = logits.T[0:N_EXPERTS, :]


def _mixer(x_parts, zs, ada_l, ng, poolw_bf, pscale, bias_tab, convw, wb_bf, wout_bf, rw_hi, rw_lo, rb,
           seq_lens):
    n = sum(p.shape[0] for p in x_parts)
    zu, zq, zk, zv, zb, zc, zx, zg = zs
    x_specs, first_b = _row_specs(x_parts, TILE)
    seq, pos, cnt = _tile_meta(seq_lens, TILE)
    per = TILE // GRID_W

    cur = lambda i, sq, ps, ct: (i, 0)
    prev_t = lambda i, sq, ps, ct: (jnp.where(ps[i] > 0, i - 1, i), 0)
    next_t = lambda i, sq, ps, ct: (jnp.where(ps[i] < ct[i] - 1, i + 1, i), 0)
    prev_h = lambda i, sq, ps, ct: (jnp.where(ps[i] > 0, i * per - 1, i * per), 0)
    next_h = lambda i, sq, ps, ct: (jnp.where(ps[i] < ct[i] - 1, (i + 1) * per, i * per), 0)
    c2 = lambda i, sq, ps, ct: (0, 0)
    c3 = lambda i, sq, ps, ct: (0, 0, 0)
    c4 = lambda i, sq, ps, ct: (0, 0, 0, 0)

    tile512 = lambda im: pl.BlockSpec((TILE, ATTN_WIDTH), im)
    halo512 = lambda im: pl.BlockSpec((GRID_W, ATTN_WIDTH), im)
    once = pl.Buffered(1)
    in_specs = x_specs + [
        tile512(cur), halo512(prev_h), halo512(next_h),
        tile512(cur),
        tile512(prev_t), tile512(cur), tile512(next_t),
        tile512(prev_t), tile512(cur), tile512(next_t),
        tile512(cur),
        tile512(cur), halo512(prev_h), halo512(next_h),
        tile512(cur), halo512(prev_h), halo512(next_h),
        pl.BlockSpec((TILE, N_BRANCH * D_MODEL), cur),
        pl.BlockSpec((SEQ_PAD, N_MOD * D_MODEL), c2),
        pl.BlockSpec((4, D_MODEL), c2),
        pl.BlockSpec((4, POOL_GROUP, POOL_GROUP), c3),
        pl.BlockSpec((1, POOL_WIDTH), c2),
        pl.BlockSpec((WIN_ROWS, N_PAIRS, WIN_ROWS * GRID_W, 2 * GRID_W), c4, pipeline_mode=once),
        pl.BlockSpec((3, CONV_WIDTH), c2),
        pl.BlockSpec((N_BRANCH, POOL_WIDTH, D_MODEL), c3, pipeline_mode=once),
        pl.BlockSpec((D_MODEL, D_MODEL), c2, pipeline_mode=once),
        pl.BlockSpec((D_MODEL, LANES), c2),
        pl.BlockSpec((D_MODEL, LANES), c2),
        pl.BlockSpec((1, LANES), c2),
    ]
    out_specs = [
        pl.BlockSpec((TILE, D_MODEL), cur),
        pl.BlockSpec((TILE * SUBLANES, LANES), cur),
        pl.BlockSpec((N_EXPERTS, TILE), lambda i, sq, ps, ct: (0, i)),
    ]
    grid_spec = pltpu.PrefetchScalarGridSpec(
        num_scalar_prefetch=3,
        grid=(n // TILE,),
        in_specs=in_specs,
        out_specs=out_specs,
        scratch_shapes=[pltpu.VMEM((TILE + 2 * HALO, POOL_WIDTH), f32),
                        pltpu.VMEM((TILE + 2 * HALO, CONV_WIDTH), f32),
                        pltpu.VMEM((3 * TILE, ATTN_WIDTH), bf16),
                        pltpu.VMEM((3 * TILE, ATTN_WIDTH), bf16),
                        pltpu.VMEM((TILE, ATTN_WIDTH), bf16),
                        pltpu.VMEM((ROWS_PER_TILE * N_PAIRS, 2 * GRID_W, WIN_ROWS * GRID_W), f32),
                        pltpu.VMEM((ROWS_PER_TILE * N_PAIRS, 2 * GRID_W, WIN_ROWS * GRID_W), bf16)],
    )
    return pl.pallas_call(
        functools.partial(_mixer_kernel, len(x_parts), first_b),
        grid_spec=grid_spec,
        out_shape=[jax.ShapeDtypeStruct((n, D_MODEL), f32),
                   jax.ShapeDtypeStruct((n * SUBLANES, LANES), f32),
                   jax.ShapeDtypeStruct((N_EXPERTS, n), f32)],
        compiler_params=_cparams(("arbitrary",)),
        name="mixer",
    )(seq, pos, cnt,
      *x_parts, zu, zu, zu, zq, zk, zk, zk, zv, zv, zv, zb, zc, zc, zc, zx, zx, zx, zg,
      ada_l, ng, poolw_bf, pscale, bias_tab, convw, wb_bf, wout_bf, rw_hi, rw_lo, rb)


def _bias_table(rpb_l):
    c = np.arange(GRID_W)[:, None]
    kc = np.arange(GRID_W)[None, :]
    cs = np.clip(c - WIN_COLS // 2, 0, GRID_W - WIN_COLS)
    valid = (kc >= cs) & (kc < cs + WIN_COLS)
    didx = np.clip(kc - c + WIN_COLS - 1, 0, 2 * WIN_COLS - 2)
    select = (didx[:, :, None] == np.arange(2 * WIN_COLS - 1)).astype(np.float32)
    dense = jnp.einsum('hrd,ckd->hrck', rpb_l, jnp.asarray(select), precision=lax.Precision.HIGHEST)
    dense = jnp.where(valid[None, None], dense, NEG_BIG)
    slabs = [dense[:, WIN_ROWS - 1 - s:2 * WIN_ROWS - 1 - s] for s in range(WIN_ROWS)]
    tab = jnp.stack(slabs, axis=0)
    tab = tab.transpose(0, 1, 2, 4, 3)
    tab = tab.reshape(WIN_ROWS, N_PAIRS, 2, WIN_ROWS * GRID_W, GRID_W).transpose(0, 1, 3, 2, 4)
    return tab.reshape(WIN_ROWS, N_PAIRS, WIN_ROWS * GRID_W, 2 * GRID_W).astype(bf16)


def _router_kernel(lg_ref, lp_ref, gt_ref, tc_ref):
    l = lg_ref[...]
    eio = lax.broadcasted_iota(i32, l.shape, 0)
    vals, hots = [], []
    for k in range(TOP_K):
        m = jnp.max(l, axis=0, keepdims=True)
        idx = jnp.min(jnp.where(l == m, eio, N_EXPERTS), axis=0, keepdims=True)
        hot = eio == idx
        l = jnp.where(hot, -jnp.inf, l)
        vals.append(m)
        hots.append(hot)
    ex = [jnp.exp(v - vals[0]) for v in vals]
    den = ex[0] + ex[1] + ex[2] + ex[3]
    for k in range(TOP_K):
        gt_ref[k:k + 1, :] = ex[k] / den

    hot_all = jnp.zeros(l.shape, f32)
    for hot in hots:
        hot_all = hot_all + hot.astype(f32)
    rr = lax.broadcasted_iota(i32, (SORT_TILE, SORT_TILE), 0)
    cc = lax.broadcasted_iota(i32, (SORT_TILE, SORT_TILE), 1)
    tri = (rr <= cc).astype(bf16)
    pref = jnp.dot(hot_all.astype(bf16), tri, preferred_element_type=f32)
    cnt = jnp.broadcast_to(pref[:, SORT_TILE - 1:SORT_TILE], (N_EXPERTS, LANES))
    padded = jnp.floor((cnt + (CHUNK - 1)) * (1.0 / CHUNK)) * CHUNK
    er = lax.broadcasted_iota(i32, (N_EXPERTS, N_EXPERTS), 0)
    ec = lax.broadcasted_iota(i32, (N_EXPERTS, N_EXPERTS), 1)
    before = (ec < er).astype(f32)
    start = jnp.dot(before, padded, preferred_element_type=f32, precision=lax.Precision.HIGHEST)
    tot = start[:, 0:1] + pref - 1.0
    for k in range(TOP_K):
        row = jnp.sum(jnp.where(hots[k], tot, 0.0), axis=0, keepdims=True).astype(i32)
        lp_ref[k:k + 1, :] = row * SUBLANES
    tc_ref[...] = cnt.astype(i32)


def _router(logits_t):
    n = logits_t.shape[1]
    blk = lambda i: (0, i)
    return pl.pallas_call(
        _router_kernel,
        grid=(n // SORT_TILE,),
        in_specs=[pl.BlockSpec((N_EXPERTS, SORT_TILE), blk)],
        out_specs=[pl.BlockSpec((None, TOP_K, SORT_TILE), lambda i: (i, 0, 0)),
                   pl.BlockSpec((None, TOP_K, SORT_TILE), lambda i: (i, 0, 0)),
                   pl.BlockSpec((None, N_EXPERTS, LANES), lambda i: (i, 0, 0))],
        out_shape=[jax.ShapeDtypeStruct((n // SORT_TILE, TOP_K, SORT_TILE), i32),
                   jax.ShapeDtypeStruct((n // SORT_TILE, TOP_K, SORT_TILE), f32),
                   jax.ShapeDtypeStruct((n // SORT_TILE, N_EXPERTS, LANES), i32)],
        compiler_params=_cparams(("arbitrary",)),
        name="router",
    )(logits_t)


def _to_token_tiles(dst_ref, val):
    rows = val.shape[0]
    for s in range(SUBLANES):
        dst_ref[pl.ds(s, rows, stride=SUBLANES), :] = val[:, s * LANES:(s + 1) * LANES]


def _from_token_tiles(src_ref, rows):
    return jnp.concatenate([src_ref[pl.ds(s, rows, stride=SUBLANES), :] for s in range(SUBLANES)], axis=1)


def _sort_kernel(lp_ref, h_ref, o_ref):
    o_ref[...] = jnp.zeros_like(o_ref)

    def body(t, carry):
        v = h_ref[pl.ds(pl.multiple_of(t * SUBLANES, SUBLANES), SUBLANES), :]
        for k in range(TOP_K):
            o_ref[pl.ds(pl.multiple_of(lp_ref[k * SORT_TILE + t], SUBLANES), SUBLANES), :] = v
        return carry

    lax.fori_loop(0, SORT_TILE, body, 0, unroll=8)


def _local_sort(h2, lrow):
    nt = h2.shape[0] // (SORT_TILE * SUBLANES)
    return pl.pallas_call(
        _sort_kernel,
        grid=(nt,),
        in_specs=[pl.BlockSpec((TOP_K * SORT_TILE,), lambda i: (i,), memory_space=pltpu.SMEM),
                  pl.BlockSpec((SORT_TILE * SUBLANES, LANES), lambda i: (i, 0))],
        out_specs=pl.BlockSpec((LOCAL_ROWS * SUBLANES, LANES), lambda i: (i, 0)),
        out_shape=jax.ShapeDtypeStruct((nt * LOCAL_ROWS * SUBLANES, LANES), f32),
        compiler_params=_cparams(("arbitrary",)),
        name="local_sort",
    )(lrow, h2)


def _expert_kernel(be_ref, nu_ref, src_ref, srcn_ref, dst_ref, xs_ref, wgu_ref, bgu_ref, wd_ref, bd_ref,
                   ys_ref, xbuf, ybuf, wgu_bf, wd_bf, act, sem_in, sem_out):
    b = pl.program_id(0)
    nb = pl.num_programs(0)
    nu = nu_ref[0]
    slot = b % 2
    e = be_ref[b]
    prev = be_ref[jnp.maximum(b - 1, 0)]
    chunk_rows = CHUNK * SUBLANES

    def gather_copy(idx_ref, c, s):
        src = pl.multiple_of(idx_ref[0, 0, c] * SUBLANES, chunk_rows)
        return pltpu.make_async_copy(xs_ref.at[pl.ds(src, chunk_rows), :],
                                     xbuf.at[s, pl.ds(c * chunk_rows, chunk_rows), :], sem_in.at[s])

    def scatter_copy(c, s):
        dst = pl.multiple_of(dst_ref[0, 0, c] * SUBLANES, chunk_rows)
        return pltpu.make_async_copy(ybuf.at[s, pl.ds(c * chunk_rows, chunk_rows), :],
                                     ys_ref.at[pl.ds(dst, chunk_rows), :], sem_out.at[s])

    def start_gather(idx_ref, s):
        for c in range(BLOCK_CHUNKS):
            gather_copy(idx_ref, c, s).start()

    def wait_gather(s):
        for c in range(BLOCK_CHUNKS):
            gather_copy(src_ref, c, s).wait()

    def wait_scatter(s):
        for c in range(BLOCK_CHUNKS):
            scatter_copy(c, s).wait()

    @pl.when(b == 0)
    def _():
        start_gather(src_ref, 0)

    @pl.when(b + 1 < nu)
    def _():
        start_gather(srcn_ref, 1 - slot)

    @pl.when((b >= 2) & (b - 2 < nu))
    def _():
        wait_scatter(slot)

    @pl.when((b == 0) | (e != prev))
    def _():
        wgu_bf[...] = wgu_ref[...].astype(bf16)
        wd_bf[...] = wd_ref[...].astype(bf16)

    @pl.when(b < nu)
    def _():
        wait_gather(slot)
        x = _from_token_tiles(xbuf.at[slot], SLOT_BLOCK).astype(bf16)
        for c0 in range(0, D_EXPERT, COL_CHUNK):
            c1 = c0 + COL_CHUNK
            gate = jnp.dot(x, wgu_bf[:, c0:c1], preferred_element_type=f32) + bgu_ref[:, c0:c1]
            up = (jnp.dot(x, wgu_bf[:, D_EXPERT + c0:D_EXPERT + c1], preferred_element_type=f32)
                  + bgu_ref[:, D_EXPERT + c0:D_EXPERT + c1])
            gate = jnp.minimum(gate, SWIGLU_LIMIT)
            up = jnp.clip(up, -SWIGLU_LIMIT, SWIGLU_LIMIT)
            act[:, c0:c1] = ((up + 1.0) * (gate * jax.nn.sigmoid(gate * SWIGLU_ALPHA))).astype(bf16)
        y = jnp.dot(act[...], wd_bf[...], preferred_element_type=f32) + bd_ref[...]
        _to_token_tiles(ybuf.at[slot], y)
        for c in range(BLOCK_CHUNKS):
            scatter_copy(c, slot).start()

    @pl.when(b == nb - 1)
    def _():
        @pl.when((b >= 1) & (b - 1 < nu))
        def _():
            wait_scatter(1 - slot)

        @pl.when(b < nu)
        def _():
            wait_scatter(slot)


def _experts(xs, chunk_src, chunk_dst, blk_expert, n_used, layer, w_gu, b_gu, w_down, b_down):
    nblk = chunk_src.shape[0]
    depth = w_gu.shape[0]
    wsel = lambda b, be, nu: (layer, be[b], 0, 0)
    cur = lambda b, be, nu: (b, 0, 0)
    nxt = lambda b, be, nu: (jnp.minimum(b + 1, nblk - 1), 0, 0)
    idx_spec = lambda im: pl.BlockSpec((1, 1, BLOCK_CHUNKS), im, memory_space=pltpu.SMEM)
    grid_spec = pltpu.PrefetchScalarGridSpec(
        num_scalar_prefetch=2,
        grid=(nblk,),
        in_specs=[idx_spec(cur), idx_spec(nxt), idx_spec(cur),
                  pl.BlockSpec(memory_space=pl.ANY),
                  pl.BlockSpec((None, None, D_MODEL, 2 * D_EXPERT), wsel),
                  pl.BlockSpec((None, None, 1, 2 * D_EXPERT), wsel),
                  pl.BlockSpec((None, None, D_EXPERT, D_MODEL), wsel),
                  pl.BlockSpec((None, None, 1, D_MODEL), wsel)],
        out_specs=pl.BlockSpec(memory_space=pl.ANY),
        scratch_shapes=[pltpu.VMEM((2, SLOT_BLOCK * SUBLANES, LANES), f32),
                        pltpu.VMEM((2, SLOT_BLOCK * SUBLANES, LANES), f32),
                        pltpu.VMEM((D_MODEL, 2 * D_EXPERT), bf16),
                        pltpu.VMEM((D_EXPERT, D_MODEL), bf16),
                        pltpu.VMEM((SLOT_BLOCK, D_EXPERT), bf16),
                        pltpu.SemaphoreType.DMA((2,)),
                        pltpu.SemaphoreType.DMA((2,))],
    )
    return pl.pallas_call(
        _expert_kernel,
        grid_spec=grid_spec,
        out_shape=jax.ShapeDtypeStruct((xs.shape[0] + 2 * SLOT_BLOCK * SUBLANES, LANES), f32),
        compiler_params=_cparams(("arbitrary",)),
        name="experts",
    )(blk_expert, n_used, chunk_src, chunk_src, chunk_dst, xs, w_gu,
      b_gu.reshape(depth, N_EXPERTS, 1, 2 * D_EXPERT), w_down,
      b_down.reshape(depth, N_EXPERTS, 1, D_MODEL))


def _combine_kernel(first_b, seq_ref, lp_ref, gt_ref, x_ref, ys_ref, ada_ref, ng_ref, *refs):
    ff = refs[-1]

    def body(t, carry):
        acc = gt_ref[t] * ys_ref[pl.ds(pl.multiple_of(lp_ref[t], SUBLANES), SUBLANES), :]
        for k in range(1, TOP_K):
            acc = acc + (gt_ref[k * SORT_TILE + t]
                         * ys_ref[pl.ds(pl.multiple_of(lp_ref[k * SORT_TILE + t], SUBLANES), SUBLANES), :])
        ff[pl.ds(pl.multiple_of(t * SUBLANES, SUBLANES), SUBLANES), :] = acc
        return carry

    lax.fori_loop(0, SORT_TILE, body, 0, unroll=8)
    i = pl.program_id(0)
    s = seq_ref[i]
    gt2 = ada_ref[pl.ds(s, 1), 5 * D_MODEL:6 * D_MODEL]
    out = x_ref[...] + gt2 * _rms(_from_token_tiles(ff, SORT_TILE), ng_ref[3:4, :])
    if first_b is None:
        refs[0][...] = out
    else:
        @pl.when(i < first_b)
        def _():
            refs[0][...] = out

        @pl.when(i >= first_b)
        def _():
            refs[1][...] = out


def _combine(x, ys, lrow, gflat, ada_l, ng, seq_lens, split_rows=None):
    n = x.shape[0]
    if split_rows is None:
        first_b = None
        out_specs = pl.BlockSpec((SORT_TILE, D_MODEL), lambda i, sq: (i, 0))
        out_shape = jax.ShapeDtypeStruct((n, D_MODEL), f32)
    else:
        first_b = split_rows // SORT_TILE
        out_specs = [pl.BlockSpec((SORT_TILE, D_MODEL), lambda i, sq: (jnp.minimum(i, first_b - 1), 0)),
                     pl.BlockSpec((SORT_TILE, D_MODEL), lambda i, sq: (jnp.maximum(i - first_b, 0), 0))]
        out_shape = [jax.ShapeDtypeStruct((split_rows, D_MODEL), f32),
                     jax.ShapeDtypeStruct((n - split_rows, D_MODEL), f32)]
    seq, _, _ = _tile_meta(seq_lens, SORT_TILE)
    row = lambda i, sq: (i, 0)
    flat = lambda i, sq: (i,)
    const = lambda i, sq: (0, 0)
    grid_spec = pltpu.PrefetchScalarGridSpec(
        num_scalar_prefetch=1,
        grid=(n // SORT_TILE,),
        in_specs=[pl.BlockSpec((TOP_K * SORT_TILE,), flat, memory_space=pltpu.SMEM),
                  pl.BlockSpec((TOP_K * SORT_TILE,), flat, memory_space=pltpu.SMEM),
                  pl.BlockSpec((SORT_TILE, D_MODEL), row),
                  pl.BlockSpec((LOCAL_ROWS * SUBLANES, LANES), row),
                  pl.BlockSpec((SEQ_PAD, N_MOD * D_MODEL), const),
                  pl.BlockSpec((4, D_MODEL), const)],
        out_specs=out_specs,
        scratch_shapes=[pltpu.VMEM((SORT_TILE * SUBLANES, LANES), f32)],
    )
    return pl.pallas_call(
        functools.partial(_combine_kernel, first_b),
        grid_spec=grid_spec,
        out_shape=out_shape,
        compiler_params=_cparams(("arbitrary",)),
        name="combine",
    )(seq, lrow, gflat, x, ys, ada_l, ng)


def _chunk_plan(tile_counts):
    nt = tile_counts.shape[0]
    nch = (tile_counts + CHUNK - 1) // CHUNK
    rows = nch * CHUNK
    loff = jnp.cumsum(rows, axis=1) - rows
    cend = jnp.cumsum(nch, axis=0)
    cstart = cend - nch
    total = cend[-1]
    nblk_e = (total + BLOCK_CHUNKS - 1) // BLOCK_CHUNKS
    bend = jnp.cumsum(nblk_e)
    bstart = bend - nblk_e
    n_used = bend[-1].reshape(1)
    max_chunks = nt * (TOP_K * SORT_TILE // CHUNK + N_EXPERTS)
    nblk = max_chunks // BLOCK_CHUNKS + N_EXPERTS
    b = jnp.arange(nblk, dtype=i32)
    blk_expert = jnp.minimum(jnp.sum((b[:, None] >= bend[None, :]).astype(i32), axis=1), N_EXPERTS - 1)
    onehot = (blk_expert[:, None] == jnp.arange(N_EXPERTS, dtype=i32)).astype(f32)

    def of_block(per_expert):
        table = per_expert.astype(f32).reshape(-1, N_EXPERTS).T
        return jnp.dot(onehot, table, precision=lax.Precision.HIGHEST).astype(i32)

    cend_b, cstart_b, loff_b = of_block(cend), of_block(cstart), of_block(loff)
    total_b, bstart_b = of_block(total)[:, 0], of_block(bstart)[:, 0]
    q = (b - bstart_b)[:, None] * BLOCK_CHUNKS + jnp.arange(BLOCK_CHUNKS, dtype=i32)[None, :]
    valid = (q < total_b[:, None]) & (b < n_used[0])[:, None]
    tile = jnp.minimum(jnp.sum((cend_b[:, None, :] <= q[:, :, None]).astype(i32), axis=2), nt - 1)
    pick = tile[:, :, None] == jnp.arange(nt, dtype=i32)
    cs = jnp.sum(jnp.where(pick, cstart_b[:, None, :], 0), axis=2)
    lo = jnp.sum(jnp.where(pick, loff_b[:, None, :], 0), axis=2)
    row = tile * LOCAL_ROWS + lo + (q - cs) * CHUNK
    spare = (nt * LOCAL_ROWS + (b % 2)[:, None] * SLOT_BLOCK
             + jnp.arange(BLOCK_CHUNKS, dtype=i32)[None, :] * CHUNK)
    shape = (nblk, 1, BLOCK_CHUNKS)
    chunk_src = jnp.where(valid, row, 0).astype(i32).reshape(shape)
    chunk_dst = jnp.where(valid, row, spare).astype(i32).reshape(shape)
    return chunk_src, chunk_dst, blk_expert.astype(i32), n_used.astype(i32)


def _trunk(x_parts, c_all, seq_lens, ada_w, ada_b, norm_g, w_in, pool_w, pool_scale, rpb, conv_w,
           w_branch, w_out, router_w, router_b, expert_w_gu, expert_b_gu, expert_w_down, expert_b_down):
    depth = ada_w.shape[0]
    split_rows = x_parts[0].shape[0]
    ada = _ada(c_all, ada_w, ada_b)
    x = tuple(x_parts)
    for l in range(depth):
        ada_l = ada[l]
        ng = norm_g[l]
        zs = _inproj(x, ada_l, ng[0:1], w_in[l].astype(bf16), seq_lens)
        rw = jnp.pad(router_w[l], ((0, 0), (0, LANES - N_EXPERTS)))
        rw_hi = rw.astype(bf16)
        rw_lo = (rw - rw_hi.astype(f32)).astype(bf16)
        x_mid, h2, logits_t = _mixer(
            x, zs, ada_l, ng, pool_w[l].astype(bf16), pool_scale[l].reshape(1, POOL_WIDTH),
            _bias_table(rpb[l]), conv_w[l], w_branch[l].astype(bf16), w_out[l].astype(bf16),
            rw_hi, rw_lo,
            jnp.pad(router_b[l], (0, LANES - N_EXPERTS)).reshape(1, LANES), seq_lens)
        lpos, gates, tile_counts = _router(logits_t)
        chunk_src, chunk_dst, blk_expert, n_used = _chunk_plan(tile_counts[:, :, 0])
        lrow = lpos.reshape(-1)
        xs = _local_sort(h2, lrow)
        ys = _experts(xs, chunk_src, chunk_dst, blk_expert, n_used, l, expert_w_gu, expert_b_gu,
                      expert_w_down, expert_b_down)
        split = split_rows if l == depth - 1 else None
        out = _combine(x_mid, ys, lrow, gates.reshape(-1), ada_l, ng, seq_lens, split)
        x = (out,) if split is None else tuple(out)
    return x


def kernel(x_prompt, x_sample, c_prompt, c_sample, ada_w, ada_b, norm_g, w_in, pool_w, pool_scale, rpb,
           conv_w, w_branch, w_out, router_w, router_b, expert_w_gu, expert_b_gu, expert_w_down,
           expert_b_down):
    bp, lp, d = x_prompt.shape
    bs, ls, _ = x_sample.shape
    assert d == D_MODEL and bp + bs <= SEQ_PAD
    assert lp % IN_TILE == 0 and ls % IN_TILE == 0 and lp >= 2 * TILE and ls >= 2 * TILE
    x_parts = (x_prompt.reshape(bp * lp, d), x_sample.reshape(bs * ls, d))
    c_all = jnp.concatenate([c_prompt, c_sample, jnp.zeros((SEQ_PAD - bp - bs, d), f32)], axis=0)
    seq_lens = ((bp, lp), (bs, ls))
    yp, ys = _trunk(x_parts, c_all, seq_lens, ada_w, ada_b, norm_g, w_in, pool_w, pool_scale, rpb, conv_w,
                    w_branch, w_out, router_w, router_b, expert_w_gu, expert_b_gu, expert_w_down,
                    expert_b_down)
    return (yp.reshape(bp, lp, d), ys.reshape(bs, ls, d))
```
